```python
import jax
import jax.numpy as jnp
from jax import lax
import numpy as np

D_MODEL = 1024
BATCH = 8
SEQ = 2048
DEPTH = 4

N_MIXERS = 3
HEAD_DIM = 64
N_HEADS = D_MODEL // HEAD_DIM
D_FF = 4 * D_MODEL
N_SHIFT_MIX = 6
DECAY_LORA = 64
ICLR_LORA = 64
VRES_LORA = 32
GATE_LORA = 160
CONV_WIDTH = 3
SB_BLOCK = 128
RMS_EPS = 1e-6
GN_EPS = 64e-5
N_RWKV = max((DEPTH + 2) // N_MIXERS, 1)
N_CONV = max((DEPTH + 1) // N_MIXERS, 1)
N_SB = max(DEPTH // N_MIXERS, 1)
N_VRES = max(N_RWKV - 1, 1)
N_KEYS = 40

kernel_name = 'hybrid_rwkv7_shortconv_stickbreaking'


def rms_norm(x, g):
    xf = x.astype(jnp.float32)
    y = xf * lax.rsqrt(jnp.mean(xf * xf, axis=-1, keepdims=True) + RMS_EPS)
    return (y * g.astype(jnp.float32)).astype(x.dtype)


def token_shift(x):
    return jnp.pad(x, ((0, 0), (1, 0), (0, 0)))[:, :-1]


def split_heads(t):
    b, s, _ = t.shape
    return t.reshape(b, s, N_HEADS, HEAD_DIM)


def wkv7_scan(r, w, k, v, a, b):
    bsz, _, nh, n = r.shape

    def step(state, inp):
        r_t, w_t, k_t, v_t, a_t, b_t = inp
        sa = jnp.einsum('bhvk,bhk->bhv', state, a_t)
        state = (state * w_t[:, :, None, :]
                 + sa[..., None] * b_t[:, :, None, :]
                 + v_t[..., None] * k_t[:, :, None, :])
        y_t = jnp.einsum('bhvk,bhk->bhv', state, r_t)
        return state, y_t

    xs = tuple(jnp.moveaxis(t, 1, 0) for t in (r, w, k, v, a, b))
    state0 = jnp.zeros((bsz, nh, n, n), jnp.float32)
    _, ys = lax.scan(step, state0, xs)
    return jnp.moveaxis(ys, 0, 1)


def rwkv7_time_mix(h, mu, w_r, w_k, w_v, w_o, w0, w1, w2, a0, a1, a2,
                   g1, g2, k_k, k_a, r_k, lnx_w, lnx_b, v_first, vres):
    f32 = jnp.float32
    bsz, s, d = h.shape
    xx = token_shift(h) - h
    xr = h + xx * mu[0]
    xw = h + xx * mu[1]
    xk = h + xx * mu[2]
    xv = h + xx * mu[3]
    xa = h + xx * mu[4]
    xg = h + xx * mu[5]
    r = xr @ w_r
    k = xk @ w_k
    v = xv @ w_v
    log_w = -jax.nn.softplus(-(w0 + jnp.tanh(xw @ w1) @ w2)) - 0.5
    decay = jnp.exp(-jnp.exp(log_w.astype(f32)))
    a = jax.nn.sigmoid(a0 + (xa @ a1) @ a2)
    g = jax.nn.sigmoid(xg @ g1) @ g2
    if vres is None:
        v_first = v
    else:
        v0, v1, v2 = vres
        v = v + (v_first - v) * jax.nn.sigmoid(v0 + (xv @ v1) @ v2)
    kk = split_heads((k * k_k).astype(f32))
    kk = kk * lax.rsqrt(jnp.maximum(jnp.sum(kk * kk, axis=-1, keepdims=True), 1e-24))
    k = k * (1.0 + (a - 1.0) * k_a)
    rh = split_heads(r.astype(f32))
    kh = split_heads(k.astype(f32))
    vh = split_heads(v.astype(f32))
    ah = split_heads(a.astype(f32))
    y = wkv7_scan(rh, split_heads(decay), kh, vh, -kk, kk * ah)
    mean = jnp.mean(y, axis=-1, keepdims=True)
    var = jnp.mean(jnp.square(y - mean), axis=-1, keepdims=True)
    y = ((y - mean) * lax.rsqrt(var + GN_EPS)).reshape(bsz, s, d)
    y = y * lnx_w.astype(f32) + lnx_b.astype(f32)
    bonus = jnp.sum(rh * kh * r_k.astype(f32), axis=-1, keepdims=True) * vh
    y = y + bonus.reshape(bsz, s, d)
    out = (y.astype(h.dtype) * g) @ w_o
    return out, v_first


def short_conv_mix(h, w_in, conv_w, w_out):
    s = h.shape[1]
    proj = h @ w_in
    d = proj.shape[-1] // 3
    b_gate = proj[..., :d]
    c_gate = proj[..., d:2 * d]
    u = proj[..., 2 * d:]
    z = c_gate * u
    zp = jnp.pad(z, ((0, 0), (CONV_WIDTH - 1, 0), (0, 0)))
    zc = (zp[:, 0:s] * conv_w[0]
          + zp[:, 1:s + 1] * conv_w[1]
          + zp[:, 2:s + 2] * conv_w[2])
    return (b_gate * zc) @ w_out


def stick_breaking_mix(h, w_qkv, q_gain, k_gain, w_o):
    f32 = jnp.float32
    bsz, s, d = h.shape
    proj = h @ w_qkv
    q = split_heads(proj[..., :d])
    k = split_heads(proj[..., d:2 * d])
    v = split_heads(proj[..., 2 * d:])
    q = rms_norm(q, q_gain)
    k = rms_norm(k, k_gain)
    scale = HEAD_DIM ** -0.5
    outs = []
    for blk in range(s // SB_BLOCK):
        q0 = blk * SB_BLOCK
        q1 = q0 + SB_BLOCK
        z = jnp.einsum('bthd,bshd->bhts', q[:, q0:q1], k[:, :q1]).astype(f32) * scale
        t_idx = q0 + jnp.arange(SB_BLOCK)[:, None]
        s_idx = jnp.arange(q1)[None, :]
        causal = s_idx < t_idx
        log_keep = jnp.where(causal, jax.nn.log_sigmoid(-z), 0.0)
        rev_cum = jnp.flip(jnp.cumsum(jnp.flip(log_keep, axis=-1), axis=-1), axis=-1)
        tail = rev_cum - log_keep
        att = jnp.where(causal, jnp.exp(jax.nn.log_sigmoid(z) + tail), 0.0)
        outs.append(jnp.einsum('bhts,bshd->bthd', att.astype(v.dtype), v[:, :q1]))
    o = jnp.concatenate(outs, axis=1).reshape(bsz, s, d)
    return o @ w_o


def sq_relu_mlp(h, w_up, w_down):
    return jnp.square(jax.nn.relu(h @ w_up)) @ w_down


def setup_inputs(seed: int = 0) -> dict:
    key = jax.random.key(seed)
    ks = jax.random.split(key, N_KEYS)
    D, R, V = D_MODEL, N_RWKV, N_VRES

    def nrm(i, shape, scale):
        return jax.random.normal(ks[i], shape, jnp.float32) * scale

    def unif(i, shape, lo, hi):
        return jax.random.uniform(ks[i], shape, jnp.float32, lo, hi)

    return {
        'x': nrm(0, (BATCH, SEQ, D), 1.0),
        'mix_norm': 1.0 + nrm(1, (DEPTH, D), 0.02),
        'mlp_norm': 1.0 + nrm(2, (DEPTH, D), 0.02),
        'mlp_up': nrm(3, (DEPTH, D, D_FF), D ** -0.5),
        'mlp_down': nrm(4, (DEPTH, D_FF, D), D_FF ** -0.5),
        'rwkv_mu': unif(5, (R, N_SHIFT_MIX, D), 0.0, 1.0),
        'rwkv_w_r': nrm(6, (R, D, D), D ** -0.5),
        'rwkv_w_k': nrm(7, (R, D, D), D ** -0.5),
        'rwkv_w_v': nrm(8, (R, D, D), D ** -0.5),
        'rwkv_w_o': nrm(9, (R, D, D), D ** -0.5),
        'rwkv_decay_w0': unif(10, (R, D), -5.5, 0.5),
        'rwkv_decay_w1': nrm(11, (R, D, DECAY_LORA), D ** -0.5),
        'rwkv_decay_w2': nrm(12, (R, DECAY_LORA, D), 0.5 * DECAY_LORA ** -0.5),
        'rwkv_iclr_a0': nrm(13, (R, D), 0.1),
        'rwkv_iclr_a1': nrm(14, (R, D, ICLR_LORA), D ** -0.5),
        'rwkv_iclr_a2': nrm(15, (R, ICLR_LORA, D), 0.5 * ICLR_LORA ** -0.5),
        'rwkv_gate_g1': nrm(16, (R, D, GATE_LORA), D ** -0.5),
        'rwkv_gate_g2': nrm(17, (R, GATE_LORA, D), GATE_LORA ** -0.5),
        'rwkv_k_k': 0.85 + nrm(18, (R, D), 0.05),
        'rwkv_k_a': 1.0 + nrm(19, (R, D), 0.05),
        'rwkv_r_k': nrm(20, (R, N_HEADS, HEAD_DIM), 0.1),
        'rwkv_lnx_w': 1.0 + nrm(21, (R, D), 0.02),
        'rwkv_lnx_b': nrm(22, (R, D), 0.02),
        'rwkv_vres_v0': nrm(23, (V, D), 0.1),
        'rwkv_vres_v1': nrm(24, (V, D, VRES_LORA), D ** -0.5),
        'rwkv_vres_v2': nrm(25, (V, VRES_LORA, D), 0.5 * VRES_LORA ** -0.5),
        'conv_w_in': nrm(26, (N_CONV, D, 3 * D), D ** -0.5),
        'conv_w': nrm(27, (N_CONV, CONV_WIDTH, D), CONV_WIDTH ** -0.5),
        'conv_w_out': nrm(28, (N_CONV, D, D), D ** -0.5),
        'sb_w_qkv': nrm(29, (N_SB, D, 3 * D), D ** -0.5),
        'sb_q_norm': 1.0 + nrm(30, (N_SB, HEAD_DIM), 0.02),
        'sb_k_norm': 1.0 + nrm(31, (N_SB, HEAD_DIM), 0.02),
        'sb_w_o': nrm(32, (N_SB, D, D), D ** -0.5),
    }


def reference(x, mix_norm, mlp_norm, mlp_up, mlp_down,
              rwkv_mu, rwkv_w_r, rwkv_w_k, rwkv_w_v, rwkv_w_o,
              rwkv_decay_w0, rwkv_decay_w1, rwkv_decay_w2,
              rwkv_iclr_a0, rwkv_iclr_a1, rwkv_iclr_a2,
              rwkv_gate_g1, rwkv_gate_g2, rwkv_k_k, rwkv_k_a, rwkv_r_k,
              rwkv_lnx_w, rwkv_lnx_b, rwkv_vres_v0, rwkv_vres_v1, rwkv_vres_v2,
              conv_w_in, conv_w, conv_w_out,
              sb_w_qkv, sb_q_norm, sb_k_norm, sb_w_o):
    h = x
    v_first = None
    for i in range(DEPTH):
        kind = i % N_MIXERS
        j = i // N_MIXERS
        u = rms_norm(h, mix_norm[i])
        if kind == 0:
            if j == 0:
                vres = None
            else:
                vres = (rwkv_vres_v0[j - 1], rwkv_vres_v1[j - 1], rwkv_vres_v2[j - 1])
            mixed, v_first = rwkv7_time_mix(
                u, rwkv_mu[j], rwkv_w_r[j], rwkv_w_k[j], rwkv_w_v[j], rwkv_w_o[j],
                rwkv_decay_w0[j], rwkv_decay_w1[j], rwkv_decay_w2[j],
                rwkv_iclr_a0[j], rwkv_iclr_a1[j], rwkv_iclr_a2[j],
                rwkv_gate_g1[j], rwkv_gate_g2[j], rwkv_k_k[j], rwkv_k_a[j], rwkv_r_k[j],
                rwkv_lnx_w[j], rwkv_lnx_b[j], v_first, vres)
        elif kind == 1:
            mixed = short_conv_mix(u, conv_w_in[j], conv_w[j], conv_w_out[j])
        else:
            mixed = stick_breaking_mix(u, sb_w_qkv[j], sb_q_norm[j], sb_k_norm[j], sb_w_o[j])
        h = h + mixed
        h = h + sq_relu_mlp(rms_norm(h, mlp_norm[i]), mlp_up[i], mlp_down[i])
    return h
```

```python
import functools

import jax
import jax.numpy as jnp
from jax import lax
from jax.experimental import pallas as pl
from jax.experimental.pallas import tpu as pltpu

F32 = jnp.float32
BF16 = jnp.bfloat16

HEAD_DIM = 64
HEAD_SHIFT = 6
LANES = 128
SUBLANES = 8
RMS_EPS = 1e-6
GN_EPS = 64e-5
KK_EPS = 1e-24
SB_SCALE = HEAD_DIM ** -0.5

V7X_VMEM_BYTES = 64 * 1024 * 1024
VMEM_LIMIT_CAP = V7X_VMEM_BYTES - 8 * 1024 * 1024

MLP_TM = 512
MLP_TF = 512
RWKV_TS = 256
CONV_TS = 256
QKV_TM = 256
WKV_CHUNK = 64
SB_TQ = 128
SB_TK = 256


def _vmem_limit(block_bytes, scratch_bytes, temp_bytes):
    return int(min(2 * block_bytes + scratch_bytes + temp_bytes, VMEM_LIMIT_CAP))


def _nbytes(shape, dtype):
    n = 1
    for s in shape:
        n *= s
    return n * jnp.dtype(dtype).itemsize


def _dot(a, b):
    return jnp.dot(a, b, preferred_element_type=F32)


def _dot_nt(a, b):
    return lax.dot_general(a, b, (((1,), (1,)), ((), ())), preferred_element_type=F32)


def _dot_tn(a, b):
    return lax.dot_general(a, b, (((0,), (0,)), ((), ())), preferred_element_type=F32)


def _split(x, pieces):
    out = []
    for _ in range(pieces - 1):
        hi = x.astype(BF16)
        out.append(hi)
        x = x - hi.astype(F32)
    out.append(x.astype(BF16))
    return out


def _dot_exact_rhs(x, m_bf16, pieces, dot=_dot):
    acc = None
    for p in _split(x, pieces):
        t = dot(p, m_bf16)
        acc = t if acc is None else acc + t
    return acc


def _dot_exact_lhs(m_bf16, x, pieces):
    acc = None
    for p in _split(x, pieces):
        t = _dot(m_bf16, p)
        acc = t if acc is None else acc + t
    return acc


def _mm(a, b, passes, dot=_dot):
    if passes == 1:
        return dot(a.astype(BF16), b.astype(BF16))
    ah, al = _split(a, 2)
    bh, bl = _split(b, 2)
    return dot(ah, bh) + (dot(al, bh) + dot(ah, bl))


def _rms(x, g):
    ms = jnp.mean(x * x, axis=-1, keepdims=True)
    return x * lax.rsqrt(ms + RMS_EPS) * g


def _sigmoid(x):
    return 1.0 / (1.0 + jnp.exp(-x))


def _softplus(x):
    return jnp.maximum(x, 0.0) + jnp.log1p(jnp.exp(-jnp.abs(x)))


def _head_sum_matrix(n):
    r = lax.broadcasted_iota(jnp.int32, (n, n), 0) >> HEAD_SHIFT
    c = lax.broadcasted_iota(jnp.int32, (n, n), 1) >> HEAD_SHIFT
    return jnp.where(r == c, 1.0, 0.0).astype(BF16)


def _shift_rows(x, carry8, n):
    rows = lax.broadcasted_iota(jnp.int32, (x.shape[0], 1), 0)
    out = pltpu.roll(x, n, axis=0)
    for i in range(n):
        out = jnp.where(rows == i, carry8[SUBLANES - n + i:SUBLANES - n + i + 1, :], out)
    return out


def _rwkv_in_kernel(*refs, has_vres):
    if has_vres:
        (h_ref, vec_ref, mu_ref, wr_ref, wk_ref, wv_ref, w1_ref, w2_ref, a1_ref, a2_ref,
         g1_ref, g2_ref, v1_ref, v2_ref, vf_ref,
         r_out, k_out, v_out, lw_out, a_out, g_out, carry_ref) = refs
    else:
        (h_ref, vec_ref, mu_ref, wr_ref, wk_ref, wv_ref, w1_ref, w2_ref, a1_ref, a2_ref,
         g1_ref, g2_ref,
         r_out, k_out, v_out, lw_out, a_out, g_out, carry_ref) = refs

    @pl.when(pl.program_id(1) == 0)
    def _():
        carry_ref[...] = jnp.zeros_like(carry_ref)

    ts = h_ref.shape[0]
    u = _rms(h_ref[...], vec_ref[0:1, :])
    prev = _shift_rows(u, carry_ref[...], 1)
    carry_ref[...] = u[ts - SUBLANES:, :]
    xx = prev - u

    def mix(i):
        return (u + xx * mu_ref[i:i + 1, :]).astype(BF16)

    xr, xw, xk, xv, xa, xg = (mix(i) for i in range(6))
    r_out[...] = _dot(xr, wr_ref[...])
    k_out[...] = _dot(xk, wk_ref[...])
    v = _dot(xv, wv_ref[...])

    dw = vec_ref[1:2, :] + _dot(jnp.tanh(_dot(xw, w1_ref[...])).astype(BF16), w2_ref[...])
    log_w = -_softplus(-dw) - 0.5
    lw_out[...] = -jnp.exp(log_w)
    a_out[...] = _sigmoid(vec_ref[2:3, :] + _dot(_dot(xa, a1_ref[...]).astype(BF16), a2_ref[...]))
    g_out[...] = _dot(_sigmoid(_dot(xg, g1_ref[...])).astype(BF16), g2_ref[...])
    if has_vres:
        mixv = _sigmoid(vec_ref[3:4, :] + _dot(_dot(xv, v1_ref[...]).astype(BF16), v2_ref[...]))
        v = v + (vf_ref[...] - v) * mixv
    v_out[...] = v


def _rwkv_in(h, vecs, mu8, wr, wk, wv, w1, w2, a1, a2, g1, g2, vres):
    b, s, d = h.shape
    ts = min(RWKV_TS, s)
    tok = pl.BlockSpec((None, ts, d), lambda i, j: (i, j, 0))

    def full(x):
        return pl.BlockSpec(x.shape, lambda i, j: (0,) * x.ndim)

    weights = [vecs, mu8, wr, wk, wv, w1, w2, a1, a2, g1, g2]
    args = [h] + weights
    in_specs = [tok] + [full(w) for w in weights]
    if vres is not None:
        v1, v2, v_first = vres
        args += [v1, v2, v_first]
        in_specs += [full(v1), full(v2), tok]
    tok_bytes = _nbytes((ts, d), F32)
    block_bytes = sum(_nbytes(w.shape, w.dtype) for w in args[1:1 + len(weights) + (2 if vres else 0)])
    block_bytes += tok_bytes * (7 + (1 if vres else 0))
    out = jax.ShapeDtypeStruct((b, s, d), F32)
    return pl.pallas_call(
        functools.partial(_rwkv_in_kernel, has_vres=vres is not None),
        grid=(b, s // ts),
        in_specs=in_specs,
        out_specs=[tok] * 6,
        out_shape=[out] * 6,
        scratch_shapes=[pltpu.VMEM((SUBLANES, d), F32)],
        compiler_params=pltpu.CompilerParams(
            dimension_semantics=("parallel", "arbitrary"),
            vmem_limit_bytes=_vmem_limit(block_bytes, _nbytes((SUBLANES, d), F32), 12 * tok_bytes)),
        name="rwkv_in",
    )(*args)


def _wkv_kernel(r_ref, lw_ref, k_ref, v_ref, a_ref, pv_ref, y_ref, s_ref, *, passes):
    c = r_ref.shape[0]
    n = 2 * c

    @pl.when(pl.program_id(2) == 0)
    def _():
        s_ref[...] = jnp.zeros_like(s_ref)

    r = r_ref[...]
    lw = lw_ref[...]
    k = k_ref[...]
    v = v_ref[...]
    a = a_ref[...]
    k_k = pv_ref[0:1, :]
    k_a = pv_ref[1:2, :]
    r_k = pv_ref[2:3, :]
    lnx_w = pv_ref[3:4, :]
    lnx_b = pv_ref[4:5, :]

    hsum = _head_sum_matrix(LANES)
    kk = k * k_k
    kk = kk * lax.rsqrt(jnp.maximum(_dot_exact_rhs(kk * kk, hsum, 2), KK_EPS))
    k = k * (1.0 + (a - 1.0) * k_a)
    a_rm = -kk
    b_rm = kk * a

    ti = lax.broadcasted_iota(jnp.int32, (c, c), 0)
    si = lax.broadcasted_iota(jnp.int32, (c, c), 1)
    lincl = jnp.where(si <= ti, 1.0, 0.0).astype(BF16)
    cum = _dot_exact_lhs(lincl, lw, 3)
    g_incl = jnp.exp(cum)
    g_excl = jnp.exp(cum - lw)
    g_inv = jnp.exp(-cum)
    g_end = g_incl[c - 1:c, :]

    row = lax.broadcasted_iota(jnp.int32, (n, LANES), 0)
    lane = lax.broadcasted_iota(jnp.int32, (n, LANES), 1)
    own = (row >> (c.bit_length() - 1)) == (lane >> HEAD_SHIFT)

    def stack(x):
        return jnp.where(own, jnp.concatenate([x, x], axis=0), 0.0)

    at = stack(a_rm * g_excl)
    rt = stack(r * g_incl)
    kt = stack(k * g_inv)
    bt = stack(b_rm * g_inv)
    vs = stack(v)

    ri = lax.broadcasted_iota(jnp.int32, (n, n), 0) & (c - 1)
    ci = lax.broadcasted_iota(jnp.int32, (n, n), 1) & (c - 1)
    strict = ci < ri
    incl = ci <= ri

    mm = functools.partial(_mm, passes=passes)
    mm_nt = functools.partial(_mm, passes=passes, dot=_dot_nt)
    mm_tn = functools.partial(_mm, passes=passes, dot=_dot_tn)

    s0 = s_ref[...]
    n_ab = jnp.where(strict, mm_nt(at, bt), 0.0)
    n_ak = jnp.where(strict, mm_nt(at, kt), 0.0)
    m_rb = jnp.where(incl, mm_nt(rt, bt), 0.0)
    m_rk = jnp.where(incl, mm_nt(rt, kt), 0.0)

    x = mm_nt(at, s0) + mm(n_ak, vs)
    p = n_ab
    steps = c.bit_length() - 1
    for i in range(steps):
        x = x + mm(p, x)
        if i + 1 < steps:
            p = mm(p, p)
    u = x

    yh = mm_nt(rt, s0) + mm(m_rk, vs) + mm(m_rb, u)
    y = yh[:c, :] + yh[c:, :]

    s_ref[...] = s0 * g_end + mm_tn(vs, kt * g_end) + mm_tn(u, bt * g_end)

    inv_n = 1.0 / HEAD_DIM
    mean = _dot_exact_rhs(y, hsum, 2) * inv_n
    yc = y - mean
    var = _dot_exact_rhs(yc * yc, hsum, 2) * inv_n
    yn = yc * lax.rsqrt(var + GN_EPS) * lnx_w + lnx_b
    bonus = _dot_exact_rhs(r * k * r_k, hsum, 2) * v
    y_ref[...] = yn + bonus


def _wkv(r, lw, k, v, a, pvec, passes):
    b, s, d = r.shape
    c = WKV_CHUNK
    tok = pl.BlockSpec((None, c, LANES), lambda i, p, j: (i, j, p))
    par = pl.BlockSpec((SUBLANES, LANES), lambda i, p, j: (0, p))
    tok_bytes = _nbytes((c, LANES), F32)
    sq_bytes = _nbytes((2 * c, 2 * c), F32)
    return pl.pallas_call(
        functools.partial(_wkv_kernel, passes=passes),
        grid=(b, d // LANES, s // c),
        in_specs=[tok] * 5 + [par],
        out_specs=tok,
        out_shape=jax.ShapeDtypeStruct((b, s, d), F32),
        scratch_shapes=[pltpu.VMEM((LANES, LANES), F32)],
        compiler_params=pltpu.CompilerParams(
            dimension_semantics=("parallel", "parallel", "arbitrary"),
            vmem_limit_bytes=_vmem_limit(7 * tok_bytes, sq_bytes, 64 * sq_bytes)),
        name="wkv",
    )(r, lw, k, v, a, pvec)


def _conv_in_kernel(h_ref, g_ref, win_ref, cw_ref, z_out, carry_ref):
    @pl.when(pl.program_id(1) == 0)
    def _():
        carry_ref[...] = jnp.zeros_like(carry_ref)

    ts, d = h_ref.shape
    u = _rms(h_ref[...], g_ref[...]).astype(BF16)
    proj = _dot(u, win_ref[...])
    b_gate = proj[:, :d]
    z = proj[:, d:2 * d] * proj[:, 2 * d:]
    carry = carry_ref[...]
    z1 = _shift_rows(z, carry, 1)
    z2 = _shift_rows(z, carry, 2)
    carry_ref[...] = z[ts - SUBLANES:, :]
    zc = z2 * cw_ref[0:1, :] + z1 * cw_ref[1:2, :] + z * cw_ref[2:3, :]
    z_out[...] = b_gate * zc


def _conv_in(h, g, w_in, cw8):
    b, s, d = h.shape
    ts = min(CONV_TS, s)
    tok = pl.BlockSpec((None, ts, d), lambda i, j: (i, j, 0))
    tok_bytes = _nbytes((ts, d), F32)
    block_bytes = 2 * tok_bytes + _nbytes(w_in.shape, BF16) + _nbytes(cw8.shape, F32)
    return pl.pallas_call(
        _conv_in_kernel,
        grid=(b, s // ts),
        in_specs=[tok,
                  pl.BlockSpec(g.shape, lambda i, j: (0, 0)),
                  pl.BlockSpec(w_in.shape, lambda i, j: (0, 0)),
                  pl.BlockSpec(cw8.shape, lambda i, j: (0, 0))],
        out_specs=tok,
        out_shape=jax.ShapeDtypeStruct((b, s, d), F32),
        scratch_shapes=[pltpu.VMEM((SUBLANES, d), F32)],
        compiler_params=pltpu.CompilerParams(
            dimension_semantics=("parallel", "arbitrary"),
            vmem_limit_bytes=_vmem_limit(block_bytes, _nbytes((SUBLANES, d), F32), 10 * tok_bytes)),
        name="conv_in",
    )(h, g, w_in, cw8)


def _sb_qkv_kernel(h_ref, g_ref, w_ref, qg_ref, kg_ref, q_out, k_out, v_out):
    d = h_ref.shape[1]
    u = _rms(h_ref[...], g_ref[...]).astype(BF16)
    proj = _dot(u, w_ref[...])
    hsum = _head_sum_matrix(2 * LANES)
    inv_n = 1.0 / HEAD_DIM

    def head_norm(x, gain):
        cols = []
        for j in range(0, d, 2 * LANES):
            xs = x[:, j:j + 2 * LANES]
            ms = _dot_exact_rhs(xs * xs, hsum, 2) * inv_n
            cols.append(xs * lax.rsqrt(ms + RMS_EPS))
        return jnp.concatenate(cols, axis=1) * gain

    q_out[...] = head_norm(proj[:, :d], qg_ref[...])
    k_out[...] = head_norm(proj[:, d:2 * d], kg_ref[...])
    v_out[...] = proj[:, 2 * d:]


def _sb_qkv(h2, g, w_qkv, qg, kg):
    t, d = h2.shape
    tm = min(QKV_TM, t)
    tok = pl.BlockSpec((tm, d), lambda i: (i, 0))
    vec = pl.BlockSpec((1, d), lambda i: (0, 0))
    tok_bytes = _nbytes((tm, d), F32)
    block_bytes = 4 * tok_bytes + _nbytes(w_qkv.shape, BF16) + 3 * _nbytes((1, d), F32)
    out = jax.ShapeDtypeStruct((t, d), F32)
    return pl.pallas_call(
        _sb_qkv_kernel,
        grid=(t // tm,),
        in_specs=[tok, vec, pl.BlockSpec(w_qkv.shape, lambda i: (0, 0)), vec, vec],
        out_specs=[tok] * 3,
        out_shape=[out] * 3,
        compiler_params=pltpu.CompilerParams(
            dimension_semantics=("parallel",),
            vmem_limit_bytes=_vmem_limit(block_bytes, 0, 10 * tok_bytes)),
        name="sb_qkv",
    )(h2, g, w_qkv, qg, kg)


def _sb_attn_kernel(q_ref, k_ref, v_ref, o_ref, acc_ref, run_ref):
    tq = q_ref.shape[0]
    tk = SB_TK
    qb = pl.program_id(2)
    n_kt = (qb * tq + tq + tk - 1) // tk

    lane = lax.broadcasted_iota(jnp.int32, (tq, LANES), 1)
    q = q_ref[...]
    qh = [jnp.where(lane < HEAD_DIM, q, 0.0).astype(BF16), jnp.where(lane >= HEAD_DIM, q, 0.0).astype(BF16)]
    t_idx = qb * tq + lax.broadcasted_iota(jnp.int32, (tq, 1), 0)
    s_loc = lax.broadcasted_iota(jnp.int32, (1, tk), 1)
    ri = lax.broadcasted_iota(jnp.int32, (tk, tk), 0)
    ci = lax.broadcasted_iota(jnp.int32, (tk, tk), 1)
    suffix = jnp.where(ri >= ci, 1.0, 0.0).astype(BF16)

    acc_ref[...] = jnp.zeros_like(acc_ref)
    run_ref[...] = jnp.zeros_like(run_ref)

    def body(it, carry):
        jj = n_kt - 1 - it
        start = pl.multiple_of(jj * tk, tk)
        ks = k_ref[pl.ds(start, tk), :].astype(BF16)
        vs = v_ref[pl.ds(start, tk), :].astype(BF16)
        causal = (start + s_loc) < t_idx
        for h in range(2):
            z = _dot_nt(qh[h], ks) * SB_SCALE
            sp = _softplus(z)
            log_keep = jnp.where(causal, -sp, 0.0)
            rev_cum = _dot_exact_rhs(log_keep, suffix, 2)
            run = run_ref[h]
            att = jnp.where(causal, jnp.exp((z - sp) + (rev_cum - log_keep) + run), 0.0)
            acc_ref[h] += _dot(att.astype(BF16), vs)
            run_ref[h] = run + rev_cum[:, 0:1]
        return carry

    lax.fori_loop(0, n_kt, body, 0)
    o_ref[...] = jnp.where(lane >= HEAD_DIM, acc_ref[1], acc_ref[0])


def _sb_attn(q, k, v):
    b, s, d = q.shape
    tq = SB_TQ
    qspec = pl.BlockSpec((None, tq, LANES), lambda i, p, j: (i, j, p))
    kspec = pl.BlockSpec((None, s, LANES), lambda i, p, j: (i, 0, p))
    block_bytes = 2 * _nbytes((tq, LANES), F32) + 2 * _nbytes((s, LANES), F32)
    scratch = [pltpu.VMEM((2, tq, LANES), F32), pltpu.VMEM((2, tq, 1), F32)]
    scratch_bytes = _nbytes((2, tq, LANES), F32) * 2
    return pl.pallas_call(
        _sb_attn_kernel,
        grid=(b, d // LANES, s // tq),
        in_specs=[qspec, kspec, kspec],
        out_specs=qspec,
        out_shape=jax.ShapeDtypeStruct((b, s, d), F32),
        scratch_shapes=scratch,
        compiler_params=pltpu.CompilerParams(
            dimension_semantics=("parallel", "parallel", "arbitrary"),
            vmem_limit_bytes=_vmem_limit(block_bytes, scratch_bytes, 32 * _nbytes((tq, SB_TK), F32))),
        name="sb_attn",
    )(q, k, v)


def _outproj_mlp_kernel(*refs, has_gate):
    if has_gate:
        h_ref, z_ref, zg_ref, wo_ref, g_ref, wup_ref, wdn_ref, o_ref, xn_ref = refs
    else:
        h_ref, z_ref, wo_ref, g_ref, wup_ref, wdn_ref, o_ref, xn_ref = refs

    @pl.when(pl.program_id(1) == 0)
    def _():
        z = z_ref[...]
        if has_gate:
            z = z * zg_ref[...]
        h1 = h_ref[...] + _dot(z.astype(BF16), wo_ref[...])
        o_ref[...] = h1
        xn_ref[...] = _rms(h1, g_ref[...]).astype(BF16)

    act = jnp.square(jnp.maximum(_dot(xn_ref[...], wup_ref[...]), 0.0)).astype(BF16)
    o_ref[...] += _dot(act, wdn_ref[...])


def _outproj_mlp(h2, z, zg, w_o, g, w_up, w_dn):
    t, d = h2.shape
    f = w_up.shape[1]
    tm = min(MLP_TM, t)
    tf = min(MLP_TF, f)
    tok = pl.BlockSpec((tm, d), lambda i, j: (i, 0))
    acts = [h2, z] + ([zg] if zg is not None else [])
    in_specs = [tok] * len(acts) + [
        pl.BlockSpec((d, d), lambda i, j: (0, 0)),
        pl.BlockSpec((1, d), lambda i, j: (0, 0)),
        pl.BlockSpec((d, tf), lambda i, j: (0, j)),
        pl.BlockSpec((tf, d), lambda i, j: (j, 0)),
    ]
    tok_bytes = _nbytes((tm, d), F32)
    block_bytes = (len(acts) + 1) * tok_bytes + _nbytes((d, d), BF16) + 2 * _nbytes((d, tf), BF16)
    return pl.pallas_call(
        functools.partial(_outproj_mlp_kernel, has_gate=zg is not None),
        grid=(t // tm, f // tf),
        in_specs=in_specs,
        out_specs=tok,
        out_shape=jax.ShapeDtypeStruct((t, d), F32),
        scratch_shapes=[pltpu.VMEM((tm, d), BF16)],
        compiler_params=pltpu.CompilerParams(
            dimension_semantics=("parallel", "arbitrary"),
            vmem_limit_bytes=_vmem_limit(block_bytes, _nbytes((tm, d), BF16),
                                         2 * tok_bytes + 2 * _nbytes((tm, tf), F32))),
        name="outproj_mlp",
    )(*acts, w_o, g, w_up, w_dn)


def _pad_cols(w, mult):
    pad = (-w.shape[1]) % mult
    return jnp.pad(w, ((0, 0), (0, pad)))


def _pad_rows(w, mult):
    pad = (-w.shape[0]) % mult
    return jnp.pad(w, ((0, pad), (0, 0)))


def _rows8(*rows):
    d = rows[0].shape[-1]
    out = jnp.zeros((SUBLANES, d), F32)
    return out.at[:len(rows)].set(jnp.stack([r.reshape(d) for r in rows]))


def _lora(w_in, w_out):
    return _pad_cols(w_in, LANES).astype(BF16), _pad_rows(w_out, LANES).astype(BF16)


WKV_PASSES = 1


def kernel(x, mix_norm, mlp_norm, mlp_up, mlp_down, rwkv_mu, rwkv_w_r, rwkv_w_k, rwkv_w_v, rwkv_w_o, rwkv_decay_w0, rwkv_decay_w1, rwkv_decay_w2, rwkv_iclr_a0, rwkv_iclr_a1, rwkv_iclr_a2, rwkv_gate_g1, rwkv_gate_g2, rwkv_k_k, rwkv_k_a, rwkv_r_k, rwkv_lnx_w, rwkv_lnx_b, rwkv_vres_v0, rwkv_vres_v1, rwkv_vres_v2, conv_w_in, conv_w, conv_w_out, sb_w_qkv, sb_q_norm, sb_k_norm, sb_w_o):
    b, s, d = x.shape
    depth = mix_norm.shape[0]
    n_heads = d // HEAD_DIM
    h = x
    v_first = None
    for i in range(depth):
        kind = i % 3
        j = i // 3
        g_mix = mix_norm[i].reshape(1, d)
        zg = None
        if kind == 0:
            vres = None
            v0 = jnp.zeros((d,), F32)
            if j > 0:
                v1, v2 = _lora(rwkv_vres_v1[j - 1], rwkv_vres_v2[j - 1])
                vres = (v1, v2, v_first)
                v0 = rwkv_vres_v0[j - 1]
            vecs = _rows8(mix_norm[i], rwkv_decay_w0[j], rwkv_iclr_a0[j], v0)
            mu8 = _rows8(*[rwkv_mu[j, m] for m in range(rwkv_mu.shape[1])])
            w1, w2 = _lora(rwkv_decay_w1[j], rwkv_decay_w2[j])
            a1, a2 = _lora(rwkv_iclr_a1[j], rwkv_iclr_a2[j])
            g1, g2 = _lora(rwkv_gate_g1[j], rwkv_gate_g2[j])
            r, k, v, lw, a, zg = _rwkv_in(
                h, vecs, mu8, rwkv_w_r[j].astype(BF16), rwkv_w_k[j].astype(BF16),
                rwkv_w_v[j].astype(BF16), w1, w2, a1, a2, g1, g2, vres)
            if j == 0:
                v_first = v
            pvec = _rows8(rwkv_k_k[j], rwkv_k_a[j], rwkv_r_k[j].reshape(d), rwkv_lnx_w[j], rwkv_lnx_b[j])
            z = _wkv(r, lw, k, v, a, pvec, WKV_PASSES)
            w_o = rwkv_w_o[j]
        elif kind == 1:
            z = _conv_in(h, g_mix, conv_w_in[j].astype(BF16), _rows8(*[conv_w[j, m] for m in range(3)]))
            w_o = conv_w_out[j]
        else:
            qg = jnp.tile(sb_q_norm[j], n_heads).reshape(1, d)
            kg = jnp.tile(sb_k_norm[j], n_heads).reshape(1, d)
            q, k, v = _sb_qkv(h.reshape(b * s, d), g_mix, sb_w_qkv[j].astype(BF16), qg, kg)
            z = _sb_attn(q.reshape(b, s, d), k.reshape(b, s, d), v.reshape(b, s, d))
            w_o = sb_w_o[j]
        h = _outproj_mlp(
            h.reshape(b * s, d), z.reshape(b * s, d), None if zg is None else zg.reshape(b * s, d),
            w_o.astype(BF16), mlp_norm[i].reshape(1, d), mlp_up[i].astype(BF16), mlp_down[i].astype(BF16),
        ).reshape(b, s, d)
    return h
```

```python
import functools

import jax
import jax.numpy as jnp
from jax import lax
from jax.experimental import pallas as pl
from jax.experimental.pallas import tpu as pltpu

F32 = jnp.float32
BF16 = jnp.bfloat16

HEAD_DIM = 64
HEAD_SHIFT = 6
LANES = 128
SUBLANES = 8
RMS_EPS = 1e-6
GN_EPS = 64e-5
KK_EPS = 1e-24
SB_SCALE = HEAD_DIM ** -0.5

V7X_VMEM_BYTES = 64 * 1024 * 1024
VMEM_LIMIT_CAP = V7X_VMEM_BYTES - 8 * 1024 * 1024

MLP_TM = 512
MLP_TF = 512
RWKV_TS = 256
CONV_TS = 256
QKV_TM = 256
WKV_CHUNK = 64
SB_TQ = 256
SB_ROWS = 128


def _vmem_limit(block_bytes, scratch_bytes, temp_bytes):
    return int(min(2 * block_bytes + scratch_bytes + temp_bytes, VMEM_LIMIT_CAP))


def _nbytes(shape, dtype):
    n = 1
    for s in shape:
        n *= s
    return n * jnp.dtype(dtype).itemsize


def _dot(a, b):
    return jnp.dot(a, b, preferred_element_type=F32)


def _dot_nt(a, b):
    return lax.dot_general(a, b, (((1,), (1,)), ((), ())), preferred_element_type=F32)


def _dot_tn(a, b):
    return lax.dot_general(a, b, (((0,), (0,)), ((), ())), preferred_element_type=F32)


def _split(x, pieces):
    out = []
    for _ in range(pieces - 1):
        hi = x.astype(BF16)
        out.append(hi)
        x = x - hi.astype(F32)
    out.append(x.astype(BF16))
    return out


def _dot_exact_rhs(x, m_bf16, pieces, dot=_dot):
    acc = None
    for p in _split(x, pieces):
        t = dot(p, m_bf16)
        acc = t if acc is None else acc + t
    return acc


def _dot_exact_lhs(m_bf16, x, pieces):
    acc = None
    for p in _split(x, pieces):
        t = _dot(m_bf16, p)
        acc = t if acc is None else acc + t
    return acc


def _mm(a, b, passes, dot=_dot):
    if passes == 1:
        return dot(a.astype(BF16), b.astype(BF16))
    ah, al = _split(a, 2)
    bh, bl = _split(b, 2)
    return dot(ah, bh) + (dot(al, bh) + dot(ah, bl))


def _rms(x, g):
    ms = jnp.mean(x * x, axis=-1, keepdims=True)
    return x * lax.rsqrt(ms + RMS_EPS) * g


def _sigmoid(x):
    return 1.0 / (1.0 + jnp.exp(-x))


def _softplus(x):
    return jnp.maximum(x, 0.0) + jnp.log(1.0 + jnp.exp(-jnp.abs(x)))


def _head_sum_matrix(n):
    r = lax.broadcasted_iota(jnp.int32, (n, n), 0) >> HEAD_SHIFT
    c = lax.broadcasted_iota(jnp.int32, (n, n), 1) >> HEAD_SHIFT
    return jnp.where(r == c, 1.0, 0.0).astype(BF16)


def _shift_rows(x, carry8, n):
    rows = lax.broadcasted_iota(jnp.int32, (x.shape[0], 1), 0)
    out = pltpu.roll(x, n, axis=0)
    for i in range(n):
        out = jnp.where(rows == i, carry8[SUBLANES - n + i:SUBLANES - n + i + 1, :], out)
    return out


def _rwkv_in_kernel(*refs, has_vres):
    if has_vres:
        (h_ref, vec_ref, mu_ref, wr_ref, wk_ref, wv_ref, w1_ref, w2_ref, a1_ref, a2_ref,
         g1_ref, g2_ref, v1_ref, v2_ref, vf_ref,
         r_out, k_out, v_out, lw_out, a_out, g_out, carry_ref) = refs
    else:
        (h_ref, vec_ref, mu_ref, wr_ref, wk_ref, wv_ref, w1_ref, w2_ref, a1_ref, a2_ref,
         g1_ref, g2_ref,
         r_out, k_out, v_out, lw_out, a_out, g_out, carry_ref) = refs

    @pl.when(pl.program_id(1) == 0)
    def _():
        carry_ref[...] = jnp.zeros_like(carry_ref)

    ts = h_ref.shape[0]
    u = _rms(h_ref[...], vec_ref[0:1, :])
    prev = _shift_rows(u, carry_ref[...], 1)
    carry_ref[...] = u[ts - SUBLANES:, :]
    xx = prev - u

    def mix(i):
        return (u + xx * mu_ref[i:i + 1, :]).astype(BF16)

    xr, xw, xk, xv, xa, xg = (mix(i) for i in range(6))
    r_out[...] = _dot(xr, wr_ref[...])
    k_out[...] = _dot(xk, wk_ref[...])
    v = _dot(xv, wv_ref[...])

    dw = vec_ref[1:2, :] + _dot(jnp.tanh(_dot(xw, w1_ref[...])).astype(BF16), w2_ref[...])
    log_w = -_softplus(-dw) - 0.5
    lw_out[...] = -jnp.exp(log_w)
    a_out[...] = _sigmoid(vec_ref[2:3, :] + _dot(_dot(xa, a1_ref[...]).astype(BF16), a2_ref[...]))
    g_out[...] = _dot(_sigmoid(_dot(xg, g1_ref[...])).astype(BF16), g2_ref[...])
    if has_vres:
        mixv = _sigmoid(vec_ref[3:4, :] + _dot(_dot(xv, v1_ref[...]).astype(BF16), v2_ref[...]))
        v = v + (vf_ref[...] - v) * mixv
    v_out[...] = v


def _rwkv_in(h, vecs, mu8, wr, wk, wv, w1, w2, a1, a2, g1, g2, vres):
    b, s, d = h.shape
    ts = min(RWKV_TS, s)
    tok = pl.BlockSpec((None, ts, d), lambda i, j: (i, j, 0))

    def full(x):
        return pl.BlockSpec(x.shape, lambda i, j: (0,) * x.ndim)

    weights = [vecs, mu8, wr, wk, wv, w1, w2, a1, a2, g1, g2]
    args = [h] + weights
    in_specs = [tok] + [full(w) for w in weights]
    if vres is not None:
        v1, v2, v_first = vres
        args += [v1, v2, v_first]
        in_specs += [full(v1), full(v2), tok]
    tok_bytes = _nbytes((ts, d), F32)
    block_bytes = sum(_nbytes(w.shape, w.dtype) for w in args[1:1 + len(weights) + (2 if vres else 0)])
    block_bytes += tok_bytes * (7 + (1 if vres else 0))
    out = jax.ShapeDtypeStruct((b, s, d), F32)
    return pl.pallas_call(
        functools.partial(_rwkv_in_kernel, has_vres=vres is not None),
        grid=(b, s // ts),
        in_specs=in_specs,
        out_specs=[tok] * 6,
        out_shape=[out] * 6,
        scratch_shapes=[pltpu.VMEM((SUBLANES, d), F32)],
        compiler_params=pltpu.CompilerParams(
            dimension_semantics=("parallel", "arbitrary"),
            vmem_limit_bytes=_vmem_limit(block_bytes, _nbytes((SUBLANES, d), F32), 12 * tok_bytes)),
        name="rwkv_in",
    )(*args)


def _wkv_kernel(r_ref, lw_ref, k_ref, v_ref, a_ref, pv_ref, y_ref, s_ref, *, passes):
    c = r_ref.shape[0]
    n = 2 * c
    n_tiles = r_ref.shape[1] // LANES

    @pl.when(pl.program_id(1) == 0)
    def _():
        s_ref[...] = jnp.zeros_like(s_ref)

    hsum = _head_sum_matrix(LANES)
    ti = lax.broadcasted_iota(jnp.int32, (c, c), 0)
    si = lax.broadcasted_iota(jnp.int32, (c, c), 1)
    lincl = jnp.where(si <= ti, 1.0, 0.0).astype(BF16)
    row = lax.broadcasted_iota(jnp.int32, (n, LANES), 0)
    lane = lax.broadcasted_iota(jnp.int32, (n, LANES), 1)
    own = (row >> (c.bit_length() - 1)) == (lane >> HEAD_SHIFT)
    ri = lax.broadcasted_iota(jnp.int32, (2 * n, 2 * n), 0)
    ci = lax.broadcasted_iota(jnp.int32, (2 * n, 2 * n), 1) & (c - 1)
    tri = ci < (ri & (c - 1)) + (ri >> (n.bit_length() - 1))
    inv_n = 1.0 / HEAD_DIM
    mm = functools.partial(_mm, passes=passes)
    mm_nt = functools.partial(_mm, passes=passes, dot=_dot_nt)
    mm_tn = functools.partial(_mm, passes=passes, dot=_dot_tn)

    def stack(x):
        return jnp.where(own, jnp.concatenate([x, x], axis=0), 0.0)

    tiles = range(n_tiles)
    sls = [slice(t * LANES, (t + 1) * LANES) for t in tiles]

    def each(f, *cols):
        return [f(*args) for args in zip(*cols)]

    def load(ref, rows=slice(None)):
        return [ref[rows, sl] for sl in sls]

    r, lw, k_raw, v, a = load(r_ref), load(lw_ref), load(k_ref), load(v_ref), load(a_ref)
    k_k, k_a, r_k = load(pv_ref, slice(0, 1)), load(pv_ref, slice(1, 2)), load(pv_ref, slice(2, 3))
    lnx_w, lnx_b = load(pv_ref, slice(3, 4)), load(pv_ref, slice(4, 5))

    kk = each(lambda x, g: x * g, k_raw, k_k)
    kk_ss = each(lambda x: _dot_exact_rhs(x * x, hsum, 2), kk)
    kk = each(lambda x, ss: x * lax.rsqrt(jnp.maximum(ss, KK_EPS)), kk, kk_ss)
    k = each(lambda x, ai, g: x * (1.0 + (ai - 1.0) * g), k_raw, a, k_a)

    cum = each(lambda x: _dot_exact_lhs(lincl, x, 3), lw)
    g_incl = each(jnp.exp, cum)
    g_excl = each(lambda cs, x: jnp.exp(cs - x), cum, lw)
    g_inv = each(lambda cs: jnp.exp(-cs), cum)
    g_end = each(lambda g: g[c - 1:c, :], g_incl)

    ar = each(lambda kki, ri_, ge, gi: jnp.concatenate([stack(-kki * ge), stack(ri_ * gi)], axis=0),
              kk, r, g_excl, g_incl)
    bk = each(lambda kki, ai, ki, gv: jnp.concatenate([stack(kki * ai * gv), stack(ki * gv)], axis=0),
              kk, a, k, g_inv)
    vs = each(stack, v)

    s0 = [s_ref[t] for t in tiles]
    prod = each(lambda x, y: jnp.where(tri, mm_nt(x, y), 0.0), ar, bk)
    from_s0 = each(mm_nt, ar, s0)

    x = each(lambda fs, pr, vi: fs[:n, :] + mm(pr[:n, n:], vi), from_s0, prod, vs)
    p = each(lambda pr: pr[:n, :n], prod)
    steps = c.bit_length() - 1
    for i in range(steps):
        x = each(lambda xi, pi: xi + mm(pi, xi), x, p)
        if i + 1 < steps:
            p = each(lambda pi: mm(pi, pi), p)
    uv = each(lambda xi, vi: jnp.concatenate([xi, vi], axis=0), x, vs)

    yh = each(lambda fs, pr, uvi: fs[n:, :] + mm(pr[n:, :], uvi), from_s0, prod, uv)
    y = each(lambda yi: yi[:c, :] + yi[c:, :], yh)
    s_new = each(lambda si_, ge, uvi, bki: si_ * ge + mm_tn(uvi, bki * ge), s0, g_end, uv, bk)
    for t in tiles:
        s_ref[t] = s_new[t]

    mean = each(lambda yi: _dot_exact_rhs(yi, hsum, 2) * inv_n, y)
    yc = each(lambda yi, m: yi - m, y, mean)
    var = each(lambda yi: _dot_exact_rhs(yi * yi, hsum, 2) * inv_n, yc)
    bonus = each(lambda ri_, ki, g, vi: _dot_exact_rhs(ri_ * ki * g, hsum, 2) * vi, r, k, r_k, v)
    out = each(lambda yi, vr, w, b_, bo: yi * lax.rsqrt(vr + GN_EPS) * w + b_ + bo, yc, var, lnx_w, lnx_b, bonus)
    for t in tiles:
        y_ref[:, sls[t]] = out[t]


def _wkv(r, lw, k, v, a, pvec, passes):
    b, s, d = r.shape
    c = WKV_CHUNK
    tok = pl.BlockSpec((None, c, d), lambda i, j: (i, j, 0))
    par = pl.BlockSpec((SUBLANES, d), lambda i, j: (0, 0))
    tok_bytes = _nbytes((c, d), F32)
    sq_bytes = _nbytes((2 * c, 2 * c), F32)
    n_tiles = d // LANES
    return pl.pallas_call(
        functools.partial(_wkv_kernel, passes=passes),
        grid=(b, s // c),
        in_specs=[tok] * 5 + [par],
        out_specs=tok,
        out_shape=jax.ShapeDtypeStruct((b, s, d), F32),
        scratch_shapes=[pltpu.VMEM((n_tiles, LANES, LANES), F32)],
        compiler_params=pltpu.CompilerParams(
            dimension_semantics=("parallel", "arbitrary"),
            vmem_limit_bytes=_vmem_limit(7 * tok_bytes, n_tiles * sq_bytes, n_tiles * 48 * sq_bytes)),
        name="wkv",
    )(r, lw, k, v, a, pvec)


def _conv_in_kernel(h_ref, g_ref, win_ref, cw_ref, z_out, carry_ref):
    @pl.when(pl.program_id(1) == 0)
    def _():
        carry_ref[...] = jnp.zeros_like(carry_ref)

    ts, d = h_ref.shape
    u = _rms(h_ref[...], g_ref[...]).astype(BF16)
    proj = _dot(u, win_ref[...])
    b_gate = proj[:, :d]
    z = proj[:, d:2 * d] * proj[:, 2 * d:]
    carry = carry_ref[...]
    z1 = _shift_rows(z, carry, 1)
    z2 = _shift_rows(z, carry, 2)
    carry_ref[...] = z[ts - SUBLANES:, :]
    zc = z2 * cw_ref[0:1, :] + z1 * cw_ref[1:2, :] + z * cw_ref[2:3, :]
    z_out[...] = b_gate * zc


def _conv_in(h, g, w_in, cw8):
    b, s, d = h.shape
    ts = min(CONV_TS, s)
    tok = pl.BlockSpec((None, ts, d), lambda i, j: (i, j, 0))
    tok_bytes = _nbytes((ts, d), F32)
    block_bytes = 2 * tok_bytes + _nbytes(w_in.shape, BF16) + _nbytes(cw8.shape, F32)
    return pl.pallas_call(
        _conv_in_kernel,
        grid=(b, s // ts),
        in_specs=[tok,
                  pl.BlockSpec(g.shape, lambda i, j: (0, 0)),
                  pl.BlockSpec(w_in.shape, lambda i, j: (0, 0)),
                  pl.BlockSpec(cw8.shape, lambda i, j: (0, 0))],
        out_specs=tok,
        out_shape=jax.ShapeDtypeStruct((b, s, d), F32),
        scratch_shapes=[pltpu.VMEM((SUBLANES, d), F32)],
        compiler_params=pltpu.CompilerParams(
            dimension_semantics=("parallel", "arbitrary"),
            vmem_limit_bytes=_vmem_limit(block_bytes, _nbytes((SUBLANES, d), F32), 10 * tok_bytes)),
        name="conv_in",
    )(h, g, w_in, cw8)


def _sb_qkv_kernel(h_ref, g_ref, w_ref, qg_ref, kg_ref, q_out, k_out, v_out):
    d = h_ref.shape[1]
    u = _rms(h_ref[...], g_ref[...]).astype(BF16)
    proj = _dot(u, w_ref[...])
    hsum = _head_sum_matrix(2 * LANES)
    inv_n = 1.0 / HEAD_DIM

    def head_norm(x, gain):
        cols = []
        for j in range(0, d, 2 * LANES):
            xs = x[:, j:j + 2 * LANES]
            ms = _dot_exact_rhs(xs * xs, hsum, 2) * inv_n
            cols.append(xs * lax.rsqrt(ms + RMS_EPS))
        return jnp.concatenate(cols, axis=1) * gain

    q_out[...] = head_norm(proj[:, :d], qg_ref[...]).astype(BF16)
    k_out[...] = head_norm(proj[:, d:2 * d], kg_ref[...]).astype(BF16)
    v_out[...] = proj[:, 2 * d:].astype(BF16)


def _sb_qkv(h2, g, w_qkv, qg, kg):
    t, d = h2.shape
    tm = min(QKV_TM, t)
    tok = pl.BlockSpec((tm, d), lambda i: (i, 0))
    vec = pl.BlockSpec((1, d), lambda i: (0, 0))
    tok_bytes = _nbytes((tm, d), F32)
    block_bytes = 4 * tok_bytes + _nbytes(w_qkv.shape, BF16) + 3 * _nbytes((1, d), F32)
    out = jax.ShapeDtypeStruct((t, d), BF16)
    return pl.pallas_call(
        _sb_qkv_kernel,
        grid=(t // tm,),
        in_specs=[tok, vec, pl.BlockSpec(w_qkv.shape, lambda i: (0, 0)), vec, vec],
        out_specs=[tok] * 3,
        out_shape=[out] * 3,
        compiler_params=pltpu.CompilerParams(
            dimension_semantics=("parallel",),
            vmem_limit_bytes=_vmem_limit(block_bytes, 0, 10 * tok_bytes)),
        name="sb_qkv",
    )(h2, g, w_qkv, qg, kg)


def _sb_attn_kernel(q_ref, k_ref, v_ref, suffix_ref, o_ref, acc_ref, run_ref):
    tq = q_ref.shape[0]
    qb = pl.program_id(2)
    rows = lax.broadcasted_iota(jnp.int32, (2 * tq, LANES), 0)
    lanes = lax.broadcasted_iota(jnp.int32, (2 * tq, LANES), 1)
    own = (rows >> (tq.bit_length() - 1)) == (lanes >> HEAD_SHIFT)
    q = q_ref[...].astype(F32)
    qs = (jnp.where(own, jnp.concatenate([q, q], axis=0), 0.0) * SB_SCALE).astype(BF16)
    suffix = suffix_ref[...]

    groups = [slice(g * SB_ROWS, (g + 1) * SB_ROWS) for g in range(2 * tq // SB_ROWS)]

    def each(f, *cols):
        return [f(*args) for args in zip(*cols)]

    def tile(start, diagonal):
        ks = k_ref[pl.ds(start, tq), :]
        vs = v_ref[pl.ds(start, tq), :]
        z = [_dot_nt(qs[g, :], ks) for g in groups]
        sp = each(_softplus, z)
        if diagonal:
            t_loc = lax.broadcasted_iota(jnp.int32, (SB_ROWS, tq), 0)
            s_loc = lax.broadcasted_iota(jnp.int32, (SB_ROWS, tq), 1)
            causal = [s_loc < t_loc + (g.start & (tq - 1)) for g in groups]
            sp = each(lambda m, x: jnp.where(m, x, 0.0), causal, sp)
        rev_cum = each(lambda x: _dot_exact_rhs(x, suffix, 2), sp)
        if diagonal:
            att = each(lambda m, zi, rc: jnp.where(m, jnp.exp(zi - rc), 0.0), causal, z, rev_cum)
        else:
            att = each(lambda g, zi, rc: jnp.exp(zi - rc - run_ref[g, :]), groups, z, rev_cum)
        pv = each(lambda x: _dot(x.astype(BF16), vs), att)
        total = each(lambda rc: jnp.broadcast_to(rc[:, 0:1], rc.shape), rev_cum)
        for g, pvi, ti in zip(groups, pv, total):
            if diagonal:
                acc_ref[g, :] = pvi
                run_ref[g, :] = ti
            else:
                acc_ref[g, :] += pvi
                run_ref[g, :] += ti

    tile(pl.multiple_of(qb * tq, tq), True)

    def body(it, carry):
        tile(pl.multiple_of((qb - 1 - it) * tq, tq), False)
        return carry

    lax.fori_loop(0, qb, body, 0)
    lane = lax.broadcasted_iota(jnp.int32, (tq, LANES), 1)
    o_ref[...] = jnp.where(lane >= HEAD_DIM, acc_ref[tq:, :], acc_ref[:tq, :])


def _sb_attn(q, k, v):
    b, s, d = q.shape
    tq = SB_TQ
    ri = lax.broadcasted_iota(jnp.int32, (tq, tq), 0)
    ci = lax.broadcasted_iota(jnp.int32, (tq, tq), 1)
    suffix = jnp.where(ri >= ci, 1.0, 0.0).astype(BF16)
    qspec = pl.BlockSpec((None, tq, LANES), lambda i, p, j: (i, j, p))
    kspec = pl.BlockSpec((None, s, LANES), lambda i, p, j: (i, 0, p))
    block_bytes = (_nbytes((tq, LANES), BF16) + _nbytes((tq, LANES), F32) + 2 * _nbytes((s, LANES), BF16)
                   + _nbytes((tq, tq), BF16))
    scratch = [pltpu.VMEM((2 * tq, LANES), F32), pltpu.VMEM((2 * tq, tq), F32)]
    scratch_bytes = _nbytes((2 * tq, LANES), F32) + _nbytes((2 * tq, tq), F32)
    return pl.pallas_call(
        _sb_attn_kernel,
        grid=(b, d // LANES, s // tq),
        in_specs=[qspec, kspec, kspec, pl.BlockSpec((tq, tq), lambda i, p, j: (0, 0))],
        out_specs=qspec,
        out_shape=jax.ShapeDtypeStruct((b, s, d), F32),
        scratch_shapes=scratch,
        compiler_params=pltpu.CompilerParams(
            dimension_semantics=("parallel", "parallel", "arbitrary"),
            vmem_limit_bytes=_vmem_limit(block_bytes, scratch_bytes, 12 * _nbytes((2 * tq, tq), F32))),
        name="sb_attn",
    )(q, k, v, suffix)


def _outproj_mlp_kernel(*refs, has_gate):
    if has_gate:
        h_ref, z_ref, zg_ref, wo_ref, g_ref, wup_ref, wdn_ref, o_ref, xn_ref = refs
    else:
        h_ref, z_ref, wo_ref, g_ref, wup_ref, wdn_ref, o_ref, xn_ref = refs

    @pl.when(pl.program_id(1) == 0)
    def _():
        z = z_ref[...]
        if has_gate:
            z = z * zg_ref[...]
        h1 = h_ref[...] + _dot(z.astype(BF16), wo_ref[...])
        o_ref[...] = h1
        xn_ref[...] = _rms(h1, g_ref[...]).astype(BF16)

    act = jnp.square(jnp.maximum(_dot(xn_ref[...], wup_ref[...]), 0.0)).astype(BF16)
    o_ref[...] += _dot(act, wdn_ref[...])


def _outproj_mlp(h2, z, zg, w_o, g, w_up, w_dn):
    t, d = h2.shape
    f = w_up.shape[1]
    tm = min(MLP_TM, t)
    tf = min(MLP_TF, f)
    tok = pl.BlockSpec((tm, d), lambda i, j: (i, 0))
    acts = [h2, z] + ([zg] if zg is not None else [])
    in_specs = [tok] * len(acts) + [
        pl.BlockSpec((d, d), lambda i, j: (0, 0)),
        pl.BlockSpec((1, d), lambda i, j: (0, 0)),
        pl.BlockSpec((d, tf), lambda i, j: (0, j)),
        pl.BlockSpec((tf, d), lambda i, j: (j, 0)),
    ]
    tok_bytes = _nbytes((tm, d), F32)
    block_bytes = (len(acts) + 1) * tok_bytes + _nbytes((d, d), BF16) + 2 * _nbytes((d, tf), BF16)
    return pl.pallas_call(
        functools.partial(_outproj_mlp_kernel, has_gate=zg is not None),
        grid=(t // tm, f // tf),
        in_specs=in_specs,
        out_specs=tok,
        out_shape=jax.ShapeDtypeStruct((t, d), F32),
        scratch_shapes=[pltpu.VMEM((tm, d), BF16)],
        compiler_params=pltpu.CompilerParams(
            dimension_semantics=("parallel", "arbitrary"),
            vmem_limit_bytes=_vmem_limit(block_bytes, _nbytes((tm, d), BF16),
                                         2 * tok_bytes + 2 * _nbytes((tm, tf), F32))),
        name="outproj_mlp",
    )(*acts, w_o, g, w_up, w_dn)


def _pad_cols(w, mult):
    pad = (-w.shape[1]) % mult
    return jnp.pad(w, ((0, 0), (0, pad)))


def _pad_rows(w, mult):
    pad = (-w.shape[0]) % mult
    return jnp.pad(w, ((0, pad), (0, 0)))


def _rows8(*rows):
    d = rows[0].shape[-1]
    out = jnp.zeros((SUBLANES, d), F32)
    return out.at[:len(rows)].set(jnp.stack([r.reshape(d) for r in rows]))


def _lora(w_in, w_out):
    return _pad_cols(w_in, LANES).astype(BF16), _pad_rows(w_out, LANES).astype(BF16)


WKV_PASSES = 1


def kernel(x, mix_norm, mlp_norm, mlp_up, mlp_down, rwkv_mu, rwkv_w_r, rwkv_w_k, rwkv_w_v, rwkv_w_o, rwkv_decay_w0, rwkv_decay_w1, rwkv_decay_w2, rwkv_iclr_a0, rwkv_iclr_a1, rwkv_iclr_a2, rwkv_gate_g1, rwkv_gate_g2, rwkv_k_k, rwkv_k_a, rwkv_r_k, rwkv_lnx_w, rwkv_lnx_b, rwkv_vres_v0, rwkv_vres_v1, rwkv_vres_v2, conv_w_in, conv_w, conv_w_out, sb_w_qkv, sb_q_norm, sb_k_norm, sb_w_o):
    b, s, d = x.shape
    depth = mix_norm.shape[0]
    n_heads = d // HEAD_DIM
    h = x
    v_first = None
    for i in range(depth):
        kind = i % 3
        j = i // 3
        g_mix = mix_norm[i].reshape(1, d)
        zg = None
        if kind == 0:
            vres = None
            v0 = jnp.zeros((d,), F32)
            if j > 0:
                v1, v2 = _lora(rwkv_vres_v1[j - 1], rwkv_vres_v2[j - 1])
                vres = (v1, v2, v_first)
                v0 = rwkv_vres_v0[j - 1]
            vecs = _rows8(mix_norm[i], rwkv_decay_w0[j], rwkv_iclr_a0[j], v0)
            mu8 = _rows8(*[rwkv_mu[j, m] for m in range(rwkv_mu.shape[1])])
            w1, w2 = _lora(rwkv_decay_w1[j], rwkv_decay_w2[j])
            a1, a2 = _lora(rwkv_iclr_a1[j], rwkv_iclr_a2[j])
            g1, g2 = _lora(rwkv_gate_g1[j], rwkv_gate_g2[j])
            r, k, v, lw, a, zg = _rwkv_in(
                h, vecs, mu8, rwkv_w_r[j].astype(BF16), rwkv_w_k[j].astype(BF16),
                rwkv_w_v[j].astype(BF16), w1, w2, a1, a2, g1, g2, vres)
            if j == 0:
                v_first = v
            pvec = _rows8(rwkv_k_k[j], rwkv_k_a[j], rwkv_r_k[j].reshape(d), rwkv_lnx_w[j], rwkv_lnx_b[j])
            z = _wkv(r, lw, k, v, a, pvec, WKV_PASSES)
            w_o = rwkv_w_o[j]
        elif kind == 1:
            z = _conv_in(h, g_mix, conv_w_in[j].astype(BF16), _rows8(*[conv_w[j, m] for m in range(3)]))
            w_o = conv_w_out[j]
        else:
            qg = jnp.tile(sb_q_norm[j], n_heads).reshape(1, d)
            kg = jnp.tile(sb_k_norm[j], n_heads).reshape(1, d)
            q, k, v = _sb_qkv(h.reshape(b * s, d), g_mix, sb_w_qkv[j].astype(BF16), qg, kg)
            z = _sb_attn(q.reshape(b, s, d), k.reshape(b, s, d), v.reshape(b, s, d))
            w_o = sb_w_o[j]
        h = _outproj_mlp(
            h.reshape(b * s, d), z.reshape(b * s, d), None if zg is None else zg.reshape(b * s, d),
            w_o.astype(BF16), mlp_norm[i].reshape(1, d), mlp_up[i].astype(BF16), mlp_down[i].astype(BF16),
        ).reshape(b, s, d)
    return h
```

```python
import functools

import jax
import jax.numpy as jnp
from jax import lax
from jax.experimental import pallas as pl
from jax.experimental.pallas import tpu as pltpu

F32 = jnp.float32
BF16 = jnp.bfloat16

HEAD_DIM = 64
HEAD_SHIFT = 6
LANES = 128
SUBLANES = 8
RMS_EPS = 1e-6
GN_EPS = 64e-5
KK_EPS = 1e-24
SB_SCALE = HEAD_DIM ** -0.5
SB_CUM_PIECES = 1

V7X_VMEM_BYTES = 64 * 1024 * 1024
VMEM_LIMIT_CAP = V7X_VMEM_BYTES - 8 * 1024 * 1024

MLP_TM = 1024
MLP_TF = 512
RWKV_TS = 256
CONV_TS = 256
QKV_TM = 256
QKV_SUM_PIECES = 1
WKV_CHUNK = 64
WKV_PASSES = 1
WKV_CUM_PIECES = 2
WKV_SUM_PIECES = 1
SB_TQ = 256
SB_ROWS = 128


def _vmem_limit(block_bytes, scratch_bytes, temp_bytes):
    return int(min(2 * block_bytes + scratch_bytes + temp_bytes, VMEM_LIMIT_CAP))


def _nbytes(shape, dtype):
    n = 1
    for s in shape:
        n *= s
    return n * jnp.dtype(dtype).itemsize


def _dot(a, b):
    return jnp.dot(a, b, preferred_element_type=F32)


def _dot_nt(a, b):
    return lax.dot_general(a, b, (((1,), (1,)), ((), ())), preferred_element_type=F32)


def _dot_tn(a, b):
    return lax.dot_general(a, b, (((0,), (0,)), ((), ())), preferred_element_type=F32)


def _split(x, pieces):
    out = []
    for _ in range(pieces - 1):
        hi = x.astype(BF16)
        out.append(hi)
        x = x - hi.astype(F32)
    out.append(x.astype(BF16))
    return out


def _dot_exact_rhs(x, m_bf16, pieces, dot=_dot):
    acc = None
    for p in _split(x, pieces):
        t = dot(p, m_bf16)
        acc = t if acc is None else acc + t
    return acc


def _dot_exact_lhs(m_bf16, x, pieces):
    acc = None
    for p in _split(x, pieces):
        t = _dot(m_bf16, p)
        acc = t if acc is None else acc + t
    return acc


def _mm(a, b, passes, dot=_dot):
    if passes == 1:
        return dot(a.astype(BF16), b.astype(BF16))
    ah, al = _split(a, 2)
    bh, bl = _split(b, 2)
    return dot(ah, bh) + (dot(al, bh) + dot(ah, bl))


def _rms(x, g):
    ms = jnp.mean(x * x, axis=-1, keepdims=True)
    return x * lax.rsqrt(ms + RMS_EPS) * g


def _sigmoid(x):
    return 1.0 / (1.0 + jnp.exp(-x))


def _softplus(x):
    return jnp.maximum(x, 0.0) + jnp.log(1.0 + jnp.exp(-jnp.abs(x)))


def _softplus_fast(x):
    bits = lax.bitcast_convert_type(x, jnp.uint32) | jnp.uint32(0x80000000)
    return jnp.maximum(x, 0.0) + jnp.log(1.0 + jnp.exp(lax.bitcast_convert_type(bits, F32)))


def _head_sum_matrix(n):
    r = lax.broadcasted_iota(jnp.int32, (n, n), 0) >> HEAD_SHIFT
    c = lax.broadcasted_iota(jnp.int32, (n, n), 1) >> HEAD_SHIFT
    return jnp.where(r == c, 1.0, 0.0).astype(BF16)


def _shift_rows(x, carry8, n):
    rows = lax.broadcasted_iota(jnp.int32, (x.shape[0], 1), 0)
    out = pltpu.roll(x, n, axis=0)
    for i in range(n):
        out = jnp.where(rows == i, carry8[SUBLANES - n + i:SUBLANES - n + i + 1, :], out)
    return out


def _rwkv_in_kernel(*refs, has_vres):
    if has_vres:
        (h_ref, vec_ref, mu_ref, wr_ref, wk_ref, wv_ref, w1_ref, w2_ref, a1_ref, a2_ref,
         g1_ref, g2_ref, v1_ref, v2_ref, vf_ref,
         r_out, k_out, v_out, lw_out, a_out, g_out, carry_ref) = refs
    else:
        (h_ref, vec_ref, mu_ref, wr_ref, wk_ref, wv_ref, w1_ref, w2_ref, a1_ref, a2_ref,
         g1_ref, g2_ref,
         r_out, k_out, v_out, lw_out, a_out, g_out, carry_ref) = refs

    @pl.when(pl.program_id(1) == 0)
    def _():
        carry_ref[...] = jnp.zeros_like(carry_ref)

    ts = h_ref.shape[0]
    u = _rms(h_ref[...], vec_ref[0:1, :])
    prev = _shift_rows(u, carry_ref[...], 1)
    carry_ref[...] = u[ts - SUBLANES:, :]
    xx = prev - u

    def mix(i):
        return (u + xx * mu_ref[i:i + 1, :]).astype(BF16)

    r_out[...] = _dot(mix(0), wr_ref[...])
    k_out[...] = _dot(mix(2), wk_ref[...])
    xv = mix(3)
    v = _dot(xv, wv_ref[...])
    hw = jnp.tanh(_dot(mix(1), w1_ref[...])).astype(BF16)
    ha = _dot(mix(4), a1_ref[...]).astype(BF16)
    hg = _sigmoid(_dot(mix(5), g1_ref[...])).astype(BF16)
    if has_vres:
        hv = _dot(xv, v1_ref[...]).astype(BF16)

    log_w = -_softplus(-(vec_ref[1:2, :] + _dot(hw, w2_ref[...]))) - 0.5
    lw_out[...] = -jnp.exp(log_w)
    a_out[...] = _sigmoid(vec_ref[2:3, :] + _dot(ha, a2_ref[...]))
    g_out[...] = _dot(hg, g2_ref[...])
    if has_vres:
        v = v + (vf_ref[...] - v) * _sigmoid(vec_ref[3:4, :] + _dot(hv, v2_ref[...]))
    v_out[...] = v


def _rwkv_in(h, vecs, mu8, wr, wk, wv, w1, w2, a1, a2, g1, g2, vres):
    b, s, d = h.shape
    ts = min(RWKV_TS, s)
    tok = pl.BlockSpec((None, ts, d), lambda i, j: (i, j, 0))

    def full(x):
        return pl.BlockSpec(x.shape, lambda i, j: (0,) * x.ndim)

    weights = [vecs, mu8, wr, wk, wv, w1, w2, a1, a2, g1, g2]
    args = [h] + weights
    in_specs = [tok] + [full(w) for w in weights]
    if vres is not None:
        v1, v2, v_first = vres
        args += [v1, v2, v_first]
        in_specs += [full(v1), full(v2), tok]
    tok_bytes = _nbytes((ts, d), F32)
    block_bytes = sum(_nbytes(w.shape, w.dtype) for w in args[1:1 + len(weights) + (2 if vres else 0)])
    block_bytes += tok_bytes * (7 + (1 if vres else 0))
    out = jax.ShapeDtypeStruct((b, s, d), F32)
    return pl.pallas_call(
        functools.partial(_rwkv_in_kernel, has_vres=vres is not None),
        grid=(b, s // ts),
        in_specs=in_specs,
        out_specs=[tok] * 6,
        out_shape=[out] * 6,
        scratch_shapes=[pltpu.VMEM((SUBLANES, d), F32)],
        compiler_params=pltpu.CompilerParams(
            dimension_semantics=("parallel", "arbitrary"),
            vmem_limit_bytes=_vmem_limit(block_bytes, _nbytes((SUBLANES, d), F32), 12 * tok_bytes)),
        name="rwkv_in",
    )(*args)


def _wkv_kernel(r_ref, lw_ref, k_ref, v_ref, a_ref, g_ref, pv_ref, y_ref, s_ref, *, passes):
    c = r_ref.shape[0]
    n = 2 * c
    n_tiles = r_ref.shape[1] // LANES

    @pl.when(pl.program_id(1) == 0)
    def _():
        s_ref[...] = jnp.zeros_like(s_ref)

    hsum = _head_sum_matrix(LANES)
    ti = lax.broadcasted_iota(jnp.int32, (c, c), 0)
    si = lax.broadcasted_iota(jnp.int32, (c, c), 1)
    lincl = jnp.where(si <= ti, 1.0, 0.0).astype(BF16)
    row = lax.broadcasted_iota(jnp.int32, (n, LANES), 0)
    lane = lax.broadcasted_iota(jnp.int32, (n, LANES), 1)
    own = (row >> (c.bit_length() - 1)) == (lane >> HEAD_SHIFT)
    ri = lax.broadcasted_iota(jnp.int32, (2 * n, 2 * n), 0)
    ci = lax.broadcasted_iota(jnp.int32, (2 * n, 2 * n), 1) & (c - 1)
    tri = ci < (ri & (c - 1)) + (ri >> (n.bit_length() - 1))
    inv_n = 1.0 / HEAD_DIM
    mm = functools.partial(_mm, passes=passes)
    mm_nt = functools.partial(_mm, passes=passes, dot=_dot_nt)
    mm_tn = functools.partial(_mm, passes=passes, dot=_dot_tn)

    def stack(x):
        return jnp.where(own, jnp.concatenate([x, x], axis=0), 0.0)

    tiles = range(n_tiles)
    sls = [slice(t * LANES, (t + 1) * LANES) for t in tiles]

    def each(f, *cols):
        return [f(*args) for args in zip(*cols)]

    def load(ref, rows=slice(None)):
        return [ref[rows, sl] for sl in sls]

    r, lw, k_raw, v, a = load(r_ref), load(lw_ref), load(k_ref), load(v_ref), load(a_ref)
    k_k, k_a, r_k = load(pv_ref, slice(0, 1)), load(pv_ref, slice(1, 2)), load(pv_ref, slice(2, 3))
    lnx_w, lnx_b = load(pv_ref, slice(3, 4)), load(pv_ref, slice(4, 5))

    kk = each(lambda x, g: x * g, k_raw, k_k)
    kk_ss = each(lambda x: _dot_exact_rhs(x * x, hsum, WKV_SUM_PIECES), kk)
    kk = each(lambda x, ss: x * lax.rsqrt(jnp.maximum(ss, KK_EPS)), kk, kk_ss)
    k = each(lambda x, ai, g: x * (1.0 + (ai - 1.0) * g), k_raw, a, k_a)

    cum = each(lambda x: _dot_exact_lhs(lincl, x, WKV_CUM_PIECES), lw)
    g_incl = each(jnp.exp, cum)
    g_excl = each(lambda cs, x: jnp.exp(cs - x), cum, lw)
    g_inv = each(lambda cs: jnp.exp(-cs), cum)
    g_end = each(lambda g: g[c - 1:c, :], g_incl)

    ar = each(lambda kki, ri_, ge, gi: jnp.concatenate([stack(-kki * ge), stack(ri_ * gi)], axis=0),
              kk, r, g_excl, g_incl)
    bk = each(lambda kki, ai, ki, gv: jnp.concatenate([stack(kki * ai * gv), stack(ki * gv)], axis=0),
              kk, a, k, g_inv)
    vs = each(stack, v)

    s0 = [s_ref[t] for t in tiles]
    prod = each(lambda x, y: jnp.where(tri, mm_nt(x, y), 0.0), ar, bk)
    from_s0 = each(mm_nt, ar, s0)

    x = each(lambda fs, pr, vi: fs[:n, :] + mm(pr[:n, n:], vi), from_s0, prod, vs)
    p = each(lambda pr: pr[:n, :n], prod)
    steps = c.bit_length() - 1
    for i in range(steps - 1):
        px = each(lambda xi, pi: mm(pi, jnp.concatenate([xi, pi], axis=1)), x, p)
        x = each(lambda xi, pxi: xi + pxi[:, :LANES], x, px)
        p = each(lambda pxi: pxi[:, LANES:], px)
    x = each(lambda xi, pi: xi + mm(pi, xi), x, p)
    uv = each(lambda xi, vi: jnp.concatenate([xi, vi], axis=0), x, vs)

    yh = each(lambda fs, pr, uvi: fs[n:, :] + mm(pr[n:, :], uvi), from_s0, prod, uv)
    y = each(lambda yi: yi[:c, :] + yi[c:, :], yh)
    s_new = each(lambda si_, ge, uvi, bki: si_ * ge + mm_tn(uvi, bki * ge), s0, g_end, uv, bk)
    for t in tiles:
        s_ref[t] = s_new[t]

    mean = each(lambda yi: _dot_exact_rhs(yi, hsum, WKV_SUM_PIECES) * inv_n, y)
    yc = each(lambda yi, m: yi - m, y, mean)
    var = each(lambda yi: _dot_exact_rhs(yi * yi, hsum, WKV_SUM_PIECES) * inv_n, yc)
    bonus = each(lambda ri_, ki, g, vi: _dot_exact_rhs(ri_ * ki * g, hsum, WKV_SUM_PIECES) * vi, r, k, r_k, v)
    out = each(lambda yi, vr, w, b_, bo: yi * lax.rsqrt(vr + GN_EPS) * w + b_ + bo, yc, var, lnx_w, lnx_b, bonus)
    for t in tiles:
        y_ref[:, sls[t]] = (out[t] * g_ref[:, sls[t]]).astype(BF16)


def _wkv(r, lw, k, v, a, g, pvec, passes):
    b, s, d = r.shape
    c = WKV_CHUNK
    tok = pl.BlockSpec((None, c, d), lambda i, j: (i, j, 0))
    par = pl.BlockSpec((SUBLANES, d), lambda i, j: (0, 0))
    tok_bytes = _nbytes((c, d), F32)
    sq_bytes = _nbytes((2 * c, 2 * c), F32)
    n_tiles = d // LANES
    return pl.pallas_call(
        functools.partial(_wkv_kernel, passes=passes),
        grid=(b, s // c),
        in_specs=[tok] * 6 + [par],
        out_specs=tok,
        out_shape=jax.ShapeDtypeStruct((b, s, d), BF16),
        scratch_shapes=[pltpu.VMEM((n_tiles, LANES, LANES), F32)],
        compiler_params=pltpu.CompilerParams(
            dimension_semantics=("parallel", "arbitrary"),
            vmem_limit_bytes=_vmem_limit(8 * tok_bytes, n_tiles * sq_bytes, n_tiles * 48 * sq_bytes)),
        name="wkv",
    )(r, lw, k, v, a, g, pvec)


def _conv_in_kernel(h_ref, g_ref, win_ref, cw_ref, z_out, carry_ref):
    @pl.when(pl.program_id(1) == 0)
    def _():
        carry_ref[...] = jnp.zeros_like(carry_ref)

    ts, d = h_ref.shape
    u = _rms(h_ref[...], g_ref[...]).astype(BF16)
    z = _dot(u, win_ref[:, d:2 * d]) * _dot(u, win_ref[:, 2 * d:])
    b_gate = _dot(u, win_ref[:, :d])
    carry = carry_ref[...]
    z1 = _shift_rows(z, carry, 1)
    z2 = _shift_rows(z, carry, 2)
    carry_ref[...] = z[ts - SUBLANES:, :]
    zc = z2 * cw_ref[0:1, :] + z1 * cw_ref[1:2, :] + z * cw_ref[2:3, :]
    z_out[...] = (b_gate * zc).astype(BF16)


def _conv_in(h, g, w_in, cw8):
    b, s, d = h.shape
    ts = min(CONV_TS, s)
    tok = pl.BlockSpec((None, ts, d), lambda i, j: (i, j, 0))
    tok_bytes = _nbytes((ts, d), F32)
    block_bytes = 2 * tok_bytes + _nbytes(w_in.shape, BF16) + _nbytes(cw8.shape, F32)
    return pl.pallas_call(
        _conv_in_kernel,
        grid=(b, s // ts),
        in_specs=[tok,
                  pl.BlockSpec(g.shape, lambda i, j: (0, 0)),
                  pl.BlockSpec(w_in.shape, lambda i, j: (0, 0)),
                  pl.BlockSpec(cw8.shape, lambda i, j: (0, 0))],
        out_specs=tok,
        out_shape=jax.ShapeDtypeStruct((b, s, d), BF16),
        scratch_shapes=[pltpu.VMEM((SUBLANES, d), F32)],
        compiler_params=pltpu.CompilerParams(
            dimension_semantics=("parallel", "arbitrary"),
            vmem_limit_bytes=_vmem_limit(block_bytes, _nbytes((SUBLANES, d), F32), 10 * tok_bytes)),
        name="conv_in",
    )(h, g, w_in, cw8)


def _sb_qkv_kernel(h_ref, g_ref, w_ref, qg_ref, kg_ref, q_out, k_out, v_out):
    d = h_ref.shape[1]
    u = _rms(h_ref[...], g_ref[...]).astype(BF16)
    hsum = _head_sum_matrix(2 * LANES)
    inv_n = 1.0 / HEAD_DIM

    def head_norm(x, gain):
        cols = []
        for j in range(0, d, 2 * LANES):
            xs = x[:, j:j + 2 * LANES]
            ms = _dot_exact_rhs(xs * xs, hsum, QKV_SUM_PIECES) * inv_n
            cols.append(xs * lax.rsqrt(ms + RMS_EPS))
        return jnp.concatenate(cols, axis=1) * gain

    q = _dot(u, w_ref[:, :d])
    k = _dot(u, w_ref[:, d:2 * d])
    q_out[...] = head_norm(q, qg_ref[...]).astype(BF16)
    v = _dot(u, w_ref[:, 2 * d:])
    k_out[...] = head_norm(k, kg_ref[...]).astype(BF16)
    v_out[...] = v.astype(BF16)


def _sb_qkv(h2, g, w_qkv, qg, kg):
    t, d = h2.shape
    tm = min(QKV_TM, t)
    tok = pl.BlockSpec((tm, d), lambda i: (i, 0))
    vec = pl.BlockSpec((1, d), lambda i: (0, 0))
    tok_bytes = _nbytes((tm, d), F32)
    block_bytes = 4 * tok_bytes + _nbytes(w_qkv.shape, BF16) + 3 * _nbytes((1, d), F32)
    out = jax.ShapeDtypeStruct((t, d), BF16)
    return pl.pallas_call(
        _sb_qkv_kernel,
        grid=(t // tm,),
        in_specs=[tok, vec, pl.BlockSpec(w_qkv.shape, lambda i: (0, 0)), vec, vec],
        out_specs=[tok] * 3,
        out_shape=[out] * 3,
        compiler_params=pltpu.CompilerParams(
            dimension_semantics=("parallel",),
            vmem_limit_bytes=_vmem_limit(block_bytes, 0, 10 * tok_bytes)),
        name="sb_qkv",
    )(h2, g, w_qkv, qg, kg)


def _sb_attn_kernel(q_ref, k_ref, v_ref, suffix_ref, o_ref, acc_ref, run_ref, za_ref, zb_ref, att_ref):
    tq = q_ref.shape[0]
    qb = pl.program_id(2)
    rows = lax.broadcasted_iota(jnp.int32, (2 * tq, LANES), 0)
    lanes = lax.broadcasted_iota(jnp.int32, (2 * tq, LANES), 1)
    own = (rows >> (tq.bit_length() - 1)) == (lanes >> HEAD_SHIFT)
    q = q_ref[...].astype(F32)
    qs = (jnp.where(own, jnp.concatenate([q, q], axis=0), 0.0) * SB_SCALE).astype(BF16)
    suffix = suffix_ref[...]

    groups = [slice(g * SB_ROWS, (g + 1) * SB_ROWS) for g in range(2 * tq // SB_ROWS)]

    def each(f, *cols):
        return [f(*args) for args in zip(*cols)]

    def key_tile(ref, j):
        return ref[pl.ds(pl.multiple_of(j * tq, tq), tq), :]

    def scores(j, z_out):
        ks = key_tile(k_ref, j)
        for g in groups:
            z_out[g, :] = _dot_nt(qs[g, :], ks)

    def step(j, z_in, z_out, diagonal):
        if not diagonal:
            vs = key_tile(v_ref, j + 1)
            pv = [_dot(att_ref[g, :], vs) for g in groups]
        sp = [_softplus_fast(z_in[g, :]) for g in groups]
        if diagonal:
            t_loc = lax.broadcasted_iota(jnp.int32, (SB_ROWS, tq), 0)
            s_loc = lax.broadcasted_iota(jnp.int32, (SB_ROWS, tq), 1)
            causal = [s_loc < t_loc + (g.start & (tq - 1)) for g in groups]
            sp = each(lambda m, x: jnp.where(m, x, 0.0), causal, sp)
        rev_cum = each(lambda x: _dot_exact_rhs(x, suffix, SB_CUM_PIECES), sp)
        scores(jnp.maximum(j - 1, 0), z_out)
        for i, g in enumerate(groups):
            rc = rev_cum[i]
            total = jnp.broadcast_to(rc[:, 0:1], (SB_ROWS, LANES))
            if diagonal:
                att = jnp.where(causal[i], jnp.exp(z_in[g, :] - rc), 0.0)
                run_ref[g, :] = total
                acc_ref[g, :] = jnp.zeros((SB_ROWS, LANES), F32)
            else:
                run = run_ref[g, :]
                att = jnp.exp(z_in[g, :] - rc - jnp.concatenate([run] * (tq // LANES), axis=1))
                run_ref[g, :] = run + total
                acc_ref[g, :] += pv[i]
            att_ref[g, :] = att.astype(BF16)

    scores(qb, za_ref)
    step(qb, za_ref, zb_ref, True)

    def body(it, carry):
        j = qb - 1 - 2 * it
        step(j, zb_ref, za_ref, False)
        step(j - 1, za_ref, zb_ref, False)
        return carry

    lax.fori_loop(0, qb // 2, body, 0)

    @pl.when((qb & 1) == 1)
    def _():
        step(0, zb_ref, za_ref, False)

    vs = key_tile(v_ref, 0)
    for g in groups:
        acc_ref[g, :] += _dot(att_ref[g, :], vs)
    lane = lax.broadcasted_iota(jnp.int32, (tq, LANES), 1)
    o_ref[...] = jnp.where(lane >= HEAD_DIM, acc_ref[tq:, :], acc_ref[:tq, :]).astype(BF16)


def _sb_attn(q, k, v):
    b, s, d = q.shape
    tq = SB_TQ
    ri = lax.broadcasted_iota(jnp.int32, (tq, tq), 0)
    ci = lax.broadcasted_iota(jnp.int32, (tq, tq), 1)
    suffix = jnp.where(ri >= ci, 1.0, 0.0).astype(BF16)
    qspec = pl.BlockSpec((None, tq, LANES), lambda i, p, j: (i, j, p))
    kspec = pl.BlockSpec((None, s, LANES), lambda i, p, j: (i, 0, p))
    block_bytes = 2 * _nbytes((tq, LANES), BF16) + 2 * _nbytes((s, LANES), BF16) + _nbytes((tq, tq), BF16)
    scratch = [pltpu.VMEM((2 * tq, LANES), F32), pltpu.VMEM((2 * tq, LANES), F32),
               pltpu.VMEM((2 * tq, tq), F32), pltpu.VMEM((2 * tq, tq), F32), pltpu.VMEM((2 * tq, tq), BF16)]
    scratch_bytes = _nbytes((2 * tq, LANES), F32) + 3 * _nbytes((2 * tq, tq), F32) + _nbytes((2 * tq, tq), BF16)
    return pl.pallas_call(
        _sb_attn_kernel,
        grid=(b, d // LANES, s // tq),
        in_specs=[qspec, kspec, kspec, pl.BlockSpec((tq, tq), lambda i, p, j: (0, 0))],
        out_specs=qspec,
        out_shape=jax.ShapeDtypeStruct((b, s, d), BF16),
        scratch_shapes=scratch,
        compiler_params=pltpu.CompilerParams(
            dimension_semantics=("parallel", "parallel", "arbitrary"),
            vmem_limit_bytes=_vmem_limit(block_bytes, scratch_bytes, 12 * _nbytes((2 * tq, tq), F32))),
        name="sb_attn",
    )(q, k, v, suffix)


def _outproj_mlp_kernel(h_ref, z_ref, wo_ref, g_ref, wup_ref, wdn_ref, o_ref, xn_ref):
    @pl.when(pl.program_id(1) == 0)
    def _():
        h1 = h_ref[...] + _dot(z_ref[...], wo_ref[...])
        o_ref[...] = h1
        xn_ref[...] = _rms(h1, g_ref[...]).astype(BF16)

    act = jnp.square(jnp.maximum(_dot(xn_ref[...], wup_ref[...]), 0.0)).astype(BF16)
    o_ref[...] += _dot(act, wdn_ref[...])


def _outproj_mlp(h2, z, w_o, g, w_up, w_dn):
    t, d = h2.shape
    f = w_up.shape[1]
    tm = min(MLP_TM, t)
    tf = min(MLP_TF, f)
    tok = pl.BlockSpec((tm, d), lambda i, j: (i, 0))
    in_specs = [
        tok, tok,
        pl.BlockSpec((d, d), lambda i, j: (0, 0)),
        pl.BlockSpec((1, d), lambda i, j: (0, 0)),
        pl.BlockSpec((d, tf), lambda i, j: (0, j)),
        pl.BlockSpec((tf, d), lambda i, j: (j, 0)),
    ]
    tok_bytes = _nbytes((tm, d), F32)
    block_bytes = 2 * tok_bytes + _nbytes((tm, d), BF16) + _nbytes((d, d), BF16) + 2 * _nbytes((d, tf), BF16)
    return pl.pallas_call(
        _outproj_mlp_kernel,
        grid=(t // tm, f // tf),
        in_specs=in_specs,
        out_specs=tok,
        out_shape=jax.ShapeDtypeStruct((t, d), F32),
        scratch_shapes=[pltpu.VMEM((tm, d), BF16)],
        compiler_params=pltpu.CompilerParams(
            dimension_semantics=("parallel", "arbitrary"),
            vmem_limit_bytes=_vmem_limit(block_bytes, _nbytes((tm, d), BF16),
                                         tok_bytes + 2 * _nbytes((tm, tf), F32))),
        name="outproj_mlp",
    )(h2, z, w_o, g, w_up, w_dn)


def _pad_cols(w, mult):
    pad = (-w.shape[1]) % mult
    return jnp.pad(w, ((0, 0), (0, pad)))


def _pad_rows(w, mult):
    pad = (-w.shape[0]) % mult
    return jnp.pad(w, ((0, pad), (0, 0)))


def _rows8(*rows):
    d = rows[0].shape[-1]
    out = jnp.zeros((SUBLANES, d), F32)
    return out.at[:len(rows)].set(jnp.stack([r.reshape(d) for r in rows]))


def _lora(w_in, w_out):
    return _pad_cols(w_in, LANES).astype(BF16), _pad_rows(w_out, LANES).astype(BF16)


def kernel(x, mix_norm, mlp_norm, mlp_up, mlp_down, rwkv_mu, rwkv_w_r, rwkv_w_k, rwkv_w_v, rwkv_w_o, rwkv_decay_w0, rwkv_decay_w1, rwkv_decay_w2, rwkv_iclr_a0, rwkv_iclr_a1, rwkv_iclr_a2, rwkv_gate_g1, rwkv_gate_g2, rwkv_k_k, rwkv_k_a, rwkv_r_k, rwkv_lnx_w, rwkv_lnx_b, rwkv_vres_v0, rwkv_vres_v1, rwkv_vres_v2, conv_w_in, conv_w, conv_w_out, sb_w_qkv, sb_q_norm, sb_k_norm, sb_w_o):
    b, s, d = x.shape
    depth = mix_norm.shape[0]
    n_heads = d // HEAD_DIM
    h = x
    v_first = None
    for i in range(depth):
        kind = i % 3
        j = i // 3
        g_mix = mix_norm[i].reshape(1, d)
        if kind == 0:
            vres = None
            v0 = jnp.zeros((d,), F32)
            if j > 0:
                v1, v2 = _lora(rwkv_vres_v1[j - 1], rwkv_vres_v2[j - 1])
                vres = (v1, v2, v_first)
                v0 = rwkv_vres_v0[j - 1]
            vecs = _rows8(mix_norm[i], rwkv_decay_w0[j], rwkv_iclr_a0[j], v0)
            mu8 = _rows8(*[rwkv_mu[j, m] for m in range(rwkv_mu.shape[1])])
            w1, w2 = _lora(rwkv_decay_w1[j], rwkv_decay_w2[j])
            a1, a2 = _lora(rwkv_iclr_a1[j], rwkv_iclr_a2[j])
            g1, g2 = _lora(rwkv_gate_g1[j], rwkv_gate_g2[j])
            r, k, v, lw, a, gate = _rwkv_in(
                h, vecs, mu8, rwkv_w_r[j].astype(BF16), rwkv_w_k[j].astype(BF16),
                rwkv_w_v[j].astype(BF16), w1, w2, a1, a2, g1, g2, vres)
            if j == 0:
                v_first = v
            pvec = _rows8(rwkv_k_k[j], rwkv_k_a[j], rwkv_r_k[j].reshape(d), rwkv_lnx_w[j], rwkv_lnx_b[j])
            z = _wkv(r, lw, k, v, a, gate, pvec, WKV_PASSES)
            w_o = rwkv_w_o[j]
        elif kind == 1:
            z = _conv_in(h, g_mix, conv_w_in[j].astype(BF16), _rows8(*[conv_w[j, m] for m in range(3)]))
            w_o = conv_w_out[j]
        else:
            qg = jnp.tile(sb_q_norm[j], n_heads).reshape(1, d)
            kg = jnp.tile(sb_k_norm[j], n_heads).reshape(1, d)
            q, k, v = _sb_qkv(h.reshape(b * s, d), g_mix, sb_w_qkv[j].astype(BF16), qg, kg)
            z = _sb_attn(q.reshape(b, s, d), k.reshape(b, s, d), v.reshape(b, s, d))
            w_o = sb_w_o[j]
        h = _outproj_mlp(
            h.reshape(b * s, d), z.reshape(b * s, d), w_o.astype(BF16), mlp_norm[i].reshape(1, d), mlp_up[i].astype(BF16), mlp_down[i].astype(BF16),
        ).reshape(b, s, d)
    return h
```

```python
import functools

import jax
import jax.numpy as jnp
from jax import lax
from jax.experimental import pallas as pl
from jax.experimental.pallas import tpu as pltpu

F32 = jnp.float32
BF16 = jnp.bfloat16

HEAD_DIM = 64
HEAD_SHIFT = 6
LANES = 128
SUBLANES = 8
RMS_EPS = 1e-6
GN_EPS = 64e-5
KK_EPS = 1e-24
SB_SCALE = HEAD_DIM ** -0.5
SB_CUM_PIECES = 1
SB_EXP_ZERO = 105.0
SB_ZMAX_SLACK = 1.02

V7X_VMEM_BYTES = 64 * 1024 * 1024
VMEM_LIMIT_CAP = V7X_VMEM_BYTES - 8 * 1024 * 1024

MLP_TM = 1024
MLP_TF = 512
RWKV_TS = 256
CONV_TS = 256
QKV_TM = 256
QKV_SUM_PIECES = 1
WKV_CHUNK = 64
WKV_PASSES = 1
WKV_CUM_PIECES = 2
WKV_SUM_PIECES = 1
SB_TQ = 256
SB_ROWS = 128


def _vmem_limit(block_bytes, scratch_bytes, temp_bytes):
    return int(min(2 * block_bytes + scratch_bytes + temp_bytes, VMEM_LIMIT_CAP))


def _nbytes(shape, dtype):
    n = 1
    for s in shape:
        n *= s
    return n * jnp.dtype(dtype).itemsize


def _dot(a, b):
    return jnp.dot(a, b, preferred_element_type=F32)


def _dot_nt(a, b):
    return lax.dot_general(a, b, (((1,), (1,)), ((), ())), preferred_element_type=F32)


def _dot_tn(a, b):
    return lax.dot_general(a, b, (((0,), (0,)), ((), ())), preferred_element_type=F32)


def _split(x, pieces):
    out = []
    for _ in range(pieces - 1):
        hi = x.astype(BF16)
        out.append(hi)
        x = x - hi.astype(F32)
    out.append(x.astype(BF16))
    return out


def _dot_exact_rhs(x, m_bf16, pieces, dot=_dot):
    acc = None
    for p in _split(x, pieces):
        t = dot(p, m_bf16)
        acc = t if acc is None else acc + t
    return acc


def _dot_exact_lhs(m_bf16, x, pieces):
    acc = None
    for p in _split(x, pieces):
        t = _dot(m_bf16, p)
        acc = t if acc is None else acc + t
    return acc


def _mm(a, b, passes, dot=_dot):
    if passes == 1:
        return dot(a.astype(BF16), b.astype(BF16))
    ah, al = _split(a, 2)
    bh, bl = _split(b, 2)
    return dot(ah, bh) + (dot(al, bh) + dot(ah, bl))


def _rms(x, g):
    ms = jnp.mean(x * x, axis=-1, keepdims=True)
    return x * lax.rsqrt(ms + RMS_EPS) * g


def _sigmoid(x):
    return 1.0 / (1.0 + jnp.exp(-x))


def _softplus(x):
    return jnp.maximum(x, 0.0) + jnp.log(1.0 + jnp.exp(-jnp.abs(x)))


def _head_sum_matrix(n):
    r = lax.broadcasted_iota(jnp.int32, (n, n), 0) >> HEAD_SHIFT
    c = lax.broadcasted_iota(jnp.int32, (n, n), 1) >> HEAD_SHIFT
    return jnp.where(r == c, 1.0, 0.0).astype(BF16)


def _shift_rows(x, carry8, n):
    rows = lax.broadcasted_iota(jnp.int32, (x.shape[0], 1), 0)
    out = pltpu.roll(x, n, axis=0)
    for i in range(n):
        out = jnp.where(rows == i, carry8[SUBLANES - n + i:SUBLANES - n + i + 1, :], out)
    return out


def _rwkv_in_kernel(*refs, has_vres):
    if has_vres:
        (h_ref, vec_ref, mu_ref, wr_ref, wk_ref, wv_ref, w1_ref, w2_ref, a1_ref, a2_ref,
         g1_ref, g2_ref, v1_ref, v2_ref, vf_ref,
         r_out, k_out, v_out, lw_out, a_out, g_out, carry_ref) = refs
    else:
        (h_ref, vec_ref, mu_ref, wr_ref, wk_ref, wv_ref, w1_ref, w2_ref, a1_ref, a2_ref,
         g1_ref, g2_ref,
         r_out, k_out, v_out, lw_out, a_out, g_out, carry_ref) = refs

    @pl.when(pl.program_id(1) == 0)
    def _():
        carry_ref[...] = jnp.zeros_like(carry_ref)

    ts = h_ref.shape[0]
    u = _rms(h_ref[...], vec_ref[0:1, :])
    prev = _shift_rows(u, carry_ref[...], 1)
    carry_ref[...] = u[ts - SUBLANES:, :]
    xx = prev - u

    def mix(i):
        return (u + xx * mu_ref[i:i + 1, :]).astype(BF16)

    r_out[...] = _dot(mix(0), wr_ref[...])
    k_out[...] = _dot(mix(2), wk_ref[...])
    xv = mix(3)
    v = _dot(xv, wv_ref[...])
    hw = jnp.tanh(_dot(mix(1), w1_ref[...])).astype(BF16)
    ha = _dot(mix(4), a1_ref[...]).astype(BF16)
    hg = _sigmoid(_dot(mix(5), g1_ref[...])).astype(BF16)
    if has_vres:
        hv = _dot(xv, v1_ref[...]).astype(BF16)

    log_w = -_softplus(-(vec_ref[1:2, :] + _dot(hw, w2_ref[...]))) - 0.5
    lw_out[...] = -jnp.exp(log_w)
    a_out[...] = _sigmoid(vec_ref[2:3, :] + _dot(ha, a2_ref[...]))
    g_out[...] = _dot(hg, g2_ref[...])
    if has_vres:
        v = v + (vf_ref[...] - v) * _sigmoid(vec_ref[3:4, :] + _dot(hv, v2_ref[...]))
    v_out[...] = v


def _rwkv_in(h, vecs, mu8, wr, wk, wv, w1, w2, a1, a2, g1, g2, vres):
    b, s, d = h.shape
    ts = min(RWKV_TS, s)
    tok = pl.BlockSpec((None, ts, d), lambda i, j: (i, j, 0))

    def full(x):
        return pl.BlockSpec(x.shape, lambda i, j: (0,) * x.ndim)

    weights = [vecs, mu8, wr, wk, wv, w1, w2, a1, a2, g1, g2]
    args = [h] + weights
    in_specs = [tok] + [full(w) for w in weights]
    if vres is not None:
        v1, v2, v_first = vres
        args += [v1, v2, v_first]
        in_specs += [full(v1), full(v2), tok]
    tok_bytes = _nbytes((ts, d), F32)
    block_bytes = sum(_nbytes(w.shape, w.dtype) for w in args[1:1 + len(weights) + (2 if vres else 0)])
    block_bytes += tok_bytes * (7 + (1 if vres else 0))
    out = jax.ShapeDtypeStruct((b, s, d), F32)
    return pl.pallas_call(
        functools.partial(_rwkv_in_kernel, has_vres=vres is not None),
        grid=(b, s // ts),
        in_specs=in_specs,
        out_specs=[tok] * 6,
        out_shape=[out] * 6,
        scratch_shapes=[pltpu.VMEM((SUBLANES, d), F32)],
        compiler_params=pltpu.CompilerParams(
            dimension_semantics=("parallel", "arbitrary"),
            vmem_limit_bytes=_vmem_limit(block_bytes, _nbytes((SUBLANES, d), F32), 12 * tok_bytes)),
        name="rwkv_in",
    )(*args)


def _wkv_kernel(r_ref, lw_ref, k_ref, v_ref, a_ref, g_ref, pv_ref, y_ref, s_ref, *, passes):
    c = r_ref.shape[0]
    n = 2 * c
    n_tiles = r_ref.shape[1] // LANES

    @pl.when(pl.program_id(1) == 0)
    def _():
        s_ref[...] = jnp.zeros_like(s_ref)

    first_head = lax.broadcasted_iota(jnp.int32, (c, LANES), 1) < HEAD_DIM

    def head_sums(x):
        lo = jnp.sum(jnp.where(first_head, x, 0.0), axis=-1, keepdims=True)
        hi = jnp.sum(jnp.where(first_head, 0.0, x), axis=-1, keepdims=True)
        return jnp.where(first_head, lo, hi)

    ti = lax.broadcasted_iota(jnp.int32, (c, c), 0)
    si = lax.broadcasted_iota(jnp.int32, (c, c), 1)
    lincl = jnp.where(si <= ti, 1.0, 0.0).astype(BF16)
    row = lax.broadcasted_iota(jnp.int32, (n, LANES), 0)
    lane = lax.broadcasted_iota(jnp.int32, (n, LANES), 1)
    own = (row >> (c.bit_length() - 1)) == (lane >> HEAD_SHIFT)
    ri = lax.broadcasted_iota(jnp.int32, (2 * n, 2 * n), 0)
    ci = lax.broadcasted_iota(jnp.int32, (2 * n, 2 * n), 1) & (c - 1)
    tri = ci < (ri & (c - 1)) + (ri >> (n.bit_length() - 1))
    inv_n = 1.0 / HEAD_DIM
    mm = functools.partial(_mm, passes=passes)
    mm_nt = functools.partial(_mm, passes=passes, dot=_dot_nt)
    mm_tn = functools.partial(_mm, passes=passes, dot=_dot_tn)

    def stack(x):
        return jnp.where(own, jnp.concatenate([x, x], axis=0), 0.0)

    tiles = range(n_tiles)
    sls = [slice(t * LANES, (t + 1) * LANES) for t in tiles]

    def each(f, *cols):
        return [f(*args) for args in zip(*cols)]

    def load(ref, rows=slice(None)):
        return [ref[rows, sl] for sl in sls]

    r, lw, k_raw, v, a = load(r_ref), load(lw_ref), load(k_ref), load(v_ref), load(a_ref)
    k_k, k_a, r_k = load(pv_ref, slice(0, 1)), load(pv_ref, slice(1, 2)), load(pv_ref, slice(2, 3))
    lnx_w, lnx_b = load(pv_ref, slice(3, 4)), load(pv_ref, slice(4, 5))

    kk = each(lambda x, g: x * g, k_raw, k_k)
    kk_ss = each(lambda x: head_sums(x * x), kk)
    kk = each(lambda x, ss: x * lax.rsqrt(jnp.maximum(ss, KK_EPS)), kk, kk_ss)
    k = each(lambda x, ai, g: x * (1.0 + (ai - 1.0) * g), k_raw, a, k_a)

    cum = each(lambda x: _dot_exact_lhs(lincl, x, WKV_CUM_PIECES), lw)
    g_incl = each(jnp.exp, cum)
    g_excl = each(lambda cs, x: jnp.exp(cs - x), cum, lw)
    g_inv = each(lambda cs: jnp.exp(-cs), cum)
    g_end = each(lambda g: g[c - 1:c, :], g_incl)

    ar = each(lambda kki, ri_, ge, gi: jnp.concatenate([stack(-kki * ge), stack(ri_ * gi)], axis=0),
              kk, r, g_excl, g_incl)
    bk = each(lambda kki, ai, ki, gv: jnp.concatenate([stack(kki * ai * gv), stack(ki * gv)], axis=0),
              kk, a, k, g_inv)
    vs = each(stack, v)

    s0 = [s_ref[t] for t in tiles]
    prod = each(lambda x, y: jnp.where(tri, mm_nt(x, y), 0.0), ar, bk)
    from_s0 = each(mm_nt, ar, s0)

    x = each(lambda fs, pr, vi: fs[:n, :] + mm(pr[:n, n:], vi), from_s0, prod, vs)
    p = each(lambda pr: pr[:n, :n], prod)
    steps = c.bit_length() - 1
    for i in range(steps - 1):
        px = each(lambda xi, pi: mm(pi, jnp.concatenate([xi, pi], axis=1)), x, p)
        x = each(lambda xi, pxi: xi + pxi[:, :LANES], x, px)
        p = each(lambda pxi: pxi[:, LANES:], px)
    x = each(lambda xi, pi: xi + mm(pi, xi), x, p)
    uv = each(lambda xi, vi: jnp.concatenate([xi, vi], axis=0), x, vs)

    yh = each(lambda fs, pr, uvi: fs[n:, :] + mm(pr[n:, :], uvi), from_s0, prod, uv)
    y = each(lambda yi: yi[:c, :] + yi[c:, :], yh)
    s_new = each(lambda si_, ge, uvi, bki: si_ * ge + mm_tn(uvi, bki * ge), s0, g_end, uv, bk)
    for t in tiles:
        s_ref[t] = s_new[t]

    mean = each(lambda yi: head_sums(yi) * inv_n, y)
    yc = each(lambda yi, m: yi - m, y, mean)
    var = each(lambda yi: head_sums(yi * yi) * inv_n, yc)
    bonus = each(lambda ri_, ki, g, vi: head_sums(ri_ * ki * g) * vi, r, k, r_k, v)
    out = each(lambda yi, vr, w, b_, bo: yi * lax.rsqrt(vr + GN_EPS) * w + b_ + bo, yc, var, lnx_w, lnx_b, bonus)
    for t in tiles:
        y_ref[:, sls[t]] = (out[t] * g_ref[:, sls[t]]).astype(BF16)


def _wkv(r, lw, k, v, a, g, pvec, passes):
    b, s, d = r.shape
    c = WKV_CHUNK
    tok = pl.BlockSpec((None, c, d), lambda i, j: (i, j, 0))
    par = pl.BlockSpec((SUBLANES, d), lambda i, j: (0, 0))
    tok_bytes = _nbytes((c, d), F32)
    sq_bytes = _nbytes((2 * c, 2 * c), F32)
    n_tiles = d // LANES
    return pl.pallas_call(
        functools.partial(_wkv_kernel, passes=passes),
        grid=(b, s // c),
        in_specs=[tok] * 6 + [par],
        out_specs=tok,
        out_shape=jax.ShapeDtypeStruct((b, s, d), BF16),
        scratch_shapes=[pltpu.VMEM((n_tiles, LANES, LANES), F32)],
        compiler_params=pltpu.CompilerParams(
            dimension_semantics=("parallel", "arbitrary"),
            vmem_limit_bytes=_vmem_limit(8 * tok_bytes, n_tiles * sq_bytes, n_tiles * 48 * sq_bytes)),
        name="wkv",
    )(r, lw, k, v, a, g, pvec)


def _conv_in_kernel(h_ref, g_ref, win_ref, cw_ref, z_out, carry_ref):
    @pl.when(pl.program_id(1) == 0)
    def _():
        carry_ref[...] = jnp.zeros_like(carry_ref)

    ts, d = h_ref.shape
    u = _rms(h_ref[...], g_ref[...]).astype(BF16)
    z = _dot(u, win_ref[:, d:2 * d]) * _dot(u, win_ref[:, 2 * d:])
    b_gate = _dot(u, win_ref[:, :d])
    carry = carry_ref[...]
    z1 = _shift_rows(z, carry, 1)
    z2 = _shift_rows(z, carry, 2)
    carry_ref[...] = z[ts - SUBLANES:, :]
    zc = z2 * cw_ref[0:1, :] + z1 * cw_ref[1:2, :] + z * cw_ref[2:3, :]
    z_out[...] = (b_gate * zc).astype(BF16)


def _conv_in(h, g, w_in, cw8):
    b, s, d = h.shape
    ts = min(CONV_TS, s)
    tok = pl.BlockSpec((None, ts, d), lambda i, j: (i, j, 0))
    tok_bytes = _nbytes((ts, d), F32)
    block_bytes = 2 * tok_bytes + _nbytes(w_in.shape, BF16) + _nbytes(cw8.shape, F32)
    return pl.pallas_call(
        _conv_in_kernel,
        grid=(b, s // ts),
        in_specs=[tok,
                  pl.BlockSpec(g.shape, lambda i, j: (0, 0)),
                  pl.BlockSpec(w_in.shape, lambda i, j: (0, 0)),
                  pl.BlockSpec(cw8.shape, lambda i, j: (0, 0))],
        out_specs=tok,
        out_shape=jax.ShapeDtypeStruct((b, s, d), BF16),
        scratch_shapes=[pltpu.VMEM((SUBLANES, d), F32)],
        compiler_params=pltpu.CompilerParams(
            dimension_semantics=("parallel", "arbitrary"),
            vmem_limit_bytes=_vmem_limit(block_bytes, _nbytes((SUBLANES, d), F32), 10 * tok_bytes)),
        name="conv_in",
    )(h, g, w_in, cw8)


def _sb_qkv_kernel(h_ref, g_ref, w_ref, qg_ref, kg_ref, q_out, k_out, v_out):
    d = h_ref.shape[1]
    u = _rms(h_ref[...], g_ref[...]).astype(BF16)
    hsum = _head_sum_matrix(2 * LANES)
    inv_n = 1.0 / HEAD_DIM

    def head_norm(x, gain):
        cols = []
        for j in range(0, d, 2 * LANES):
            xs = x[:, j:j + 2 * LANES]
            ms = _dot_exact_rhs(xs * xs, hsum, QKV_SUM_PIECES) * inv_n
            cols.append(xs * lax.rsqrt(ms + RMS_EPS))
        return jnp.concatenate(cols, axis=1) * gain

    q = _dot(u, w_ref[:, :d])
    k = _dot(u, w_ref[:, d:2 * d])
    q_out[...] = head_norm(q, qg_ref[...]).astype(BF16)
    v = _dot(u, w_ref[:, 2 * d:])
    k_out[...] = head_norm(k, kg_ref[...]).astype(BF16)
    v_out[...] = v.astype(BF16)


def _sb_qkv(h2, g, w_qkv, qg, kg):
    t, d = h2.shape
    tm = min(QKV_TM, t)
    tok = pl.BlockSpec((tm, d), lambda i: (i, 0))
    vec = pl.BlockSpec((1, d), lambda i: (0, 0))
    tok_bytes = _nbytes((tm, d), F32)
    block_bytes = 4 * tok_bytes + _nbytes(w_qkv.shape, BF16) + 3 * _nbytes((1, d), F32)
    out = jax.ShapeDtypeStruct((t, d), BF16)
    return pl.pallas_call(
        _sb_qkv_kernel,
        grid=(t // tm,),
        in_specs=[tok, vec, pl.BlockSpec(w_qkv.shape, lambda i: (0, 0)), vec, vec],
        out_specs=[tok] * 3,
        out_shape=[out] * 3,
        compiler_params=pltpu.CompilerParams(
            dimension_semantics=("parallel",),
            vmem_limit_bytes=_vmem_limit(block_bytes, 0, 10 * tok_bytes)),
        name="sb_qkv",
    )(h2, g, w_qkv, qg, kg)


def _sb_attn_kernel(zmax_ref, q_ref, k_ref, v_ref, suffix_ref, o_ref, acc_ref, run_ref, za_ref, zb_ref, att_ref):
    tq = q_ref.shape[0]
    qb = pl.program_id(2)
    rows = lax.broadcasted_iota(jnp.int32, (2 * tq, LANES), 0)
    lanes = lax.broadcasted_iota(jnp.int32, (2 * tq, LANES), 1)
    own = (rows >> (tq.bit_length() - 1)) == (lanes >> HEAD_SHIFT)
    q = q_ref[...].astype(F32)
    qs = (jnp.where(own, jnp.concatenate([q, q], axis=0), 0.0) * SB_SCALE).astype(BF16)
    suffix = suffix_ref[...]

    groups = [slice(g * SB_ROWS, (g + 1) * SB_ROWS) for g in range(2 * tq // SB_ROWS)]

    def each(f, *cols):
        return [f(*args) for args in zip(*cols)]

    def key_tile(ref, j):
        return ref[pl.ds(pl.multiple_of(j * tq, tq), tq), :]

    def scores(j, z_out):
        ks = key_tile(k_ref, j)
        for g in groups:
            z_out[g, :] = _dot_nt(qs[g, :], ks)

    def step(j, z_in, z_out, diagonal):
        if not diagonal:
            vs = key_tile(v_ref, j + 1)
            pv = [_dot(att_ref[g, :], vs) for g in groups]
        sp = [_softplus(z_in[g, :]) for g in groups]
        if diagonal:
            t_loc = lax.broadcasted_iota(jnp.int32, (SB_ROWS, tq), 0)
            s_loc = lax.broadcasted_iota(jnp.int32, (SB_ROWS, tq), 1)
            causal = [s_loc < t_loc + (g.start & (tq - 1)) for g in groups]
            sp = each(lambda m, x: jnp.where(m, x, 0.0), causal, sp)
        rev_cum = each(lambda x: _dot_exact_rhs(x, suffix, SB_CUM_PIECES), sp)
        scores(jnp.maximum(j - 1, 0), z_out)
        for i, g in enumerate(groups):
            rc = rev_cum[i]
            total = jnp.broadcast_to(rc[:, 0:1], (SB_ROWS, LANES))
            if diagonal:
                att = jnp.where(causal[i], jnp.exp(z_in[g, :] - rc), 0.0)
                run_ref[g, :] = total
                acc_ref[g, :] = jnp.zeros((SB_ROWS, LANES), F32)
            else:
                run = run_ref[g, :]
                att = jnp.exp(z_in[g, :] - rc - jnp.concatenate([run] * (tq // LANES), axis=1))
                run_ref[g, :] = run + total
                acc_ref[g, :] += pv[i]
            att_ref[g, :] = att.astype(BF16)

    stop_at = zmax_ref[0] + SB_EXP_ZERO

    def run_min():
        return jnp.min(run_ref[...])

    scores(qb, za_ref)
    step(qb, za_ref, zb_ref, True)

    @pl.when(qb >= 1)
    def _():
        step(qb - 1, zb_ref, za_ref, False)

    def more(state):
        j, low = state
        return jnp.logical_and(j >= 1, low < stop_at)

    def pair(state):
        j, _ = state
        step(j, za_ref, zb_ref, False)
        step(j - 1, zb_ref, za_ref, False)
        return j - 2, run_min()

    j, low = lax.while_loop(more, pair, (jnp.maximum(qb - 2, -1), run_min()))
    last_tile = jnp.logical_and(j == 0, low < stop_at)

    @pl.when(last_tile)
    def _():
        step(0, za_ref, zb_ref, False)

    vs = key_tile(v_ref, jnp.where(last_tile, 0, j + 1))
    for g in groups:
        acc_ref[g, :] += _dot(att_ref[g, :], vs)
    lane = lax.broadcasted_iota(jnp.int32, (tq, LANES), 1)
    o_ref[...] = jnp.where(lane >= HEAD_DIM, acc_ref[tq:, :], acc_ref[:tq, :]).astype(BF16)


def _sb_attn(q, k, v, z_max):
    b, s, d = q.shape
    tq = SB_TQ
    ri = lax.broadcasted_iota(jnp.int32, (tq, tq), 0)
    ci = lax.broadcasted_iota(jnp.int32, (tq, tq), 1)
    suffix = jnp.where(ri >= ci, 1.0, 0.0).astype(BF16)
    qspec = pl.BlockSpec((None, tq, LANES), lambda i, p, j: (i, j, p))
    kspec = pl.BlockSpec((None, s, LANES), lambda i, p, j: (i, 0, p))
    block_bytes = 2 * _nbytes((tq, LANES), BF16) + 2 * _nbytes((s, LANES), BF16) + _nbytes((tq, tq), BF16)
    scratch = [pltpu.VMEM((2 * tq, LANES), F32), pltpu.VMEM((2 * tq, LANES), F32),
               pltpu.VMEM((2 * tq, tq), F32), pltpu.VMEM((2 * tq, tq), F32), pltpu.VMEM((2 * tq, tq), BF16)]
    scratch_bytes = _nbytes((2 * tq, LANES), F32) + 3 * _nbytes((2 * tq, tq), F32) + _nbytes((2 * tq, tq), BF16)
    return pl.pallas_call(
        _sb_attn_kernel,
        grid=(b, d // LANES, s // tq),
        in_specs=[pl.BlockSpec(memory_space=pltpu.SMEM), qspec, kspec, kspec,
                  pl.BlockSpec((tq, tq), lambda i, p, j: (0, 0))],
        out_specs=qspec,
        out_shape=jax.ShapeDtypeStruct((b, s, d), BF16),
        scratch_shapes=scratch,
        compiler_params=pltpu.CompilerParams(
            dimension_semantics=("parallel", "parallel", "arbitrary"),
            vmem_limit_bytes=_vmem_limit(block_bytes, scratch_bytes, 12 * _nbytes((2 * tq, tq), F32))),
        name="sb_attn",
    )(z_max.reshape(1).astype(F32), q, k, v, suffix)


def _outproj_mlp_kernel(h_ref, z_ref, wo_ref, g_ref, wup_ref, wdn_ref, o_ref, xn_ref):
    @pl.when(pl.program_id(1) == 0)
    def _():
        h1 = h_ref[...] + _dot(z_ref[...], wo_ref[...])
        o_ref[...] = h1
        xn_ref[...] = _rms(h1, g_ref[...]).astype(BF16)

    act = jnp.square(jnp.maximum(_dot(xn_ref[...], wup_ref[...]), 0.0)).astype(BF16)
    o_ref[...] += _dot(act, wdn_ref[...])


def _outproj_mlp(h2, z, w_o, g, w_up, w_dn):
    t, d = h2.shape
    f = w_up.shape[1]
    tm = min(MLP_TM, t)
    tf = min(MLP_TF, f)
    tok = pl.BlockSpec((tm, d), lambda i, j: (i, 0))
    in_specs = [
        tok, tok,
        pl.BlockSpec((d, d), lambda i, j: (0, 0)),
        pl.BlockSpec((1, d), lambda i, j: (0, 0)),
        pl.BlockSpec((d, tf), lambda i, j: (0, j)),
        pl.BlockSpec((tf, d), lambda i, j: (j, 0)),
    ]
    tok_bytes = _nbytes((tm, d), F32)
    block_bytes = 2 * tok_bytes + _nbytes((tm, d), BF16) + _nbytes((d, d), BF16) + 2 * _nbytes((d, tf), BF16)
    return pl.pallas_call(
        _outproj_mlp_kernel,
        grid=(t // tm, f // tf),
        in_specs=in_specs,
        out_specs=tok,
        out_shape=jax.ShapeDtypeStruct((t, d), F32),
        scratch_shapes=[pltpu.VMEM((tm, d), BF16)],
        compiler_params=pltpu.CompilerParams(
            dimension_semantics=("parallel", "arbitrary"),
            vmem_limit_bytes=_vmem_limit(block_bytes, _nbytes((tm, d), BF16),
                                         tok_bytes + 2 * _nbytes((tm, tf), F32))),
        name="outproj_mlp",
    )(h2, z, w_o, g, w_up, w_dn)


def _pad_cols(w, mult):
    pad = (-w.shape[1]) % mult
    return jnp.pad(w, ((0, 0), (0, pad)))


def _pad_rows(w, mult):
    pad = (-w.shape[0]) % mult
    return jnp.pad(w, ((0, pad), (0, 0)))


def _rows8(*rows):
    d = rows[0].shape[-1]
    out = jnp.zeros((SUBLANES, d), F32)
    return out.at[:len(rows)].set(jnp.stack([r.reshape(d) for r in rows]))


def _lora(w_in, w_out):
    return _pad_cols(w_in, LANES).astype(BF16), _pad_rows(w_out, LANES).astype(BF16)


def kernel(x, mix_norm, mlp_norm, mlp_up, mlp_down, rwkv_mu, rwkv_w_r, rwkv_w_k, rwkv_w_v, rwkv_w_o, rwkv_decay_w0, rwkv_decay_w1, rwkv_decay_w2, rwkv_iclr_a0, rwkv_iclr_a1, rwkv_iclr_a2, rwkv_gate_g1, rwkv_gate_g2, rwkv_k_k, rwkv_k_a, rwkv_r_k, rwkv_lnx_w, rwkv_lnx_b, rwkv_vres_v0, rwkv_vres_v1, rwkv_vres_v2, conv_w_in, conv_w, conv_w_out, sb_w_qkv, sb_q_norm, sb_k_norm, sb_w_o):
    b, s, d = x.shape
    depth = mix_norm.shape[0]
    n_heads = d // HEAD_DIM
    h = x
    v_first = None
    for i in range(depth):
        kind = i % 3
        j = i // 3
        g_mix = mix_norm[i].reshape(1, d)
        if kind == 0:
            vres = None
            v0 = jnp.zeros((d,), F32)
            if j > 0:
                v1, v2 = _lora(rwkv_vres_v1[j - 1], rwkv_vres_v2[j - 1])
                vres = (v1, v2, v_first)
                v0 = rwkv_vres_v0[j - 1]
            vecs = _rows8(mix_norm[i], rwkv_decay_w0[j], rwkv_iclr_a0[j], v0)
            mu8 = _rows8(*[rwkv_mu[j, m] for m in range(rwkv_mu.shape[1])])
            w1, w2 = _lora(rwkv_decay_w1[j], rwkv_decay_w2[j])
            a1, a2 = _lora(rwkv_iclr_a1[j], rwkv_iclr_a2[j])
            g1, g2 = _lora(rwkv_gate_g1[j], rwkv_gate_g2[j])
            r, k, v, lw, a, gate = _rwkv_in(
                h, vecs, mu8, rwkv_w_r[j].astype(BF16), rwkv_w_k[j].astype(BF16),
                rwkv_w_v[j].astype(BF16), w1, w2, a1, a2, g1, g2, vres)
            if j == 0:
                v_first = v
            pvec = _rows8(rwkv_k_k[j], rwkv_k_a[j], rwkv_r_k[j].reshape(d), rwkv_lnx_w[j], rwkv_lnx_b[j])
            z = _wkv(r, lw, k, v, a, gate, pvec, WKV_PASSES)
            w_o = rwkv_w_o[j]
        elif kind == 1:
            z = _conv_in(h, g_mix, conv_w_in[j].astype(BF16), _rows8(*[conv_w[j, m] for m in range(3)]))
            w_o = conv_w_out[j]
        else:
            qg = jnp.tile(sb_q_norm[j], n_heads).reshape(1, d)
            kg = jnp.tile(sb_k_norm[j], n_heads).reshape(1, d)
            q, k, v = _sb_qkv(h.reshape(b * s, d), g_mix, sb_w_qkv[j].astype(BF16), qg, kg)
            z_max = (SB_ZMAX_SLACK * HEAD_DIM * SB_SCALE
                     * jnp.max(jnp.abs(sb_q_norm[j])) * jnp.max(jnp.abs(sb_k_norm[j])))
            z = _sb_attn(q.reshape(b, s, d), k.reshape(b, s, d), v.reshape(b, s, d), z_max)
            w_o = sb_w_o[j]
        h = _outproj_mlp(
            h.reshape(b * s, d), z.reshape(b * s, d), w_o.astype(BF16), mlp_norm[i].reshape(1, d), mlp_up[i].astype(BF16), mlp_down[i].astype(BF16),
        ).reshape(b, s, d)
    return h
```

```python
import functools

import jax
import jax.numpy as jnp
from jax import lax
from jax.experimental import pallas as pl
from jax.experimental.pallas import tpu as pltpu

F32 = jnp.float32
BF16 = jnp.bfloat16

HEAD_DIM = 64
HEAD_SHIFT = 6
LANES = 128
SUBLANES = 8
RMS_EPS = 1e-6
GN_EPS = 64e-5
KK_EPS = 1e-24
SB_SCALE = HEAD_DIM ** -0.5
SB_CUM_PIECES = 1
SB_EXP_ZERO = 105.0
SB_ZMAX_SLACK = 1.02

V7X_VMEM_BYTES = 64 * 1024 * 1024
VMEM_LIMIT_CAP = V7X_VMEM_BYTES - 8 * 1024 * 1024

MLP_TM = 1024
MLP_TF = 1024
RWKV_TS = 256
CONV_TS = 512
QKV_TM = 512
QKV_SUM_PIECES = 1
WKV_CHUNK = 64
WKV_ROWS = 2
WKV_PASSES = 1
WKV_CUM_PIECES = 2
WKV_SUM_PIECES = 1
SB_TQ = 256
SB_ROWS = 128


def _vmem_limit(block_bytes, scratch_bytes, temp_bytes):
    return int(min(2 * block_bytes + scratch_bytes + temp_bytes, VMEM_LIMIT_CAP))


def _nbytes(shape, dtype):
    n = 1
    for s in shape:
        n *= s
    return n * jnp.dtype(dtype).itemsize


def _dot(a, b):
    return jnp.dot(a, b, preferred_element_type=F32)


def _dot_nt(a, b):
    return lax.dot_general(a, b, (((1,), (1,)), ((), ())), preferred_element_type=F32)


def _dot_tn(a, b):
    return lax.dot_general(a, b, (((0,), (0,)), ((), ())), preferred_element_type=F32)


def _split(x, pieces):
    out = []
    for _ in range(pieces - 1):
        hi = x.astype(BF16)
        out.append(hi)
        x = x - hi.astype(F32)
    out.append(x.astype(BF16))
    return out


def _dot_exact_rhs(x, m_bf16, pieces, dot=_dot):
    acc = None
    for p in _split(x, pieces):
        t = dot(p, m_bf16)
        acc = t if acc is None else acc + t
    return acc


def _dot_exact_lhs(m_bf16, x, pieces):
    acc = None
    for p in _split(x, pieces):
        t = _dot(m_bf16, p)
        acc = t if acc is None else acc + t
    return acc


def _mm(a, b, passes, dot=_dot):
    if passes == 1:
        return dot(a.astype(BF16), b.astype(BF16))
    ah, al = _split(a, 2)
    bh, bl = _split(b, 2)
    return dot(ah, bh) + (dot(al, bh) + dot(ah, bl))


def _rms(x, g):
    ms = jnp.mean(x * x, axis=-1, keepdims=True)
    return x * lax.rsqrt(ms + RMS_EPS) * g


def _sigmoid(x):
    return 1.0 / (1.0 + jnp.exp(-x))


def _softplus(x):
    return jnp.maximum(x, 0.0) + jnp.log(1.0 + jnp.exp(-jnp.abs(x)))


def _head_sum_matrix(n):
    r = lax.broadcasted_iota(jnp.int32, (n, n), 0) >> HEAD_SHIFT
    c = lax.broadcasted_iota(jnp.int32, (n, n), 1) >> HEAD_SHIFT
    return jnp.where(r == c, 1.0, 0.0).astype(BF16)


def _shift_rows(x, carry8, n):
    rows = lax.broadcasted_iota(jnp.int32, (x.shape[0], 1), 0)
    out = pltpu.roll(x, n, axis=0)
    for i in range(n):
        out = jnp.where(rows == i, carry8[SUBLANES - n + i:SUBLANES - n + i + 1, :], out)
    return out


def _rwkv_in_kernel(*refs, has_vres):
    if has_vres:
        (h_ref, vec_ref, mu_ref, wr_ref, wk_ref, wv_ref, w1_ref, w2_ref, a1_ref, a2_ref,
         g1_ref, g2_ref, v1_ref, v2_ref, vf_ref,
         r_out, k_out, v_out, lw_out, a_out, g_out, carry_ref) = refs
    else:
        (h_ref, vec_ref, mu_ref, wr_ref, wk_ref, wv_ref, w1_ref, w2_ref, a1_ref, a2_ref,
         g1_ref, g2_ref,
         r_out, k_out, v_out, lw_out, a_out, g_out, carry_ref) = refs

    @pl.when(pl.program_id(1) == 0)
    def _():
        carry_ref[...] = jnp.zeros_like(carry_ref)

    ts = h_ref.shape[0]
    u = _rms(h_ref[...], vec_ref[0:1, :])
    prev = _shift_rows(u, carry_ref[...], 1)
    carry_ref[...] = u[ts - SUBLANES:, :]
    xx = prev - u

    def mix(i):
        return (u + xx * mu_ref[i:i + 1, :]).astype(BF16)

    r_out[...] = _dot(mix(0), wr_ref[...])
    k_out[...] = _dot(mix(2), wk_ref[...])
    xv = mix(3)
    v = _dot(xv, wv_ref[...])
    hw = jnp.tanh(_dot(mix(1), w1_ref[...])).astype(BF16)
    ha = _dot(mix(4), a1_ref[...]).astype(BF16)
    hg = _sigmoid(_dot(mix(5), g1_ref[...])).astype(BF16)
    if has_vres:
        hv = _dot(xv, v1_ref[...]).astype(BF16)

    log_w = -_softplus(-(vec_ref[1:2, :] + _dot(hw, w2_ref[...]))) - 0.5
    lw_out[...] = -jnp.exp(log_w)
    a_out[...] = _sigmoid(vec_ref[2:3, :] + _dot(ha, a2_ref[...]))
    g_out[...] = _dot(hg, g2_ref[...])
    if has_vres:
        v = v + (vf_ref[...] - v) * _sigmoid(vec_ref[3:4, :] + _dot(hv, v2_ref[...]))
    v_out[...] = v


def _rwkv_in(h, vecs, mu8, wr, wk, wv, w1, w2, a1, a2, g1, g2, vres):
    b, s, d = h.shape
    ts = min(RWKV_TS, s)
    tok = pl.BlockSpec((None, ts, d), lambda i, j: (i, j, 0))

    def full(x):
        return pl.BlockSpec(x.shape, lambda i, j: (0,) * x.ndim)

    weights = [vecs, mu8, wr, wk, wv, w1, w2, a1, a2, g1, g2]
    args = [h] + weights
    in_specs = [tok] + [full(w) for w in weights]
    if vres is not None:
        v1, v2, v_first = vres
        args += [v1, v2, v_first]
        in_specs += [full(v1), full(v2), tok]
    tok_bytes = _nbytes((ts, d), F32)
    block_bytes = sum(_nbytes(w.shape, w.dtype) for w in args[1:1 + len(weights) + (2 if vres else 0)])
    block_bytes += tok_bytes * (7 + (1 if vres else 0))
    out = jax.ShapeDtypeStruct((b, s, d), F32)
    return pl.pallas_call(
        functools.partial(_rwkv_in_kernel, has_vres=vres is not None),
        grid=(b, s // ts),
        in_specs=in_specs,
        out_specs=[tok] * 6,
        out_shape=[out] * 6,
        scratch_shapes=[pltpu.VMEM((SUBLANES, d), F32)],
        compiler_params=pltpu.CompilerParams(
            dimension_semantics=("parallel", "arbitrary"),
            vmem_limit_bytes=_vmem_limit(block_bytes, _nbytes((SUBLANES, d), F32), 12 * tok_bytes)),
        name="rwkv_in",
    )(*args)


def _wkv_kernel(r_ref, lw_ref, k_ref, v_ref, a_ref, g_ref, pv_ref, y_ref, s_ref, *, passes):
    n_rows, c, d_model = r_ref.shape
    n = 2 * c
    n_tiles = d_model // LANES

    @pl.when(pl.program_id(1) == 0)
    def _():
        s_ref[...] = jnp.zeros_like(s_ref)

    first_head = lax.broadcasted_iota(jnp.int32, (c, LANES), 1) < HEAD_DIM

    def head_sums(x):
        lo = jnp.sum(jnp.where(first_head, x, 0.0), axis=-1, keepdims=True)
        hi = jnp.sum(jnp.where(first_head, 0.0, x), axis=-1, keepdims=True)
        return jnp.where(first_head, lo, hi)

    ti = lax.broadcasted_iota(jnp.int32, (c, c), 0)
    si = lax.broadcasted_iota(jnp.int32, (c, c), 1)
    lincl = jnp.where(si <= ti, 1.0, 0.0).astype(BF16)
    row = lax.broadcasted_iota(jnp.int32, (n, LANES), 0)
    lane = lax.broadcasted_iota(jnp.int32, (n, LANES), 1)
    own = (row >> (c.bit_length() - 1)) == (lane >> HEAD_SHIFT)
    ri = lax.broadcasted_iota(jnp.int32, (2 * n, 2 * n), 0)
    ci = lax.broadcasted_iota(jnp.int32, (2 * n, 2 * n), 1) & (c - 1)
    tri = ci < (ri & (c - 1)) + (ri >> (n.bit_length() - 1))
    inv_n = 1.0 / HEAD_DIM
    mm = functools.partial(_mm, passes=passes)
    mm_nt = functools.partial(_mm, passes=passes, dot=_dot_nt)
    mm_tn = functools.partial(_mm, passes=passes, dot=_dot_tn)

    def stack(x):
        return jnp.where(own, jnp.concatenate([x, x], axis=0), 0.0)

    tiles = list(range(n_tiles))
    sls = [slice(t * LANES, (t + 1) * LANES) for t in tiles]
    steps = c.bit_length() - 1
    parts = [tiles[i * n_tiles // steps:(i + 1) * n_tiles // steps] for i in range(steps)]

    def each(f, *cols):
        return [f(*args) for args in zip(*cols)]

    def prologue(bi, ts):
        def load(ref):
            return [ref[bi, :, sls[t]] for t in ts]

        def param(i):
            return [pv_ref[i:i + 1, sls[t]] for t in ts]

        r, lw, k_raw, v, a = load(r_ref), load(lw_ref), load(k_ref), load(v_ref), load(a_ref)
        kk = each(lambda x, g: x * g, k_raw, param(0))
        kk_ss = each(lambda x: head_sums(x * x), kk)
        kk = each(lambda x, ss: x * lax.rsqrt(jnp.maximum(ss, KK_EPS)), kk, kk_ss)
        k = each(lambda x, ai, g: x * (1.0 + (ai - 1.0) * g), k_raw, a, param(1))

        cum = each(lambda x: _dot_exact_lhs(lincl, x, WKV_CUM_PIECES), lw)
        g_incl = each(jnp.exp, cum)
        g_excl = each(lambda cs, x: jnp.exp(cs - x), cum, lw)
        g_inv = each(lambda cs: jnp.exp(-cs), cum)
        g_end = each(lambda g: g[c - 1:c, :], g_incl)

        ar = each(lambda kki, ri_, ge, gi: jnp.concatenate([stack(-kki * ge), stack(ri_ * gi)], axis=0),
                  kk, r, g_excl, g_incl)
        bk = each(lambda kki, ai, ki, gv: jnp.concatenate([stack(kki * ai * gv), stack(ki * gv)], axis=0),
                  kk, a, k, g_inv)
        vs = each(stack, v)
        s0 = [s_ref[bi, t] for t in ts]
        prod = each(lambda x, y: jnp.where(tri, mm_nt(x, y), 0.0), ar, bk)
        from_s0 = each(mm_nt, ar, s0)
        bonus = each(lambda ri_, ki, g, vi: head_sums(ri_ * ki * g) * vi, r, k, param(2), v)
        keys = ("bk", "vs", "s0", "g_end", "prod", "from_s0", "bonus")
        return [dict(zip(keys, vals)) for vals in zip(bk, vs, s0, g_end, prod, from_s0, bonus)]

    def solve_start(pro):
        x = [d["from_s0"][:n, :] + mm(d["prod"][:n, n:], d["vs"]) for d in pro]
        return x, [d["prod"][:n, :n] for d in pro]

    def solve_stage(x, p, last):
        if last:
            return each(lambda xi, pi: xi + mm(pi, xi), x, p), None
        px = each(lambda xi, pi: mm(pi, jnp.concatenate([xi, pi], axis=1)), x, p)
        return each(lambda xi, pxi: xi + pxi[:, :LANES], x, px), each(lambda pxi: pxi[:, LANES:], px)

    def finish(bi, pro, x):
        uv = each(lambda xi, d: jnp.concatenate([xi, d["vs"]], axis=0), x, pro)
        yh = each(lambda d, uvi: d["from_s0"][n:, :] + mm(d["prod"][n:, :], uvi), pro, uv)
        s_new = each(lambda d, uvi: d["s0"] * d["g_end"] + mm_tn(uvi, d["bk"] * d["g_end"]), pro, uv)
        for t in tiles:
            s_ref[bi, t] = s_new[t]
        return each(lambda yi: yi[:c, :] + yi[c:, :], yh)

    def epilogue(bi, ts, pro, y):
        yt = [y[t] for t in ts]
        mean = each(lambda yi: head_sums(yi) * inv_n, yt)
        yc = each(lambda yi, m: yi - m, yt, mean)
        var = each(lambda yi: head_sums(yi * yi) * inv_n, yc)
        for t, yi, vr in zip(ts, yc, var):
            out = yi * lax.rsqrt(vr + GN_EPS) * pv_ref[3:4, sls[t]] + pv_ref[4:5, sls[t]] + pro[t]["bonus"]
            y_ref[bi, :, sls[t]] = (out * g_ref[bi, :, sls[t]]).astype(BF16)

    pro = prologue(0, tiles)
    y_prev = pro_prev = None
    for bi in range(n_rows):
        x, p = solve_start(pro)
        pro_next = prologue(bi + 1, tiles) if bi + 1 < n_rows else None
        for i in range(steps):
            x, p = solve_stage(x, p, i == steps - 1)
            if bi >= 1:
                epilogue(bi - 1, parts[i], pro_prev, y_prev)
        y_prev, pro_prev = finish(bi, pro, x), pro
        pro = pro_next
    epilogue(n_rows - 1, tiles, pro_prev, y_prev)


def _wkv(r, lw, k, v, a, g, pvec, passes):
    b, s, d = r.shape
    c = WKV_CHUNK
    rows = WKV_ROWS if b % WKV_ROWS == 0 else 1
    tok = pl.BlockSpec((rows, c, d), lambda i, j: (i, j, 0))
    par = pl.BlockSpec((SUBLANES, d), lambda i, j: (0, 0))
    tok_bytes = _nbytes((rows, c, d), F32)
    sq_bytes = _nbytes((2 * c, 2 * c), F32)
    n_tiles = d // LANES
    return pl.pallas_call(
        functools.partial(_wkv_kernel, passes=passes),
        grid=(b // rows, s // c),
        in_specs=[tok] * 6 + [par],
        out_specs=tok,
        out_shape=jax.ShapeDtypeStruct((b, s, d), BF16),
        scratch_shapes=[pltpu.VMEM((rows, n_tiles, LANES, LANES), F32)],
        compiler_params=pltpu.CompilerParams(
            dimension_semantics=("parallel", "arbitrary"),
            vmem_limit_bytes=_vmem_limit(8 * tok_bytes, rows * n_tiles * sq_bytes, rows * n_tiles * 48 * sq_bytes)),
        name="wkv",
    )(r, lw, k, v, a, g, pvec)


def _conv_in_kernel(h_ref, g_ref, win_ref, cw_ref, z_out, carry_ref):
    @pl.when(pl.program_id(1) == 0)
    def _():
        carry_ref[...] = jnp.zeros_like(carry_ref)

    ts, d = h_ref.shape
    u = _rms(h_ref[...], g_ref[...]).astype(BF16)
    z = _dot(u, win_ref[:, d:2 * d]) * _dot(u, win_ref[:, 2 * d:])
    b_gate = _dot(u, win_ref[:, :d])
    carry = carry_ref[...]
    z1 = _shift_rows(z, carry, 1)
    z2 = _shift_rows(z, carry, 2)
    carry_ref[...] = z[ts - SUBLANES:, :]
    zc = z2 * cw_ref[0:1, :] + z1 * cw_ref[1:2, :] + z * cw_ref[2:3, :]
    z_out[...] = (b_gate * zc).astype(BF16)


def _conv_in(h, g, w_in, cw8):
    b, s, d = h.shape
    ts = min(CONV_TS, s)
    tok = pl.BlockSpec((None, ts, d), lambda i, j: (i, j, 0))
    tok_bytes = _nbytes((ts, d), F32)
    block_bytes = 2 * tok_bytes + _nbytes(w_in.shape, BF16) + _nbytes(cw8.shape, F32)
    return pl.pallas_call(
        _conv_in_kernel,
        grid=(b, s // ts),
        in_specs=[tok,
                  pl.BlockSpec(g.shape, lambda i, j: (0, 0)),
                  pl.BlockSpec(w_in.shape, lambda i, j: (0, 0)),
                  pl.BlockSpec(cw8.shape, lambda i, j: (0, 0))],
        out_specs=tok,
        out_shape=jax.ShapeDtypeStruct((b, s, d), BF16),
        scratch_shapes=[pltpu.VMEM((SUBLANES, d), F32)],
        compiler_params=pltpu.CompilerParams(
            dimension_semantics=("parallel", "arbitrary"),
            vmem_limit_bytes=_vmem_limit(block_bytes, _nbytes((SUBLANES, d), F32), 10 * tok_bytes)),
        name="conv_in",
    )(h, g, w_in, cw8)


def _sb_qkv_kernel(h_ref, g_ref, w_ref, qg_ref, kg_ref, q_out, k_out, v_out):
    d = h_ref.shape[1]
    u = _rms(h_ref[...], g_ref[...]).astype(BF16)
    hsum = _head_sum_matrix(2 * LANES)
    inv_n = 1.0 / HEAD_DIM

    def head_norm(x, gain):
        cols = []
        for j in range(0, d, 2 * LANES):
            xs = x[:, j:j + 2 * LANES]
            ms = _dot_exact_rhs(xs * xs, hsum, QKV_SUM_PIECES) * inv_n
            cols.append(xs * lax.rsqrt(ms + RMS_EPS))
        return jnp.concatenate(cols, axis=1) * gain

    q = _dot(u, w_ref[:, :d])
    k = _dot(u, w_ref[:, d:2 * d])
    q_out[...] = head_norm(q, qg_ref[...]).astype(BF16)
    v = _dot(u, w_ref[:, 2 * d:])
    k_out[...] = head_norm(k, kg_ref[...]).astype(BF16)
    v_out[...] = v.astype(BF16)


def _sb_qkv(h2, g, w_qkv, qg, kg):
    t, d = h2.shape
    tm = min(QKV_TM, t)
    tok = pl.BlockSpec((tm, d), lambda i: (i, 0))
    vec = pl.BlockSpec((1, d), lambda i: (0, 0))
    tok_bytes = _nbytes((tm, d), F32)
    block_bytes = 4 * tok_bytes + _nbytes(w_qkv.shape, BF16) + 3 * _nbytes((1, d), F32)
    out = jax.ShapeDtypeStruct((t, d), BF16)
    return pl.pallas_call(
        _sb_qkv_kernel,
        grid=(t // tm,),
        in_specs=[tok, vec, pl.BlockSpec(w_qkv.shape, lambda i: (0, 0)), vec, vec],
        out_specs=[tok] * 3,
        out_shape=[out] * 3,
        compiler_params=pltpu.CompilerParams(
            dimension_semantics=("parallel",),
            vmem_limit_bytes=_vmem_limit(block_bytes, 0, 10 * tok_bytes)),
        name="sb_qkv",
    )(h2, g, w_qkv, qg, kg)


def _sb_attn_kernel(zmax_ref, q_ref, k_ref, v_ref, suffix_ref, o_ref, acc_ref, run_ref, za_ref, zb_ref, att_ref):
    tq = q_ref.shape[0]
    qb = pl.program_id(2)
    rows = lax.broadcasted_iota(jnp.int32, (2 * tq, LANES), 0)
    lanes = lax.broadcasted_iota(jnp.int32, (2 * tq, LANES), 1)
    own = (rows >> (tq.bit_length() - 1)) == (lanes >> HEAD_SHIFT)
    q = q_ref[...].astype(F32)
    qs = (jnp.where(own, jnp.concatenate([q, q], axis=0), 0.0) * SB_SCALE).astype(BF16)
    suffix = suffix_ref[...]

    groups = [slice(g * SB_ROWS, (g + 1) * SB_ROWS) for g in range(2 * tq // SB_ROWS)]

    def each(f, *cols):
        return [f(*args) for args in zip(*cols)]

    def key_tile(ref, j):
        return ref[pl.ds(pl.multiple_of(j * tq, tq), tq), :]

    def scores(j, z_out):
        ks = key_tile(k_ref, j)
        for g in groups:
            z_out[g, :] = _dot_nt(qs[g, :], ks)

    def step(j, z_in, z_out, diagonal):
        if not diagonal:
            vs = key_tile(v_ref, j + 1)
            pv = [_dot(att_ref[g, :], vs) for g in groups]
        sp = [_softplus(z_in[g, :]) for g in groups]
        if diagonal:
            t_loc = lax.broadcasted_iota(jnp.int32, (SB_ROWS, tq), 0)
            s_loc = lax.broadcasted_iota(jnp.int32, (SB_ROWS, tq), 1)
            causal = [s_loc < t_loc + (g.start & (tq - 1)) for g in groups]
            sp = each(lambda m, x: jnp.where(m, x, 0.0), causal, sp)
        rev_cum = each(lambda x: _dot_exact_rhs(x, suffix, SB_CUM_PIECES), sp)
        scores(jnp.maximum(j - 1, 0), z_out)
        for i, g in enumerate(groups):
            rc = rev_cum[i]
            total = jnp.broadcast_to(rc[:, 0:1], (SB_ROWS, LANES))
            if diagonal:
                att = jnp.where(causal[i], jnp.exp(z_in[g, :] - rc), 0.0)
                run_ref[g, :] = total
                acc_ref[g, :] = jnp.zeros((SB_ROWS, LANES), F32)
            else:
                run = run_ref[g, :]
                att = jnp.exp(z_in[g, :] - rc - jnp.concatenate([run] * (tq // LANES), axis=1))
                run_ref[g, :] = run + total
                acc_ref[g, :] += pv[i]
            att_ref[g, :] = att.astype(BF16)

    stop_at = zmax_ref[0] + SB_EXP_ZERO

    def run_min():
        return jnp.min(run_ref[...])

    scores(qb, za_ref)
    step(qb, za_ref, zb_ref, True)

    @pl.when(qb >= 1)
    def _():
        step(qb - 1, zb_ref, za_ref, False)

    def more(state):
        j, low = state
        return jnp.logical_and(j >= 1, low < stop_at)

    def pair(state):
        j, _ = state
        step(j, za_ref, zb_ref, False)
        step(j - 1, zb_ref, za_ref, False)
        return j - 2, run_min()

    j, low = lax.while_loop(more, pair, (jnp.maximum(qb - 2, -1), run_min()))
    last_tile = jnp.logical_and(j == 0, low < stop_at)

    @pl.when(last_tile)
    def _():
        step(0, za_ref, zb_ref, False)

    vs = key_tile(v_ref, jnp.where(last_tile, 0, j + 1))
    for g in groups:
        acc_ref[g, :] += _dot(att_ref[g, :], vs)
    lane = lax.broadcasted_iota(jnp.int32, (tq, LANES), 1)
    o_ref[...] = jnp.where(lane >= HEAD_DIM, acc_ref[tq:, :], acc_ref[:tq, :]).astype(BF16)


def _sb_attn(q, k, v, z_max):
    b, s, d = q.shape
    tq = SB_TQ
    ri = lax.broadcasted_iota(jnp.int32, (tq, tq), 0)
    ci = lax.broadcasted_iota(jnp.int32, (tq, tq), 1)
    suffix = jnp.where(ri >= ci, 1.0, 0.0).astype(BF16)
    qspec = pl.BlockSpec((None, tq, LANES), lambda i, p, j: (i, j, p))
    kspec = pl.BlockSpec((None, s, LANES), lambda i, p, j: (i, 0, p))
    block_bytes = 2 * _nbytes((tq, LANES), BF16) + 2 * _nbytes((s, LANES), BF16) + _nbytes((tq, tq), BF16)
    scratch = [pltpu.VMEM((2 * tq, LANES), F32), pltpu.VMEM((2 * tq, LANES), F32),
               pltpu.VMEM((2 * tq, tq), F32), pltpu.VMEM((2 * tq, tq), F32), pltpu.VMEM((2 * tq, tq), BF16)]
    scratch_bytes = _nbytes((2 * tq, LANES), F32) + 3 * _nbytes((2 * tq, tq), F32) + _nbytes((2 * tq, tq), BF16)
    return pl.pallas_call(
        _sb_attn_kernel,
        grid=(b, d // LANES, s // tq),
        in_specs=[pl.BlockSpec(memory_space=pltpu.SMEM), qspec, kspec, kspec,
                  pl.BlockSpec((tq, tq), lambda i, p, j: (0, 0))],
        out_specs=qspec,
        out_shape=jax.ShapeDtypeStruct((b, s, d), BF16),
        scratch_shapes=scratch,
        compiler_params=pltpu.CompilerParams(
            dimension_semantics=("parallel", "parallel", "arbitrary"),
            vmem_limit_bytes=_vmem_limit(block_bytes, scratch_bytes, 12 * _nbytes((2 * tq, tq), F32))),
        name="sb_attn",
    )(z_max.reshape(1).astype(F32), q, k, v, suffix)


def _outproj_mlp_kernel(h_ref, z_ref, wo_ref, g_ref, wup_ref, wdn_ref, o_ref, xn_ref):
    @pl.when(pl.program_id(1) == 0)
    def _():
        h1 = h_ref[...] + _dot(z_ref[...], wo_ref[...])
        o_ref[...] = h1
        xn_ref[...] = _rms(h1, g_ref[...]).astype(BF16)

    act = jnp.square(jnp.maximum(_dot(xn_ref[...], wup_ref[...]), 0.0)).astype(BF16)
    o_ref[...] += _dot(act, wdn_ref[...])


def _outproj_mlp(h2, z, w_o, g, w_up, w_dn):
    t, d = h2.shape
    f = w_up.shape[1]
    tm = min(MLP_TM, t)
    tf = min(MLP_TF, f)
    tok = pl.BlockSpec((tm, d), lambda i, j: (i, 0))
    in_specs = [
        tok, tok,
        pl.BlockSpec((d, d), lambda i, j: (0, 0)),
        pl.BlockSpec((1, d), lambda i, j: (0, 0)),
        pl.BlockSpec((d, tf), lambda i, j: (0, j)),
        pl.BlockSpec((tf, d), lambda i, j: (j, 0)),
    ]
    tok_bytes = _nbytes((tm, d), F32)
    block_bytes = 2 * tok_bytes + _nbytes((tm, d), BF16) + _nbytes((d, d), BF16) + 2 * _nbytes((d, tf), BF16)
    return pl.pallas_call(
        _outproj_mlp_kernel,
        grid=(t // tm, f // tf),
        in_specs=in_specs,
        out_specs=tok,
        out_shape=jax.ShapeDtypeStruct((t, d), F32),
        scratch_shapes=[pltpu.VMEM((tm, d), BF16)],
        compiler_params=pltpu.CompilerParams(
            dimension_semantics=("parallel", "arbitrary"),
            vmem_limit_bytes=_vmem_limit(block_bytes, _nbytes((tm, d), BF16),
                                         tok_bytes + 2 * _nbytes((tm, tf), F32))),
        name="outproj_mlp",
    )(h2, z, w_o, g, w_up, w_dn)


def _pad_cols(w, mult):
    pad = (-w.shape[1]) % mult
    return jnp.pad(w, ((0, 0), (0, pad)))


def _pad_rows(w, mult):
    pad = (-w.shape[0]) % mult
    return jnp.pad(w, ((0, pad), (0, 0)))


def _rows8(*rows):
    d = rows[0].shape[-1]
    out = jnp.zeros((SUBLANES, d), F32)
    return out.at[:len(rows)].set(jnp.stack([r.reshape(d) for r in rows]))


def _lora(w_in, w_out):
    return _pad_cols(w_in, LANES).astype(BF16), _pad_rows(w_out, LANES).astype(BF16)


def kernel(x, mix_norm, mlp_norm, mlp_up, mlp_down, rwkv_mu, rwkv_w_r, rwkv_w_k, rwkv_w_v, rwkv_w_o, rwkv_decay_w0, rwkv_decay_w1, rwkv_decay_w2, rwkv_iclr_a0, rwkv_iclr_a1, rwkv_iclr_a2, rwkv_gate_g1, rwkv_gate_g2, rwkv_k_k, rwkv_k_a, rwkv_r_k, rwkv_lnx_w, rwkv_lnx_b, rwkv_vres_v0, rwkv_vres_v1, rwkv_vres_v2, conv_w_in, conv_w, conv_w_out, sb_w_qkv, sb_q_norm, sb_k_norm, sb_w_o):
    b, s, d = x.shape
    depth = mix_norm.shape[0]
    n_heads = d // HEAD_DIM
    h = x
    v_first = None
    for i in range(depth):
        kind = i % 3
        j = i // 3
        g_mix = mix_norm[i].reshape(1, d)
        if kind == 0:
            vres = None
            v0 = jnp.zeros((d,), F32)
            if j > 0:
                v1, v2 = _lora(rwkv_vres_v1[j - 1], rwkv_vres_v2[j - 1])
                vres = (v1, v2, v_first)
                v0 = rwkv_vres_v0[j - 1]
            vecs = _rows8(mix_norm[i], rwkv_decay_w0[j], rwkv_iclr_a0[j], v0)
            mu8 = _rows8(*[rwkv_mu[j, m] for m in range(rwkv_mu.shape[1])])
            w1, w2 = _lora(rwkv_decay_w1[j], rwkv_decay_w2[j])
            a1, a2 = _lora(rwkv_iclr_a1[j], rwkv_iclr_a2[j])
            g1, g2 = _lora(rwkv_gate_g1[j], rwkv_gate_g2[j])
            r, k, v, lw, a, gate = _rwkv_in(
                h, vecs, mu8, rwkv_w_r[j].astype(BF16), rwkv_w_k[j].astype(BF16),
                rwkv_w_v[j].astype(BF16), w1, w2, a1, a2, g1, g2, vres)
            if j == 0:
                v_first = v
            pvec = _rows8(rwkv_k_k[j], rwkv_k_a[j], rwkv_r_k[j].reshape(d), rwkv_lnx_w[j], rwkv_lnx_b[j])
            z = _wkv(r, lw, k, v, a, gate, pvec, WKV_PASSES)
            w_o = rwkv_w_o[j]
        elif kind == 1:
            z = _conv_in(h, g_mix, conv_w_in[j].astype(BF16), _rows8(*[conv_w[j, m] for m in range(3)]))
            w_o = conv_w_out[j]
        else:
            qg = jnp.tile(sb_q_norm[j], n_heads).reshape(1, d)
            kg = jnp.tile(sb_k_norm[j], n_heads).reshape(1, d)
            q, k, v = _sb_qkv(h.reshape(b * s, d), g_mix, sb_w_qkv[j].astype(BF16), qg, kg)
            z_max = (SB_ZMAX_SLACK * HEAD_DIM * SB_SCALE
                     * jnp.max(jnp.abs(sb_q_norm[j])) * jnp.max(jnp.abs(sb_k_norm[j])))
            z = _sb_attn(q.reshape(b, s, d), k.reshape(b, s, d), v.reshape(b, s, d), z_max)
            w_o = sb_w_o[j]
        h = _outproj_mlp(
            h.reshape(b * s, d), z.reshape(b * s, d), w_o.astype(BF16), mlp_norm[i].reshape(1, d), mlp_up[i].astype(BF16), mlp_down[i].astype(BF16),
        ).reshape(b, s, d)
    return h
```

```python
import functools

import jax
import jax.numpy as jnp
from jax import lax
from jax.experimental import pallas as pl
from jax.experimental.pallas import tpu as pltpu

F32 = jnp.float32
BF16 = jnp.bfloat16

HEAD_DIM = 64
HEAD_SHIFT = 6
LANES = 128
SUBLANES = 8
RMS_EPS = 1e-6
GN_EPS = 64e-5
KK_EPS = 1e-24
SB_SCALE = HEAD_DIM ** -0.5
SB_CUM_PIECES = 1
SB_EXP_ZERO = 105.0
SB_ZMAX_SLACK = 1.02

V7X_VMEM_BYTES = 64 * 1024 * 1024
VMEM_LIMIT_CAP = V7X_VMEM_BYTES - 8 * 1024 * 1024

MLP_TM = 1024
MLP_TF = 1024
RWKV_TS = 256
CONV_TS = 512
QKV_TM = 512
QKV_SUM_PIECES = 1
WKV_CHUNK = 64
WKV_ROWS = 4
WKV_GROUP_ROWS = 2
WKV_PASSES = 1
WKV_CUM_PIECES = 2
WKV_SUM_PIECES = 1
SB_TQ = 256
SB_ROWS = 128


def _vmem_limit(block_bytes, scratch_bytes, temp_bytes):
    return int(min(2 * block_bytes + scratch_bytes + temp_bytes, VMEM_LIMIT_CAP))


def _nbytes(shape, dtype):
    n = 1
    for s in shape:
        n *= s
    return n * jnp.dtype(dtype).itemsize


def _dot(a, b):
    return jnp.dot(a, b, preferred_element_type=F32)


def _dot_nt(a, b):
    return lax.dot_general(a, b, (((1,), (1,)), ((), ())), preferred_element_type=F32)


def _dot_tn(a, b):
    return lax.dot_general(a, b, (((0,), (0,)), ((), ())), preferred_element_type=F32)


def _split(x, pieces):
    out = []
    for _ in range(pieces - 1):
        hi = x.astype(BF16)
        out.append(hi)
        x = x - hi.astype(F32)
    out.append(x.astype(BF16))
    return out


def _dot_exact_rhs(x, m_bf16, pieces, dot=_dot):
    acc = None
    for p in _split(x, pieces):
        t = dot(p, m_bf16)
        acc = t if acc is None else acc + t
    return acc


def _dot_exact_lhs(m_bf16, x, pieces):
    acc = None
    for p in _split(x, pieces):
        t = _dot(m_bf16, p)
        acc = t if acc is None else acc + t
    return acc


def _mm(a, b, passes, dot=_dot):
    if passes == 1:
        return dot(a.astype(BF16), b.astype(BF16))
    ah, al = _split(a, 2)
    bh, bl = _split(b, 2)
    return dot(ah, bh) + (dot(al, bh) + dot(ah, bl))


def _rms(x, g):
    ms = jnp.mean(x * x, axis=-1, keepdims=True)
    return x * lax.rsqrt(ms + RMS_EPS) * g


def _sigmoid(x):
    return 1.0 / (1.0 + jnp.exp(-x))


def _softplus(x):
    return jnp.maximum(x, 0.0) + jnp.log(1.0 + jnp.exp(-jnp.abs(x)))


def _head_sum_matrix(n):
    r = lax.broadcasted_iota(jnp.int32, (n, n), 0) >> HEAD_SHIFT
    c = lax.broadcasted_iota(jnp.int32, (n, n), 1) >> HEAD_SHIFT
    return jnp.where(r == c, 1.0, 0.0).astype(BF16)


def _shift_rows(x, carry8, n):
    rows = lax.broadcasted_iota(jnp.int32, (x.shape[0], 1), 0)
    out = pltpu.roll(x, n, axis=0)
    for i in range(n):
        out = jnp.where(rows == i, carry8[SUBLANES - n + i:SUBLANES - n + i + 1, :], out)
    return out


def _rwkv_in_kernel(*refs, has_vres):
    if has_vres:
        (h_ref, vec_ref, mu_ref, wr_ref, wk_ref, wv_ref, w1_ref, w2_ref, a1_ref, a2_ref,
         g1_ref, g2_ref, v1_ref, v2_ref, vf_ref,
         r_out, k_out, v_out, lw_out, a_out, g_out, carry_ref) = refs
    else:
        (h_ref, vec_ref, mu_ref, wr_ref, wk_ref, wv_ref, w1_ref, w2_ref, a1_ref, a2_ref,
         g1_ref, g2_ref,
         r_out, k_out, v_out, lw_out, a_out, g_out, carry_ref) = refs

    @pl.when(pl.program_id(1) == 0)
    def _():
        carry_ref[...] = jnp.zeros_like(carry_ref)

    ts = h_ref.shape[0]
    u = _rms(h_ref[...], vec_ref[0:1, :])
    prev = _shift_rows(u, carry_ref[...], 1)
    carry_ref[...] = u[ts - SUBLANES:, :]
    xx = prev - u

    def mix(i):
        return (u + xx * mu_ref[i:i + 1, :]).astype(BF16)

    hw = jnp.tanh(_dot(mix(1), w1_ref[...])).astype(BF16)
    ha = _dot(mix(4), a1_ref[...]).astype(BF16)
    hg = _sigmoid(_dot(mix(5), g1_ref[...])).astype(BF16)
    xv = mix(3)
    if has_vres:
        hv = _dot(xv, v1_ref[...]).astype(BF16)
    dw = vec_ref[1:2, :] + _dot(hw, w2_ref[...])
    da = vec_ref[2:3, :] + _dot(ha, a2_ref[...])
    g_out[...] = _dot(hg, g2_ref[...])
    if has_vres:
        dv = vec_ref[3:4, :] + _dot(hv, v2_ref[...])

    r_out[...] = _dot(mix(0), wr_ref[...])
    lw_out[...] = -jnp.exp(-_softplus(-dw) - 0.5)
    k_out[...] = _dot(mix(2), wk_ref[...])
    a_out[...] = _sigmoid(da)
    v = _dot(xv, wv_ref[...])
    if has_vres:
        v = v + (vf_ref[...] - v) * _sigmoid(dv)
    v_out[...] = v


def _rwkv_in(h, vecs, mu8, wr, wk, wv, w1, w2, a1, a2, g1, g2, vres):
    b, s, d = h.shape
    ts = min(RWKV_TS, s)
    tok = pl.BlockSpec((None, ts, d), lambda i, j: (i, j, 0))

    def full(x):
        return pl.BlockSpec(x.shape, lambda i, j: (0,) * x.ndim)

    weights = [vecs, mu8, wr, wk, wv, w1, w2, a1, a2, g1, g2]
    args = [h] + weights
    in_specs = [tok] + [full(w) for w in weights]
    if vres is not None:
        v1, v2, v_first = vres
        args += [v1, v2, v_first]
        in_specs += [full(v1), full(v2), tok]
    tok_bytes = _nbytes((ts, d), F32)
    block_bytes = sum(_nbytes(w.shape, w.dtype) for w in args[1:1 + len(weights) + (2 if vres else 0)])
    block_bytes += tok_bytes * (7 + (1 if vres else 0))
    out = jax.ShapeDtypeStruct((b, s, d), F32)
    return pl.pallas_call(
        functools.partial(_rwkv_in_kernel, has_vres=vres is not None),
        grid=(b, s // ts),
        in_specs=in_specs,
        out_specs=[tok] * 6,
        out_shape=[out] * 6,
        scratch_shapes=[pltpu.VMEM((SUBLANES, d), F32)],
        compiler_params=pltpu.CompilerParams(
            dimension_semantics=("parallel", "arbitrary"),
            vmem_limit_bytes=_vmem_limit(block_bytes, _nbytes((SUBLANES, d), F32), 12 * tok_bytes)),
        name="rwkv_in",
    )(*args)


def _wkv_kernel(r_ref, lw_ref, k_ref, v_ref, a_ref, g_ref, pv_ref, y_ref, s_ref, *, passes):
    n_rows, c, d_model = r_ref.shape
    n = 2 * c
    n_tiles = d_model // LANES

    @pl.when(pl.program_id(1) == 0)
    def _():
        s_ref[...] = jnp.zeros_like(s_ref)

    first_head = lax.broadcasted_iota(jnp.int32, (c, LANES), 1) < HEAD_DIM

    def head_sums(x):
        lo = jnp.sum(jnp.where(first_head, x, 0.0), axis=-1, keepdims=True)
        hi = jnp.sum(jnp.where(first_head, 0.0, x), axis=-1, keepdims=True)
        return jnp.where(first_head, lo, hi)

    ti = lax.broadcasted_iota(jnp.int32, (c, c), 0)
    si = lax.broadcasted_iota(jnp.int32, (c, c), 1)
    lincl = jnp.where(si <= ti, 1.0, 0.0).astype(BF16)
    log2c = c.bit_length() - 1
    row = lax.broadcasted_iota(jnp.int32, (n, LANES), 0)
    lane = lax.broadcasted_iota(jnp.int32, (n, LANES), 1)
    own = (row >> log2c) == (lane >> HEAD_SHIFT)
    ri = lax.broadcasted_iota(jnp.int32, (n, 2 * n), 0)
    ci = lax.broadcasted_iota(jnp.int32, (n, 2 * n), 1) & (c - 1)
    tri = ci < (ri & (c - 1)) + (ri >> log2c)
    inv_n = 1.0 / HEAD_DIM
    mm = functools.partial(_mm, passes=passes)
    mm_nt = functools.partial(_mm, passes=passes, dot=_dot_nt)
    mm_tn = functools.partial(_mm, passes=passes, dot=_dot_tn)

    def stack(x):
        return jnp.where(own, jnp.concatenate([x, x], axis=0), 0.0)

    steps = log2c
    rows_per_group = WKV_GROUP_ROWS if n_rows % WKV_GROUP_ROWS == 0 else 1
    groups = [[(bi, slice(t * LANES, (t + 1) * LANES), t)
               for bi in range(g0, g0 + rows_per_group) for t in range(n_tiles)]
              for g0 in range(0, n_rows, rows_per_group)]

    def each(f, *cols):
        return [f(*args) for args in zip(*cols)]

    def pro_keys(units):
        st = {"units": units}
        for name, ref in (("r", r_ref), ("lw", lw_ref), ("k_raw", k_ref), ("v", v_ref), ("a", a_ref)):
            st[name] = [ref[bi, :, sl] for bi, sl, _ in units]

        def param(i):
            return [pv_ref[i:i + 1, sl] for _, sl, _ in units]

        kk = each(lambda x, g: x * g, st["k_raw"], param(0))
        kk_ss = each(lambda x: head_sums(x * x), kk)
        st["kk"] = each(lambda x, ss: x * lax.rsqrt(jnp.maximum(ss, KK_EPS)), kk, kk_ss)
        st["k"] = each(lambda x, ai, g: x * (1.0 + (ai - 1.0) * g), st["k_raw"], st["a"], param(1))
        st["bonus"] = each(lambda ri_, ki, g, vi: head_sums(ri_ * ki * g) * vi, st["r"], st["k"], param(2), st["v"])
        return st

    def pro_cum(st):
        st["cum"] = each(lambda x: _dot_exact_lhs(lincl, x, WKV_CUM_PIECES), st["lw"])
        return st

    def pro_scale(st):
        g_incl = each(jnp.exp, st["cum"])
        g_excl = each(lambda cs, x: jnp.exp(cs - x), st["cum"], st["lw"])
        g_inv = each(lambda cs: jnp.exp(-cs), st["cum"])
        st["g_end"] = each(lambda g: g[c - 1:c, :], g_incl)
        st["ar"] = each(lambda kki, ri_, ge, gi: jnp.concatenate([-kki * ge, ri_ * gi], axis=0),
                        st["kk"], st["r"], g_excl, g_incl)
        st["bt"] = each(lambda kki, ai, gv: kki * ai * gv, st["kk"], st["a"], g_inv)
        st["kt"] = each(lambda ki, gv: ki * gv, st["k"], g_inv)
        return st

    def pro_stack(st):
        st["bks"] = each(lambda b_, k_: jnp.concatenate([stack(b_), stack(k_)], axis=0), st["bt"], st["kt"])
        st["vs"] = each(stack, st["v"])
        return st

    def pro_products(st):
        st["s0"] = [s_ref[bi, t] for bi, _, t in st["units"]]
        st["prod"] = each(lambda x, y: jnp.where(tri, mm_nt(x, y), 0.0), st["ar"], st["bks"])
        st["from_s0"] = each(mm_nt, st["ar"], st["s0"])
        return st

    pro_pieces = [pro_keys, pro_cum, pro_scale, pro_stack, pro_products]
    assert len(pro_pieces) <= steps

    def solve_start(st):
        x = each(lambda fs, pr, vs: fs[:c, :] + mm(pr[:c, n:], vs), st["from_s0"], st["prod"], st["vs"])
        return x, [pr[:c, :n] for pr in st["prod"]]

    def solve_stage(x, p, last):
        if last:
            return each(lambda xi, pi: xi + mm(pi, stack(xi)), x, p), None
        px = each(lambda xi, pi: mm(pi, jnp.concatenate([stack(xi), stack(pi)], axis=1)), x, p)
        return each(lambda xi, pxi: xi + pxi[:, :LANES], x, px), each(lambda pxi: pxi[:, LANES:], px)

    def finish(st, x):
        st["y"] = each(lambda fs, pr, vs, u: fs[c:, :] + mm(pr[c:, :], jnp.concatenate([stack(u), vs], axis=0)),
                       st["from_s0"], st["prod"], st["vs"], x)
        outer = each(lambda u, v, bt, kt, ge: mm_tn(jnp.concatenate([u, v], axis=0),
                                                    jnp.concatenate([bt * ge, kt * ge], axis=0)),
                     x, st["v"], st["bt"], st["kt"], st["g_end"])
        for (bi, _, t), s0, ge, o in zip(st["units"], st["s0"], st["g_end"], outer):
            s_ref[bi, t] = s0 * ge + jnp.where(own, o, 0.0)

    def epilogue(st, lo, hi):
        y = st["y"][lo:hi]
        mean = each(lambda yi: head_sums(yi) * inv_n, y)
        yc = each(lambda yi, m: yi - m, y, mean)
        var = each(lambda yi: head_sums(yi * yi) * inv_n, yc)
        for (bi, sl, _), bonus, yi, vr in zip(st["units"][lo:hi], st["bonus"][lo:hi], yc, var):
            out = yi * lax.rsqrt(vr + GN_EPS) * pv_ref[3:4, sl] + pv_ref[4:5, sl] + bonus
            y_ref[bi, :, sl] = (out * g_ref[bi, :, sl]).astype(BF16)

    cur = groups[0]
    for piece in pro_pieces:
        cur = piece(cur)
    prev = None
    for gi in range(len(groups)):
        x, p = solve_start(cur)
        nxt = groups[gi + 1] if gi + 1 < len(groups) else None
        n_units = len(groups[gi])
        for i in range(steps):
            x, p = solve_stage(x, p, i == steps - 1)
            if nxt is not None and i < len(pro_pieces):
                nxt = pro_pieces[i](nxt)
            if prev is not None:
                epilogue(prev, i * n_units // steps, (i + 1) * n_units // steps)
        finish(cur, x)
        prev, cur = cur, nxt
    epilogue(prev, 0, len(prev["units"]))


def _wkv(r, lw, k, v, a, g, pvec, passes):
    b, s, d = r.shape
    c = WKV_CHUNK
    rows = WKV_ROWS if b % WKV_ROWS == 0 else 1
    tok = pl.BlockSpec((rows, c, d), lambda i, j: (i, j, 0))
    par = pl.BlockSpec((SUBLANES, d), lambda i, j: (0, 0))
    tok_bytes = _nbytes((rows, c, d), F32)
    sq_bytes = _nbytes((2 * c, 2 * c), F32)
    n_tiles = d // LANES
    return pl.pallas_call(
        functools.partial(_wkv_kernel, passes=passes),
        grid=(b // rows, s // c),
        in_specs=[tok] * 6 + [par],
        out_specs=tok,
        out_shape=jax.ShapeDtypeStruct((b, s, d), BF16),
        scratch_shapes=[pltpu.VMEM((rows, n_tiles, LANES, LANES), F32)],
        compiler_params=pltpu.CompilerParams(
            dimension_semantics=("parallel", "arbitrary"),
            vmem_limit_bytes=_vmem_limit(8 * tok_bytes, rows * n_tiles * sq_bytes, rows * n_tiles * 48 * sq_bytes)),
        name="wkv",
    )(r, lw, k, v, a, g, pvec)


def _conv_in_kernel(h_ref, g_ref, win_ref, cw_ref, z_out, carry_ref):
    @pl.when(pl.program_id(1) == 0)
    def _():
        carry_ref[...] = jnp.zeros_like(carry_ref)

    ts, d = h_ref.shape
    u = _rms(h_ref[...], g_ref[...]).astype(BF16)
    z = _dot(u, win_ref[:, d:2 * d]) * _dot(u, win_ref[:, 2 * d:])
    b_gate = _dot(u, win_ref[:, :d])
    carry = carry_ref[...]
    z1 = _shift_rows(z, carry, 1)
    z2 = _shift_rows(z, carry, 2)
    carry_ref[...] = z[ts - SUBLANES:, :]
    zc = z2 * cw_ref[0:1, :] + z1 * cw_ref[1:2, :] + z * cw_ref[2:3, :]
    z_out[...] = (b_gate * zc).astype(BF16)


def _conv_in(h, g, w_in, cw8):
    b, s, d = h.shape
    ts = min(CONV_TS, s)
    tok = pl.BlockSpec((None, ts, d), lambda i, j: (i, j, 0))
    tok_bytes = _nbytes((ts, d), F32)
    block_bytes = 2 * tok_bytes + _nbytes(w_in.shape, BF16) + _nbytes(cw8.shape, F32)
    return pl.pallas_call(
        _conv_in_kernel,
        grid=(b, s // ts),
        in_specs=[tok,
                  pl.BlockSpec(g.shape, lambda i, j: (0, 0)),
                  pl.BlockSpec(w_in.shape, lambda i, j: (0, 0)),
                  pl.BlockSpec(cw8.shape, lambda i, j: (0, 0))],
        out_specs=tok,
        out_shape=jax.ShapeDtypeStruct((b, s, d), BF16),
        scratch_shapes=[pltpu.VMEM((SUBLANES, d), F32)],
        compiler_params=pltpu.CompilerParams(
            dimension_semantics=("parallel", "arbitrary"),
            vmem_limit_bytes=_vmem_limit(block_bytes, _nbytes((SUBLANES, d), F32), 10 * tok_bytes)),
        name="conv_in",
    )(h, g, w_in, cw8)


def _sb_qkv_kernel(h_ref, g_ref, w_ref, qg_ref, kg_ref, q_out, k_out, v_out):
    d = h_ref.shape[1]
    u = _rms(h_ref[...], g_ref[...]).astype(BF16)
    hsum = _head_sum_matrix(2 * LANES)
    inv_n = 1.0 / HEAD_DIM

    def head_norm(x, gain):
        cols = []
        for j in range(0, d, 2 * LANES):
            xs = x[:, j:j + 2 * LANES]
            ms = _dot_exact_rhs(xs * xs, hsum, QKV_SUM_PIECES) * inv_n
            cols.append(xs * lax.rsqrt(ms + RMS_EPS))
        return jnp.concatenate(cols, axis=1) * gain

    q = _dot(u, w_ref[:, :d])
    k = _dot(u, w_ref[:, d:2 * d])
    q_out[...] = head_norm(q, qg_ref[...]).astype(BF16)
    v = _dot(u, w_ref[:, 2 * d:])
    k_out[...] = head_norm(k, kg_ref[...]).astype(BF16)
    v_out[...] = v.astype(BF16)


def _sb_qkv(h2, g, w_qkv, qg, kg):
    t, d = h2.shape
    tm = min(QKV_TM, t)
    tok = pl.BlockSpec((tm, d), lambda i: (i, 0))
    vec = pl.BlockSpec((1, d), lambda i: (0, 0))
    tok_bytes = _nbytes((tm, d), F32)
    block_bytes = 4 * tok_bytes + _nbytes(w_qkv.shape, BF16) + 3 * _nbytes((1, d), F32)
    out = jax.ShapeDtypeStruct((t, d), BF16)
    return pl.pallas_call(
        _sb_qkv_kernel,
        grid=(t // tm,),
        in_specs=[tok, vec, pl.BlockSpec(w_qkv.shape, lambda i: (0, 0)), vec, vec],
        out_specs=[tok] * 3,
        out_shape=[out] * 3,
        compiler_params=pltpu.CompilerParams(
            dimension_semantics=("parallel",),
            vmem_limit_bytes=_vmem_limit(block_bytes, 0, 10 * tok_bytes)),
        name="sb_qkv",
    )(h2, g, w_qkv, qg, kg)


def _sb_attn_kernel(zmax_ref, q_ref, k_ref, v_ref, suffix_ref, o_ref, acc_ref, run_ref, za_ref, zb_ref, att_ref):
    tq = q_ref.shape[0]
    qb = pl.program_id(2)
    rows = lax.broadcasted_iota(jnp.int32, (2 * tq, LANES), 0)
    lanes = lax.broadcasted_iota(jnp.int32, (2 * tq, LANES), 1)
    own = (rows >> (tq.bit_length() - 1)) == (lanes >> HEAD_SHIFT)
    q = q_ref[...].astype(F32)
    qs = (jnp.where(own, jnp.concatenate([q, q], axis=0), 0.0) * SB_SCALE).astype(BF16)
    suffix = suffix_ref[...]

    groups = [slice(g * SB_ROWS, (g + 1) * SB_ROWS) for g in range(2 * tq // SB_ROWS)]

    def each(f, *cols):
        return [f(*args) for args in zip(*cols)]

    def key_tile(ref, j):
        return ref[pl.ds(pl.multiple_of(j * tq, tq), tq), :]

    def scores(j, z_out):
        ks = key_tile(k_ref, j)
        for g in groups:
            z_out[g, :] = _dot_nt(qs[g, :], ks)

    def step(j, z_in, z_out, diagonal):
        if not diagonal:
            vs = key_tile(v_ref, j + 1)
            pv = [_dot(att_ref[g, :], vs) for g in groups]
        sp = [_softplus(z_in[g, :]) for g in groups]
        if diagonal:
            t_loc = lax.broadcasted_iota(jnp.int32, (SB_ROWS, tq), 0)
            s_loc = lax.broadcasted_iota(jnp.int32, (SB_ROWS, tq), 1)
            causal = [s_loc < t_loc + (g.start & (tq - 1)) for g in groups]
            sp = each(lambda m, x: jnp.where(m, x, 0.0), causal, sp)
        rev_cum = each(lambda x: _dot_exact_rhs(x, suffix, SB_CUM_PIECES), sp)
        scores(jnp.maximum(j - 1, 0), z_out)
        for i, g in enumerate(groups):
            rc = rev_cum[i]
            total = jnp.broadcast_to(rc[:, 0:1], (SB_ROWS, LANES))
            if diagonal:
                att = jnp.where(causal[i], jnp.exp(z_in[g, :] - rc), 0.0)
                run_ref[g, :] = total
                acc_ref[g, :] = jnp.zeros((SB_ROWS, LANES), F32)
            else:
                run = run_ref[g, :]
                att = jnp.exp(z_in[g, :] - rc - jnp.concatenate([run] * (tq // LANES), axis=1))
                run_ref[g, :] = run + total
                acc_ref[g, :] += pv[i]
            att_ref[g, :] = att.astype(BF16)

    stop_at = zmax_ref[0] + SB_EXP_ZERO

    def run_min():
        return jnp.min(run_ref[...])

    scores(qb, za_ref)
    step(qb, za_ref, zb_ref, True)

    @pl.when(qb >= 1)
    def _():
        step(qb - 1, zb_ref, za_ref, False)

    def more(state):
        j, low = state
        return jnp.logical_and(j >= 1, low < stop_at)

    def pair(state):
        j, _ = state
        step(j, za_ref, zb_ref, False)
        step(j - 1, zb_ref, za_ref, False)
        return j - 2, run_min()

    j, low = lax.while_loop(more, pair, (jnp.maximum(qb - 2, -1), run_min()))
    last_tile = jnp.logical_and(j == 0, low < stop_at)

    @pl.when(last_tile)
    def _():
        step(0, za_ref, zb_ref, False)

    vs = key_tile(v_ref, jnp.where(last_tile, 0, j + 1))
    for g in groups:
        acc_ref[g, :] += _dot(att_ref[g, :], vs)
    lane = lax.broadcasted_iota(jnp.int32, (tq, LANES), 1)
    o_ref[...] = jnp.where(lane >= HEAD_DIM, acc_ref[tq:, :], acc_ref[:tq, :]).astype(BF16)


def _sb_attn(q, k, v, z_max):
    b, s, d = q.shape
    tq = SB_TQ
    ri = lax.broadcasted_iota(jnp.int32, (tq, tq), 0)
    ci = lax.broadcasted_iota(jnp.int32, (tq, tq), 1)
    suffix = jnp.where(ri >= ci, 1.0, 0.0).astype(BF16)
    qspec = pl.BlockSpec((None, tq, LANES), lambda i, p, j: (i, j, p))
    kspec = pl.BlockSpec((None, s, LANES), lambda i, p, j: (i, 0, p))
    block_bytes = 2 * _nbytes((tq, LANES), BF16) + 2 * _nbytes((s, LANES), BF16) + _nbytes((tq, tq), BF16)
    scratch = [pltpu.VMEM((2 * tq, LANES), F32), pltpu.VMEM((2 * tq, LANES), F32),
               pltpu.VMEM((2 * tq, tq), F32), pltpu.VMEM((2 * tq, tq), F32), pltpu.VMEM((2 * tq, tq), BF16)]
    scratch_bytes = _nbytes((2 * tq, LANES), F32) + 3 * _nbytes((2 * tq, tq), F32) + _nbytes((2 * tq, tq), BF16)
    return pl.pallas_call(
        _sb_attn_kernel,
        grid=(b, d // LANES, s // tq),
        in_specs=[pl.BlockSpec(memory_space=pltpu.SMEM), qspec, kspec, kspec,
                  pl.BlockSpec((tq, tq), lambda i, p, j: (0, 0))],
        out_specs=qspec,
        out_shape=jax.ShapeDtypeStruct((b, s, d), BF16),
        scratch_shapes=scratch,
        compiler_params=pltpu.CompilerParams(
            dimension_semantics=("parallel", "parallel", "arbitrary"),
            vmem_limit_bytes=_vmem_limit(block_bytes, scratch_bytes, 12 * _nbytes((2 * tq, tq), F32))),
        name="sb_attn",
    )(z_max.reshape(1).astype(F32), q, k, v, suffix)


def _outproj_mlp_kernel(h_ref, z_ref, wo_ref, g_ref, wup_ref, wdn_ref, o_ref, xn_ref):
    @pl.when(pl.program_id(1) == 0)
    def _():
        h1 = h_ref[...] + _dot(z_ref[...], wo_ref[...])
        o_ref[...] = h1
        xn_ref[...] = _rms(h1, g_ref[...]).astype(BF16)

    act = jnp.square(jnp.maximum(_dot(xn_ref[...], wup_ref[...]), 0.0)).astype(BF16)
    o_ref[...] += _dot(act, wdn_ref[...])


def _outproj_mlp(h2, z, w_o, g, w_up, w_dn):
    t, d = h2.shape
    f = w_up.shape[1]
    tm = min(MLP_TM, t)
    tf = min(MLP_TF, f)
    tok = pl.BlockSpec((tm, d), lambda i, j: (i, 0))
    in_specs = [
        tok, tok,
        pl.BlockSpec((d, d), lambda i, j: (0, 0)),
        pl.BlockSpec((1, d), lambda i, j: (0, 0)),
        pl.BlockSpec((d, tf), lambda i, j: (0, j)),
        pl.BlockSpec((tf, d), lambda i, j: (j, 0)),
    ]
    tok_bytes = _nbytes((tm, d), F32)
    block_bytes = 2 * tok_bytes + _nbytes((tm, d), BF16) + _nbytes((d, d), BF16) + 2 * _nbytes((d, tf), BF16)
    return pl.pallas_call(
        _outproj_mlp_kernel,
        grid=(t // tm, f // tf),
        in_specs=in_specs,
        out_specs=tok,
        out_shape=jax.ShapeDtypeStruct((t, d), F32),
        scratch_shapes=[pltpu.VMEM((tm, d), BF16)],
        compiler_params=pltpu.CompilerParams(
            dimension_semantics=("parallel", "arbitrary"),
            vmem_limit_bytes=_vmem_limit(block_bytes, _nbytes((tm, d), BF16),
                                         tok_bytes + 2 * _nbytes((tm, tf), F32))),
        name="outproj_mlp",
    )(h2, z, w_o, g, w_up, w_dn)


def _pad_cols(w, mult):
    pad = (-w.shape[1]) % mult
    return jnp.pad(w, ((0, 0), (0, pad)))


def _pad_rows(w, mult):
    pad = (-w.shape[0]) % mult
    return jnp.pad(w, ((0, pad), (0, 0)))


def _rows8(*rows):
    d = rows[0].shape[-1]
    out = jnp.zeros((SUBLANES, d), F32)
    return out.at[:len(rows)].set(jnp.stack([r.reshape(d) for r in rows]))


def _lora(w_in, w_out):
    return _pad_cols(w_in, LANES).astype(BF16), _pad_rows(w_out, LANES).astype(BF16)


def kernel(x, mix_norm, mlp_norm, mlp_up, mlp_down, rwkv_mu, rwkv_w_r, rwkv_w_k, rwkv_w_v, rwkv_w_o, rwkv_decay_w0, rwkv_decay_w1, rwkv_decay_w2, rwkv_iclr_a0, rwkv_iclr_a1, rwkv_iclr_a2, rwkv_gate_g1, rwkv_gate_g2, rwkv_k_k, rwkv_k_a, rwkv_r_k, rwkv_lnx_w, rwkv_lnx_b, rwkv_vres_v0, rwkv_vres_v1, rwkv_vres_v2, conv_w_in, conv_w, conv_w_out, sb_w_qkv, sb_q_norm, sb_k_norm, sb_w_o):
    b, s, d = x.shape
    depth = mix_norm.shape[0]
    n_heads = d // HEAD_DIM
    h = x
    v_first = None
    for i in range(depth):
        kind = i % 3
        j = i // 3
        g_mix = mix_norm[i].reshape(1, d)
        if kind == 0:
            vres = None
            v0 = jnp.zeros((d,), F32)
            if j > 0:
                v1, v2 = _lora(rwkv_vres_v1[j - 1], rwkv_vres_v2[j - 1])
                vres = (v1, v2, v_first)
                v0 = rwkv_vres_v0[j - 1]
            vecs = _rows8(mix_norm[i], rwkv_decay_w0[j], rwkv_iclr_a0[j], v0)
            mu8 = _rows8(*[rwkv_mu[j, m] for m in range(rwkv_mu.shape[1])])
            w1, w2 = _lora(rwkv_decay_w1[j], rwkv_decay_w2[j])
            a1, a2 = _lora(rwkv_iclr_a1[j], rwkv_iclr_a2[j])
            g1, g2 = _lora(rwkv_gate_g1[j], rwkv_gate_g2[j])
            r, k, v, lw, a, gate = _rwkv_in(
                h, vecs, mu8, rwkv_w_r[j].astype(BF16), rwkv_w_k[j].astype(BF16),
                rwkv_w_v[j].astype(BF16), w1, w2, a1, a2, g1, g2, vres)
            if j == 0:
                v_first = v
            pvec = _rows8(rwkv_k_k[j], rwkv_k_a[j], rwkv_r_k[j].reshape(d), rwkv_lnx_w[j], rwkv_lnx_b[j])
            z = _wkv(r, lw, k, v, a, gate, pvec, WKV_PASSES)
            w_o = rwkv_w_o[j]
        elif kind == 1:
            z = _conv_in(h, g_mix, conv_w_in[j].astype(BF16), _rows8(*[conv_w[j, m] for m in range(3)]))
            w_o = conv_w_out[j]
        else:
            qg = jnp.tile(sb_q_norm[j], n_heads).reshape(1, d)
            kg = jnp.tile(sb_k_norm[j], n_heads).reshape(1, d)
            q, k, v = _sb_qkv(h.reshape(b * s, d), g_mix, sb_w_qkv[j].astype(BF16), qg, kg)
            z_max = (SB_ZMAX_SLACK * HEAD_DIM * SB_SCALE
                     * jnp.max(jnp.abs(sb_q_norm[j])) * jnp.max(jnp.abs(sb_k_norm[j])))
            z = _sb_attn(q.reshape(b, s, d), k.reshape(b, s, d), v.reshape(b, s, d), z_max)
            w_o = sb_w_o[j]
        h = _outproj_mlp(
            h.reshape(b * s, d), z.reshape(b * s, d), w_o.astype(BF16), mlp_norm[i].reshape(1, d), mlp_up[i].astype(BF16), mlp_down[i].astype(BF16),
        ).reshape(b, s, d)
    return h
```

```python
import functools

import jax
import jax.numpy as jnp
from jax import lax
from jax.experimental import pallas as pl
from jax.experimental.pallas import tpu as pltpu

F32 = jnp.float32
BF16 = jnp.bfloat16

HEAD_DIM = 64
HEAD_SHIFT = 6
LANES = 128
SUBLANES = 8
RMS_EPS = 1e-6
GN_EPS = 64e-5
KK_EPS = 1e-24
SB_SCALE = HEAD_DIM ** -0.5
SB_CUM_PIECES = 1
SB_EXP_ZERO = 105.0
SB_ZMAX_SLACK = 1.02

V7X_VMEM_BYTES = 64 * 1024 * 1024
VMEM_LIMIT_CAP = V7X_VMEM_BYTES - 8 * 1024 * 1024

MLP_TM = 1024
MLP_TF = 1024
RWKV_TS = 256
CONV_TS = 512
QKV_TM = 512
QKV_SUM_PIECES = 1
WKV_CHUNK = 64
WKV_ROWS = 4
WKV_GROUP_ROWS = 2
WKV_PASSES = 1
WKV_CUM_PIECES = 2
WKV_SUM_PIECES = 1
SB_TQ = 256
SB_ROWS = 256
SB_LANE_TILES = 8


def _vmem_limit(block_bytes, scratch_bytes, temp_bytes):
    return int(min(2 * block_bytes + scratch_bytes + temp_bytes, VMEM_LIMIT_CAP))


def _nbytes(shape, dtype):
    n = 1
    for s in shape:
        n *= s
    return n * jnp.dtype(dtype).itemsize


def _dot(a, b):
    return jnp.dot(a, b, preferred_element_type=F32)


def _dot_nt(a, b):
    return lax.dot_general(a, b, (((1,), (1,)), ((), ())), preferred_element_type=F32)


def _dot_tn(a, b):
    return lax.dot_general(a, b, (((0,), (0,)), ((), ())), preferred_element_type=F32)


def _split(x, pieces):
    out = []
    for _ in range(pieces - 1):
        hi = x.astype(BF16)
        out.append(hi)
        x = x - hi.astype(F32)
    out.append(x.astype(BF16))
    return out


def _dot_exact_rhs(x, m_bf16, pieces, dot=_dot):
    acc = None
    for p in _split(x, pieces):
        t = dot(p, m_bf16)
        acc = t if acc is None else acc + t
    return acc


def _dot_exact_lhs(m_bf16, x, pieces):
    acc = None
    for p in _split(x, pieces):
        t = _dot(m_bf16, p)
        acc = t if acc is None else acc + t
    return acc


def _mm(a, b, passes, dot=_dot):
    if passes == 1:
        return dot(a.astype(BF16), b.astype(BF16))
    ah, al = _split(a, 2)
    bh, bl = _split(b, 2)
    return dot(ah, bh) + (dot(al, bh) + dot(ah, bl))


def _rms(x, g):
    ms = jnp.mean(x * x, axis=-1, keepdims=True)
    return x * lax.rsqrt(ms + RMS_EPS) * g


def _sigmoid(x):
    return 1.0 / (1.0 + jnp.exp(-x))


def _softplus(x):
    return jnp.maximum(x, 0.0) + jnp.log(1.0 + jnp.exp(-jnp.abs(x)))


def _head_sum_matrix(n):
    r = lax.broadcasted_iota(jnp.int32, (n, n), 0) >> HEAD_SHIFT
    c = lax.broadcasted_iota(jnp.int32, (n, n), 1) >> HEAD_SHIFT
    return jnp.where(r == c, 1.0, 0.0).astype(BF16)


def _shift_rows(x, carry8, n):
    rows = lax.broadcasted_iota(jnp.int32, (x.shape[0], 1), 0)
    out = pltpu.roll(x, n, axis=0)
    for i in range(n):
        out = jnp.where(rows == i, carry8[SUBLANES - n + i:SUBLANES - n + i + 1, :], out)
    return out


def _rwkv_in_kernel(*refs, has_vres):
    if has_vres:
        (h_ref, vec_ref, mu_ref, wr_ref, wk_ref, wv_ref, w1_ref, w2_ref, a1_ref, a2_ref,
         g1_ref, g2_ref, v1_ref, v2_ref, vf_ref,
         r_out, k_out, v_out, lw_out, a_out, g_out, carry_ref) = refs
    else:
        (h_ref, vec_ref, mu_ref, wr_ref, wk_ref, wv_ref, w1_ref, w2_ref, a1_ref, a2_ref,
         g1_ref, g2_ref,
         r_out, k_out, v_out, lw_out, a_out, g_out, carry_ref) = refs

    @pl.when(pl.program_id(1) == 0)
    def _():
        carry_ref[...] = jnp.zeros_like(carry_ref)

    ts = h_ref.shape[0]
    u = _rms(h_ref[...], vec_ref[0:1, :])
    prev = _shift_rows(u, carry_ref[...], 1)
    carry_ref[...] = u[ts - SUBLANES:, :]
    xx = prev - u

    def mix(i):
        return (u + xx * mu_ref[i:i + 1, :]).astype(BF16)

    hw = jnp.tanh(_dot(mix(1), w1_ref[...])).astype(BF16)
    ha = _dot(mix(4), a1_ref[...]).astype(BF16)
    hg = _sigmoid(_dot(mix(5), g1_ref[...])).astype(BF16)
    xv = mix(3)
    if has_vres:
        hv = _dot(xv, v1_ref[...]).astype(BF16)
    dw = vec_ref[1:2, :] + _dot(hw, w2_ref[...])
    da = vec_ref[2:3, :] + _dot(ha, a2_ref[...])
    g_out[...] = _dot(hg, g2_ref[...])
    if has_vres:
        dv = vec_ref[3:4, :] + _dot(hv, v2_ref[...])

    r_out[...] = _dot(mix(0), wr_ref[...])
    lw_out[...] = -jnp.exp(-_softplus(-dw) - 0.5)
    k_out[...] = _dot(mix(2), wk_ref[...])
    a_out[...] = _sigmoid(da)
    v = _dot(xv, wv_ref[...])
    if has_vres:
        v = v + (vf_ref[...] - v) * _sigmoid(dv)
    v_out[...] = v


def _rwkv_in(h, vecs, mu8, wr, wk, wv, w1, w2, a1, a2, g1, g2, vres):
    b, s, d = h.shape
    ts = min(RWKV_TS, s)
    tok = pl.BlockSpec((None, ts, d), lambda i, j: (i, j, 0))

    def full(x):
        return pl.BlockSpec(x.shape, lambda i, j: (0,) * x.ndim)

    weights = [vecs, mu8, wr, wk, wv, w1, w2, a1, a2, g1, g2]
    args = [h] + weights
    in_specs = [tok] + [full(w) for w in weights]
    if vres is not None:
        v1, v2, v_first = vres
        args += [v1, v2, v_first]
        in_specs += [full(v1), full(v2), tok]
    tok_bytes = _nbytes((ts, d), F32)
    block_bytes = sum(_nbytes(w.shape, w.dtype) for w in args[1:1 + len(weights) + (2 if vres else 0)])
    block_bytes += tok_bytes * (7 + (1 if vres else 0))
    out = jax.ShapeDtypeStruct((b, s, d), F32)
    return pl.pallas_call(
        functools.partial(_rwkv_in_kernel, has_vres=vres is not None),
        grid=(b, s // ts),
        in_specs=in_specs,
        out_specs=[tok] * 6,
        out_shape=[out] * 6,
        scratch_shapes=[pltpu.VMEM((SUBLANES, d), F32)],
        compiler_params=pltpu.CompilerParams(
            dimension_semantics=("parallel", "arbitrary"),
            vmem_limit_bytes=_vmem_limit(block_bytes, _nbytes((SUBLANES, d), F32), 12 * tok_bytes)),
        name="rwkv_in",
    )(*args)


def _wkv_kernel(r_ref, lw_ref, k_ref, v_ref, a_ref, g_ref, pv_ref, y_ref, s_ref, *, passes):
    n_rows, c, d_model = r_ref.shape
    n = 2 * c
    n_tiles = d_model // LANES

    @pl.when(pl.program_id(1) == 0)
    def _():
        s_ref[...] = jnp.zeros_like(s_ref)

    first_head = lax.broadcasted_iota(jnp.int32, (c, LANES), 1) < HEAD_DIM

    def head_sums(x):
        lo = jnp.sum(jnp.where(first_head, x, 0.0), axis=-1, keepdims=True)
        hi = jnp.sum(jnp.where(first_head, 0.0, x), axis=-1, keepdims=True)
        return jnp.where(first_head, lo, hi)

    ti = lax.broadcasted_iota(jnp.int32, (c, c), 0)
    si = lax.broadcasted_iota(jnp.int32, (c, c), 1)
    lincl = jnp.where(si <= ti, 1.0, 0.0).astype(BF16)
    log2c = c.bit_length() - 1
    row = lax.broadcasted_iota(jnp.int32, (n, LANES), 0)
    lane = lax.broadcasted_iota(jnp.int32, (n, LANES), 1)
    own = (row >> log2c) == (lane >> HEAD_SHIFT)
    ri = lax.broadcasted_iota(jnp.int32, (n, 2 * n), 0)
    ci = lax.broadcasted_iota(jnp.int32, (n, 2 * n), 1) & (c - 1)
    tri = ci < (ri & (c - 1)) + (ri >> log2c)
    inv_n = 1.0 / HEAD_DIM
    mm = functools.partial(_mm, passes=passes)
    mm_nt = functools.partial(_mm, passes=passes, dot=_dot_nt)
    mm_tn = functools.partial(_mm, passes=passes, dot=_dot_tn)

    def stack(x):
        return jnp.where(own, jnp.concatenate([x, x], axis=0), 0.0)

    steps = log2c
    rows_per_group = WKV_GROUP_ROWS if n_rows % WKV_GROUP_ROWS == 0 else 1
    groups = [[(bi, slice(t * LANES, (t + 1) * LANES), t)
               for bi in range(g0, g0 + rows_per_group) for t in range(n_tiles)]
              for g0 in range(0, n_rows, rows_per_group)]

    def each(f, *cols):
        return [f(*args) for args in zip(*cols)]

    def pro_keys(units):
        st = {"units": units}
        for name, ref in (("r", r_ref), ("lw", lw_ref), ("k_raw", k_ref), ("v", v_ref), ("a", a_ref)):
            st[name] = [ref[bi, :, sl] for bi, sl, _ in units]

        def param(i):
            return [pv_ref[i:i + 1, sl] for _, sl, _ in units]

        kk = each(lambda x, g: x * g, st["k_raw"], param(0))
        kk_ss = each(lambda x: head_sums(x * x), kk)
        st["kk"] = each(lambda x, ss: x * lax.rsqrt(jnp.maximum(ss, KK_EPS)), kk, kk_ss)
        st["k"] = each(lambda x, ai, g: x * (1.0 + (ai - 1.0) * g), st["k_raw"], st["a"], param(1))
        st["bonus"] = each(lambda ri_, ki, g, vi: head_sums(ri_ * ki * g) * vi, st["r"], st["k"], param(2), st["v"])
        return st

    def pro_cum(st):
        st["cum"] = each(lambda x: _dot_exact_lhs(lincl, x, WKV_CUM_PIECES), st["lw"])
        return st

    def pro_scale(st):
        g_incl = each(jnp.exp, st["cum"])
        g_excl = each(lambda cs, x: jnp.exp(cs - x), st["cum"], st["lw"])
        g_inv = each(lambda cs: jnp.exp(-cs), st["cum"])
        st["g_end"] = each(lambda g: g[c - 1:c, :], g_incl)
        st["ar"] = each(lambda kki, ri_, ge, gi: jnp.concatenate([-kki * ge, ri_ * gi], axis=0),
                        st["kk"], st["r"], g_excl, g_incl)
        st["bt"] = each(lambda kki, ai, gv: kki * ai * gv, st["kk"], st["a"], g_inv)
        st["kt"] = each(lambda ki, gv: ki * gv, st["k"], g_inv)
        return st

    def pro_stack(st):
        st["bks"] = each(lambda b_, k_: jnp.concatenate([stack(b_), stack(k_)], axis=0), st["bt"], st["kt"])
        st["vs"] = each(stack, st["v"])
        return st

    def pro_products(st):
        st["s0"] = [s_ref[bi, t] for bi, _, t in st["units"]]
        st["prod"] = each(lambda x, y: jnp.where(tri, mm_nt(x, y), 0.0), st["ar"], st["bks"])
        st["from_s0"] = each(mm_nt, st["ar"], st["s0"])
        return st

    pro_pieces = [pro_keys, pro_cum, pro_scale, pro_stack, pro_products]
    assert len(pro_pieces) <= steps

    def solve_start(st):
        x = each(lambda fs, pr, vs: fs[:c, :] + mm(pr[:c, n:], vs), st["from_s0"], st["prod"], st["vs"])
        return x, [pr[:c, :n] for pr in st["prod"]]

    def solve_stage(x, p, last):
        if last:
            return each(lambda xi, pi: xi + mm(pi, stack(xi)), x, p), None
        px = each(lambda xi, pi: mm(pi, jnp.concatenate([stack(xi), stack(pi)], axis=1)), x, p)
        return each(lambda xi, pxi: xi + pxi[:, :LANES], x, px), each(lambda pxi: pxi[:, LANES:], px)

    def finish(st, x):
        st["y"] = each(lambda fs, pr, vs, u: fs[c:, :] + mm(pr[c:, :], jnp.concatenate([stack(u), vs], axis=0)),
                       st["from_s0"], st["prod"], st["vs"], x)
        outer = each(lambda u, v, bt, kt, ge: mm_tn(jnp.concatenate([u, v], axis=0),
                                                    jnp.concatenate([bt * ge, kt * ge], axis=0)),
                     x, st["v"], st["bt"], st["kt"], st["g_end"])
        for (bi, _, t), s0, ge, o in zip(st["units"], st["s0"], st["g_end"], outer):
            s_ref[bi, t] = s0 * ge + jnp.where(own, o, 0.0)

    def epilogue(st, lo, hi):
        y = st["y"][lo:hi]
        mean = each(lambda yi: head_sums(yi) * inv_n, y)
        yc = each(lambda yi, m: yi - m, y, mean)
        var = each(lambda yi: head_sums(yi * yi) * inv_n, yc)
        for (bi, sl, _), bonus, yi, vr in zip(st["units"][lo:hi], st["bonus"][lo:hi], yc, var):
            out = yi * lax.rsqrt(vr + GN_EPS) * pv_ref[3:4, sl] + pv_ref[4:5, sl] + bonus
            y_ref[bi, :, sl] = (out * g_ref[bi, :, sl]).astype(BF16)

    cur = groups[0]
    for piece in pro_pieces:
        cur = piece(cur)
    prev = None
    for gi in range(len(groups)):
        x, p = solve_start(cur)
        nxt = groups[gi + 1] if gi + 1 < len(groups) else None
        n_units = len(groups[gi])
        for i in range(steps):
            x, p = solve_stage(x, p, i == steps - 1)
            if nxt is not None and i < len(pro_pieces):
                nxt = pro_pieces[i](nxt)
            if prev is not None:
                epilogue(prev, i * n_units // steps, (i + 1) * n_units // steps)
        finish(cur, x)
        prev, cur = cur, nxt
    epilogue(prev, 0, len(prev["units"]))


def _wkv(r, lw, k, v, a, g, pvec, passes):
    b, s, d = r.shape
    c = WKV_CHUNK
    rows = WKV_ROWS if b % WKV_ROWS == 0 else 1
    tok = pl.BlockSpec((rows, c, d), lambda i, j: (i, j, 0))
    par = pl.BlockSpec((SUBLANES, d), lambda i, j: (0, 0))
    tok_bytes = _nbytes((rows, c, d), F32)
    sq_bytes = _nbytes((2 * c, 2 * c), F32)
    n_tiles = d // LANES
    return pl.pallas_call(
        functools.partial(_wkv_kernel, passes=passes),
        grid=(b // rows, s // c),
        in_specs=[tok] * 6 + [par],
        out_specs=tok,
        out_shape=jax.ShapeDtypeStruct((b, s, d), BF16),
        scratch_shapes=[pltpu.VMEM((rows, n_tiles, LANES, LANES), F32)],
        compiler_params=pltpu.CompilerParams(
            dimension_semantics=("parallel", "arbitrary"),
            vmem_limit_bytes=_vmem_limit(8 * tok_bytes, rows * n_tiles * sq_bytes, rows * n_tiles * 48 * sq_bytes)),
        name="wkv",
    )(r, lw, k, v, a, g, pvec)


def _conv_in_kernel(h_ref, g_ref, win_ref, cw_ref, z_out, carry_ref):
    @pl.when(pl.program_id(1) == 0)
    def _():
        carry_ref[...] = jnp.zeros_like(carry_ref)

    ts, d = h_ref.shape
    u = _rms(h_ref[...], g_ref[...]).astype(BF16)
    z = _dot(u, win_ref[:, d:2 * d]) * _dot(u, win_ref[:, 2 * d:])
    b_gate = _dot(u, win_ref[:, :d])
    carry = carry_ref[...]
    z1 = _shift_rows(z, carry, 1)
    z2 = _shift_rows(z, carry, 2)
    carry_ref[...] = z[ts - SUBLANES:, :]
    zc = z2 * cw_ref[0:1, :] + z1 * cw_ref[1:2, :] + z * cw_ref[2:3, :]
    z_out[...] = (b_gate * zc).astype(BF16)


def _conv_in(h, g, w_in, cw8):
    b, s, d = h.shape
    ts = min(CONV_TS, s)
    tok = pl.BlockSpec((None, ts, d), lambda i, j: (i, j, 0))
    tok_bytes = _nbytes((ts, d), F32)
    block_bytes = 2 * tok_bytes + _nbytes(w_in.shape, BF16) + _nbytes(cw8.shape, F32)
    return pl.pallas_call(
        _conv_in_kernel,
        grid=(b, s // ts),
        in_specs=[tok,
                  pl.BlockSpec(g.shape, lambda i, j: (0, 0)),
                  pl.BlockSpec(w_in.shape, lambda i, j: (0, 0)),
                  pl.BlockSpec(cw8.shape, lambda i, j: (0, 0))],
        out_specs=tok,
        out_shape=jax.ShapeDtypeStruct((b, s, d), BF16),
        scratch_shapes=[pltpu.VMEM((SUBLANES, d), F32)],
        compiler_params=pltpu.CompilerParams(
            dimension_semantics=("parallel", "arbitrary"),
            vmem_limit_bytes=_vmem_limit(block_bytes, _nbytes((SUBLANES, d), F32), 10 * tok_bytes)),
        name="conv_in",
    )(h, g, w_in, cw8)


def _sb_qkv_kernel(h_ref, g_ref, w_ref, qg_ref, kg_ref, q_out, k_out, v_out):
    d = h_ref.shape[1]
    u = _rms(h_ref[...], g_ref[...]).astype(BF16)
    hsum = _head_sum_matrix(2 * LANES)
    inv_n = 1.0 / HEAD_DIM

    def head_norm(x, gain):
        cols = []
        for j in range(0, d, 2 * LANES):
            xs = x[:, j:j + 2 * LANES]
            ms = _dot_exact_rhs(xs * xs, hsum, QKV_SUM_PIECES) * inv_n
            cols.append(xs * lax.rsqrt(ms + RMS_EPS))
        return jnp.concatenate(cols, axis=1) * gain

    q = _dot(u, w_ref[:, :d])
    k = _dot(u, w_ref[:, d:2 * d])
    q_out[...] = head_norm(q, qg_ref[...]).astype(BF16)
    v = _dot(u, w_ref[:, 2 * d:])
    k_out[...] = head_norm(k, kg_ref[...]).astype(BF16)
    v_out[...] = v.astype(BF16)


def _sb_qkv(h2, g, w_qkv, qg, kg):
    t, d = h2.shape
    tm = min(QKV_TM, t)
    tok = pl.BlockSpec((tm, d), lambda i: (i, 0))
    vec = pl.BlockSpec((1, d), lambda i: (0, 0))
    tok_bytes = _nbytes((tm, d), F32)
    block_bytes = 4 * tok_bytes + _nbytes(w_qkv.shape, BF16) + 3 * _nbytes((1, d), F32)
    out = jax.ShapeDtypeStruct((t, d), BF16)
    return pl.pallas_call(
        _sb_qkv_kernel,
        grid=(t // tm,),
        in_specs=[tok, vec, pl.BlockSpec(w_qkv.shape, lambda i: (0, 0)), vec, vec],
        out_specs=[tok] * 3,
        out_shape=[out] * 3,
        compiler_params=pltpu.CompilerParams(
            dimension_semantics=("parallel",),
            vmem_limit_bytes=_vmem_limit(block_bytes, 0, 10 * tok_bytes)),
        name="sb_qkv",
    )(h2, g, w_qkv, qg, kg)


def _sb_attn_kernel(zmax_ref, q_ref, k_ref, v_ref, suffix_ref, o_ref,
                    qs_ref, acc_ref, run_ref, za_ref, zb_ref, att_ref):
    tq = q_ref.shape[0]
    n_lt = q_ref.shape[1] // LANES
    qb = pl.program_id(2)
    rows = lax.broadcasted_iota(jnp.int32, (2 * tq, LANES), 0)
    lanes = lax.broadcasted_iota(jnp.int32, (2 * tq, LANES), 1)
    own = (rows >> (tq.bit_length() - 1)) == (lanes >> HEAD_SHIFT)
    lts = [slice(p * LANES, (p + 1) * LANES) for p in range(n_lt)]
    for p, sl in enumerate(lts):
        q = q_ref[:, sl].astype(F32)
        qs_ref[p] = (jnp.where(own, jnp.concatenate([q, q], axis=0), 0.0) * SB_SCALE).astype(BF16)
    suffix = suffix_ref[...]

    groups = [slice(g * SB_ROWS, (g + 1) * SB_ROWS) for g in range(2 * tq // SB_ROWS)]
    units = [(p, g) for p in range(n_lt) for g in groups]

    def each(f, *cols):
        return [f(*args) for args in zip(*cols)]

    def key_tile(ref, j, p):
        return ref[pl.ds(pl.multiple_of(j * tq, tq), tq), lts[p]]

    def scores(j, z_out):
        for p in range(n_lt):
            ks = key_tile(k_ref, j, p)
            for g in groups:
                z_out[p, g, :] = _dot_nt(qs_ref[p, g, :], ks)

    def add_values(j):
        pv = []
        for p in range(n_lt):
            vs = key_tile(v_ref, j, p)
            pv += [_dot(att_ref[p, g, :], vs) for g in groups]
        return pv

    def step(j, z_in, z_out, diagonal):
        if not diagonal:
            pv = add_values(j + 1)
        sp = [_softplus(z_in[p, g, :]) for p, g in units]
        if diagonal:
            t_loc = lax.broadcasted_iota(jnp.int32, (SB_ROWS, tq), 0)
            s_loc = lax.broadcasted_iota(jnp.int32, (SB_ROWS, tq), 1)
            causal = [s_loc < t_loc + (g.start & (tq - 1)) for _, g in units]
            sp = each(lambda m, x: jnp.where(m, x, 0.0), causal, sp)
        rev_cum = each(lambda x: _dot_exact_rhs(x, suffix, SB_CUM_PIECES), sp)
        scores(jnp.maximum(j - 1, 0), z_out)
        for i, (p, g) in enumerate(units):
            rc = rev_cum[i]
            total = jnp.broadcast_to(rc[:, 0:1], (SB_ROWS, LANES))
            if diagonal:
                att = jnp.where(causal[i], jnp.exp(z_in[p, g, :] - rc), 0.0)
                run_ref[p, g, :] = total
                acc_ref[p, g, :] = jnp.zeros((SB_ROWS, LANES), F32)
            else:
                run = run_ref[p, g, :]
                att = jnp.exp(z_in[p, g, :] - rc - jnp.concatenate([run] * (tq // LANES), axis=1))
                run_ref[p, g, :] = run + total
                acc_ref[p, g, :] += pv[i]
            att_ref[p, g, :] = att.astype(BF16)

    stop_at = zmax_ref[0] + SB_EXP_ZERO

    def run_min():
        return jnp.min(run_ref[...])

    scores(qb, za_ref)
    step(qb, za_ref, zb_ref, True)

    @pl.when(qb >= 1)
    def _():
        step(qb - 1, zb_ref, za_ref, False)

    def more(state):
        j, low = state
        return jnp.logical_and(j >= 1, low < stop_at)

    def pair(state):
        j, _ = state
        step(j, za_ref, zb_ref, False)
        step(j - 1, zb_ref, za_ref, False)
        return j - 2, run_min()

    j, low = lax.while_loop(more, pair, (jnp.maximum(qb - 2, -1), run_min()))
    last_tile = jnp.logical_and(j == 0, low < stop_at)

    @pl.when(last_tile)
    def _():
        step(0, za_ref, zb_ref, False)

    pv = add_values(jnp.where(last_tile, 0, j + 1))
    for (p, g), pvi in zip(units, pv):
        acc_ref[p, g, :] += pvi
    lane = lax.broadcasted_iota(jnp.int32, (tq, LANES), 1)
    for p, sl in enumerate(lts):
        o_ref[:, sl] = jnp.where(lane >= HEAD_DIM, acc_ref[p, tq:, :], acc_ref[p, :tq, :]).astype(BF16)


def _sb_attn(q, k, v, z_max):
    b, s, d = q.shape
    tq = SB_TQ
    n_lt = min(SB_LANE_TILES, d // LANES)
    width = n_lt * LANES
    ri = lax.broadcasted_iota(jnp.int32, (tq, tq), 0)
    ci = lax.broadcasted_iota(jnp.int32, (tq, tq), 1)
    suffix = jnp.where(ri >= ci, 1.0, 0.0).astype(BF16)
    qspec = pl.BlockSpec((None, tq, width), lambda i, p, j: (i, j, p))
    kspec = pl.BlockSpec((None, s, width), lambda i, p, j: (i, 0, p))
    block_bytes = 2 * _nbytes((tq, width), BF16) + 2 * _nbytes((s, width), BF16) + _nbytes((tq, tq), BF16)
    scratch = [pltpu.VMEM((n_lt, 2 * tq, LANES), BF16),
               pltpu.VMEM((n_lt, 2 * tq, LANES), F32),
               pltpu.VMEM((n_lt, 2 * tq, LANES), F32),
               pltpu.VMEM((n_lt, 2 * tq, tq), F32),
               pltpu.VMEM((n_lt, 2 * tq, tq), F32),
               pltpu.VMEM((n_lt, 2 * tq, tq), BF16)]
    scratch_bytes = n_lt * (_nbytes((2 * tq, LANES), BF16) + 2 * _nbytes((2 * tq, LANES), F32)
                            + 2 * _nbytes((2 * tq, tq), F32) + _nbytes((2 * tq, tq), BF16))
    return pl.pallas_call(
        _sb_attn_kernel,
        grid=(b, d // width, s // tq),
        in_specs=[pl.BlockSpec(memory_space=pltpu.SMEM), qspec, kspec, kspec,
                  pl.BlockSpec((tq, tq), lambda i, p, j: (0, 0))],
        out_specs=qspec,
        out_shape=jax.ShapeDtypeStruct((b, s, d), BF16),
        scratch_shapes=scratch,
        compiler_params=pltpu.CompilerParams(
            dimension_semantics=("parallel", "parallel", "arbitrary"),
            vmem_limit_bytes=_vmem_limit(block_bytes, scratch_bytes, 12 * _nbytes((2 * tq, tq), F32))),
        name="sb_attn",
    )(z_max.reshape(1).astype(F32), q, k, v, suffix)


def _outproj_mlp_kernel(h_ref, z_ref, wo_ref, g_ref, wup_ref, wdn_ref, o_ref, xn_ref):
    @pl.when(pl.program_id(1) == 0)
    def _():
        h1 = h_ref[...] + _dot(z_ref[...], wo_ref[...])
        o_ref[...] = h1
        xn_ref[...] = _rms(h1, g_ref[...]).astype(BF16)

    act = jnp.square(jnp.maximum(_dot(xn_ref[...], wup_ref[...]), 0.0)).astype(BF16)
    o_ref[...] += _dot(act, wdn_ref[...])


def _outproj_mlp(h2, z, w_o, g, w_up, w_dn):
    t, d = h2.shape
    f = w_up.shape[1]
    tm = min(MLP_TM, t)
    tf = min(MLP_TF, f)
    tok = pl.BlockSpec((tm, d), lambda i, j: (i, 0))
    in_specs = [
        tok, tok,
        pl.BlockSpec((d, d), lambda i, j: (0, 0)),
        pl.BlockSpec((1, d), lambda i, j: (0, 0)),
        pl.BlockSpec((d, tf), lambda i, j: (0, j)),
        pl.BlockSpec((tf, d), lambda i, j: (j, 0)),
    ]
    tok_bytes = _nbytes((tm, d), F32)
    block_bytes = 2 * tok_bytes + _nbytes((tm, d), BF16) + _nbytes((d, d), BF16) + 2 * _nbytes((d, tf), BF16)
    return pl.pallas_call(
        _outproj_mlp_kernel,
        grid=(t // tm, f // tf),
        in_specs=in_specs,
        out_specs=tok,
        out_shape=jax.ShapeDtypeStruct((t, d), F32),
        scratch_shapes=[pltpu.VMEM((tm, d), BF16)],
        compiler_params=pltpu.CompilerParams(
            dimension_semantics=("parallel", "arbitrary"),
            vmem_limit_bytes=_vmem_limit(block_bytes, _nbytes((tm, d), BF16),
                                         tok_bytes + 2 * _nbytes((tm, tf), F32))),
        name="outproj_mlp",
    )(h2, z, w_o, g, w_up, w_dn)


def _pad_cols(w, mult):
    pad = (-w.shape[1]) % mult
    return jnp.pad(w, ((0, 0), (0, pad)))


def _pad_rows(w, mult):
    pad = (-w.shape[0]) % mult
    return jnp.pad(w, ((0, pad), (0, 0)))


def _rows8(*rows):
    d = rows[0].shape[-1]
    out = jnp.zeros((SUBLANES, d), F32)
    return out.at[:len(rows)].set(jnp.stack([r.reshape(d) for r in rows]))


def _lora(w_in, w_out):
    return _pad_cols(w_in, LANES).astype(BF16), _pad_rows(w_out, LANES).astype(BF16)


def kernel(x, mix_norm, mlp_norm, mlp_up, mlp_down, rwkv_mu, rwkv_w_r, rwkv_w_k, rwkv_w_v, rwkv_w_o, rwkv_decay_w0, rwkv_decay_w1, rwkv_decay_w2, rwkv_iclr_a0, rwkv_iclr_a1, rwkv_iclr_a2, rwkv_gate_g1, rwkv_gate_g2, rwkv_k_k, rwkv_k_a, rwkv_r_k, rwkv_lnx_w, rwkv_lnx_b, rwkv_vres_v0, rwkv_vres_v1, rwkv_vres_v2, conv_w_in, conv_w, conv_w_out, sb_w_qkv, sb_q_norm, sb_k_norm, sb_w_o):
    b, s, d = x.shape
    depth = mix_norm.shape[0]
    n_heads = d // HEAD_DIM
    h = x
    v_first = None
    for i in range(depth):
        kind = i % 3
        j = i // 3
        g_mix = mix_norm[i].reshape(1, d)
        if kind == 0:
            vres = None
            v0 = jnp.zeros((d,), F32)
            if j > 0:
                v1, v2 = _lora(rwkv_vres_v1[j - 1], rwkv_vres_v2[j - 1])
                vres = (v1, v2, v_first)
                v0 = rwkv_vres_v0[j - 1]
            vecs = _rows8(mix_norm[i], rwkv_decay_w0[j], rwkv_iclr_a0[j], v0)
            mu8 = _rows8(*[rwkv_mu[j, m] for m in range(rwkv_mu.shape[1])])
            w1, w2 = _lora(rwkv_decay_w1[j], rwkv_decay_w2[j])
            a1, a2 = _lora(rwkv_iclr_a1[j], rwkv_iclr_a2[j])
            g1, g2 = _lora(rwkv_gate_g1[j], rwkv_gate_g2[j])
            r, k, v, lw, a, gate = _rwkv_in(
                h, vecs, mu8, rwkv_w_r[j].astype(BF16), rwkv_w_k[j].astype(BF16),
                rwkv_w_v[j].astype(BF16), w1, w2, a1, a2, g1, g2, vres)
            if j == 0:
                v_first = v
            pvec = _rows8(rwkv_k_k[j], rwkv_k_a[j], rwkv_r_k[j].reshape(d), rwkv_lnx_w[j], rwkv_lnx_b[j])
            z = _wkv(r, lw, k, v, a, gate, pvec, WKV_PASSES)
            w_o = rwkv_w_o[j]
        elif kind == 1:
            z = _conv_in(h, g_mix, conv_w_in[j].astype(BF16), _rows8(*[conv_w[j, m] for m in range(3)]))
            w_o = conv_w_out[j]
        else:
            qg = jnp.tile(sb_q_norm[j], n_heads).reshape(1, d)
            kg = jnp.tile(sb_k_norm[j], n_heads).reshape(1, d)
            q, k, v = _sb_qkv(h.reshape(b * s, d), g_mix, sb_w_qkv[j].astype(BF16), qg, kg)
            z_max = (SB_ZMAX_SLACK * HEAD_DIM * SB_SCALE
                     * jnp.max(jnp.abs(sb_q_norm[j])) * jnp.max(jnp.abs(sb_k_norm[j])))
            z = _sb_attn(q.reshape(b, s, d), k.reshape(b, s, d), v.reshape(b, s, d), z_max)
            w_o = sb_w_o[j]
        h = _outproj_mlp(
            h.reshape(b * s, d), z.reshape(b * s, d), w_o.astype(BF16), mlp_norm[i].reshape(1, d), mlp_up[i].astype(BF16), mlp_down[i].astype(BF16),
        ).reshape(b, s, d)
    return h
```

```python
import functools
import math

import jax
import jax.numpy as jnp
from jax import lax
from jax.experimental import pallas as pl
from jax.experimental.pallas import tpu as pltpu

F32 = jnp.float32
BF16 = jnp.bfloat16

HEAD_DIM = 64
HEAD_SHIFT = 6
LANES = 128
SUBLANES = 8
RMS_EPS = 1e-6
GN_EPS = 64e-5
KK_EPS = 1e-24
SB_SCALE = HEAD_DIM ** -0.5
SB_CUM_PIECES = 1
SB_EXP_ZERO = 105.0
SB_ZMAX_SLACK = 1.02

V7X_VMEM_BYTES = 64 * 1024 * 1024
VMEM_LIMIT_CAP = V7X_VMEM_BYTES - 8 * 1024 * 1024

MLP_TM = 1024
MLP_TF = 1024
RWKV_TS = 256
CONV_TS = 512
QKV_TM = 512
QKV_SUM_PIECES = 1
WKV_CHUNK = 64
WKV_ROWS = 8
WKV_GROUP_ROWS = 2
WKV_PASSES = 1
WKV_CUM_PIECES = 2
WKV_SUM_PIECES = 1
SB_TQ = 256
SB_ROWS = 256
SB_LANE_TILES = 8


def _vmem_limit(block_bytes, scratch_bytes, temp_bytes):
    return int(min(2 * block_bytes + scratch_bytes + temp_bytes, VMEM_LIMIT_CAP))


def _nbytes(shape, dtype):
    n = 1
    for s in shape:
        n *= s
    return n * jnp.dtype(dtype).itemsize


def _dot(a, b):
    return jnp.dot(a, b, preferred_element_type=F32)


def _dot_nt(a, b):
    return lax.dot_general(a, b, (((1,), (1,)), ((), ())), preferred_element_type=F32)


def _dot_tn(a, b):
    return lax.dot_general(a, b, (((0,), (0,)), ((), ())), preferred_element_type=F32)


def _split(x, pieces):
    out = []
    for _ in range(pieces - 1):
        hi = x.astype(BF16)
        out.append(hi)
        x = x - hi.astype(F32)
    out.append(x.astype(BF16))
    return out


def _dot_exact_rhs(x, m_bf16, pieces, dot=_dot):
    acc = None
    for p in _split(x, pieces):
        t = dot(p, m_bf16)
        acc = t if acc is None else acc + t
    return acc


def _dot_exact_lhs(m_bf16, x, pieces):
    acc = None
    for p in _split(x, pieces):
        t = _dot(m_bf16, p)
        acc = t if acc is None else acc + t
    return acc


def _mm(a, b, passes, dot=_dot):
    if passes == 1:
        return dot(a.astype(BF16), b.astype(BF16))
    ah, al = _split(a, 2)
    bh, bl = _split(b, 2)
    return dot(ah, bh) + (dot(al, bh) + dot(ah, bl))


def _rms(x, g):
    ms = jnp.mean(x * x, axis=-1, keepdims=True)
    return x * lax.rsqrt(ms + RMS_EPS) * g


def _sigmoid(x):
    return 1.0 / (1.0 + jnp.exp(-x))


def _softplus(x):
    return jnp.maximum(x, 0.0) + jnp.log(1.0 + jnp.exp(-jnp.abs(x)))


def _head_sum_matrix(n):
    r = lax.broadcasted_iota(jnp.int32, (n, n), 0) >> HEAD_SHIFT
    c = lax.broadcasted_iota(jnp.int32, (n, n), 1) >> HEAD_SHIFT
    return jnp.where(r == c, 1.0, 0.0).astype(BF16)


def _shift_rows(x, carry8, n):
    rows = lax.broadcasted_iota(jnp.int32, (x.shape[0], 1), 0)
    out = pltpu.roll(x, n, axis=0)
    for i in range(n):
        out = jnp.where(rows == i, carry8[SUBLANES - n + i:SUBLANES - n + i + 1, :], out)
    return out


def _rwkv_in_kernel(*refs, has_vres):
    if has_vres:
        (h_ref, vec_ref, mu_ref, wr_ref, wk_ref, wv_ref, w1_ref, w2_ref, a1_ref, a2_ref,
         g1_ref, g2_ref, v1_ref, v2_ref, vf_ref,
         r_out, k_out, v_out, lw_out, a_out, g_out, carry_ref) = refs
    else:
        (h_ref, vec_ref, mu_ref, wr_ref, wk_ref, wv_ref, w1_ref, w2_ref, a1_ref, a2_ref,
         g1_ref, g2_ref,
         r_out, k_out, v_out, lw_out, a_out, g_out, carry_ref) = refs

    @pl.when(pl.program_id(1) == 0)
    def _():
        carry_ref[...] = jnp.zeros_like(carry_ref)

    ts = h_ref.shape[0]
    u = _rms(h_ref[...], vec_ref[0:1, :])
    prev = _shift_rows(u, carry_ref[...], 1)
    carry_ref[...] = u[ts - SUBLANES:, :]
    xx = prev - u

    def mix(i):
        return (u + xx * mu_ref[i:i + 1, :]).astype(BF16)

    hw = jnp.tanh(_dot(mix(1), w1_ref[...])).astype(BF16)
    ha = _dot(mix(4), a1_ref[...]).astype(BF16)
    hg = _sigmoid(_dot(mix(5), g1_ref[...])).astype(BF16)
    xv = mix(3)
    if has_vres:
        hv = _dot(xv, v1_ref[...]).astype(BF16)
    dw = vec_ref[1:2, :] + _dot(hw, w2_ref[...])
    da = vec_ref[2:3, :] + _dot(ha, a2_ref[...])
    g_out[...] = _dot(hg, g2_ref[...])
    if has_vres:
        dv = vec_ref[3:4, :] + _dot(hv, v2_ref[...])

    r_out[...] = _dot(mix(0), wr_ref[...])
    lw_out[...] = -jnp.exp(-_softplus(-dw) - 0.5)
    k_out[...] = _dot(mix(2), wk_ref[...])
    a_out[...] = _sigmoid(da)
    v = _dot(xv, wv_ref[...])
    if has_vres:
        v = v + (vf_ref[...] - v) * _sigmoid(dv)
    v_out[...] = v


def _rwkv_in(h, vecs, mu8, wr, wk, wv, w1, w2, a1, a2, g1, g2, vres):
    b, s, d = h.shape
    ts = min(RWKV_TS, s)
    tok = pl.BlockSpec((None, ts, d), lambda i, j: (i, j, 0))

    def full(x):
        return pl.BlockSpec(x.shape, lambda i, j: (0,) * x.ndim)

    weights = [vecs, mu8, wr, wk, wv, w1, w2, a1, a2, g1, g2]
    args = [h] + weights
    in_specs = [tok] + [full(w) for w in weights]
    if vres is not None:
        v1, v2, v_first = vres
        args += [v1, v2, v_first]
        in_specs += [full(v1), full(v2), tok]
    tok_bytes = _nbytes((ts, d), F32)
    block_bytes = sum(_nbytes(w.shape, w.dtype) for w in args[1:1 + len(weights) + (2 if vres else 0)])
    block_bytes += tok_bytes * (7 + (1 if vres else 0))
    out = jax.ShapeDtypeStruct((b, s, d), F32)
    return pl.pallas_call(
        functools.partial(_rwkv_in_kernel, has_vres=vres is not None),
        grid=(b, s // ts),
        in_specs=in_specs,
        out_specs=[tok] * 6,
        out_shape=[out] * 6,
        scratch_shapes=[pltpu.VMEM((SUBLANES, d), F32)],
        compiler_params=pltpu.CompilerParams(
            dimension_semantics=("parallel", "arbitrary"),
            vmem_limit_bytes=_vmem_limit(block_bytes, _nbytes((SUBLANES, d), F32), 12 * tok_bytes)),
        name="rwkv_in",
    )(*args)


def _wkv_kernel(r_ref, lw_ref, k_ref, v_ref, a_ref, g_ref, pv_ref, y_ref, s_ref, *, passes):
    n_rows, c, d_model = r_ref.shape
    n = 2 * c
    n_tiles = d_model // LANES

    @pl.when(pl.program_id(1) == 0)
    def _():
        s_ref[...] = jnp.zeros_like(s_ref)

    first_head = lax.broadcasted_iota(jnp.int32, (c, LANES), 1) < HEAD_DIM

    def head_sums(x):
        lo = jnp.sum(jnp.where(first_head, x, 0.0), axis=-1, keepdims=True)
        hi = jnp.sum(jnp.where(first_head, 0.0, x), axis=-1, keepdims=True)
        return jnp.where(first_head, lo, hi)

    ti = lax.broadcasted_iota(jnp.int32, (c, c), 0)
    si = lax.broadcasted_iota(jnp.int32, (c, c), 1)
    lincl = jnp.where(si <= ti, 1.0, 0.0).astype(BF16)
    log2c = c.bit_length() - 1
    row = lax.broadcasted_iota(jnp.int32, (n, LANES), 0)
    lane = lax.broadcasted_iota(jnp.int32, (n, LANES), 1)
    own = (row >> log2c) == (lane >> HEAD_SHIFT)
    ri = lax.broadcasted_iota(jnp.int32, (n, 2 * n), 0)
    ci = lax.broadcasted_iota(jnp.int32, (n, 2 * n), 1) & (c - 1)
    tri = ci < (ri & (c - 1)) + (ri >> log2c)
    inv_n = 1.0 / HEAD_DIM
    mm = functools.partial(_mm, passes=passes)
    mm_nt = functools.partial(_mm, passes=passes, dot=_dot_nt)
    mm_tn = functools.partial(_mm, passes=passes, dot=_dot_tn)

    def stack(x):
        return jnp.where(own, jnp.concatenate([x, x], axis=0), 0.0)

    steps = log2c
    rows_per_group = WKV_GROUP_ROWS if n_rows % WKV_GROUP_ROWS == 0 else 1
    groups = [[(bi, slice(t * LANES, (t + 1) * LANES), t)
               for bi in range(g0, g0 + rows_per_group) for t in range(n_tiles)]
              for g0 in range(0, n_rows, rows_per_group)]

    def each(f, *cols):
        return [f(*args) for args in zip(*cols)]

    def pro_keys(units):
        st = {"units": units}
        for name, ref in (("r", r_ref), ("lw", lw_ref), ("k_raw", k_ref), ("v", v_ref), ("a", a_ref)):
            st[name] = [ref[bi, :, sl] for bi, sl, _ in units]

        def param(i):
            return [pv_ref[i:i + 1, sl] for _, sl, _ in units]

        kk = each(lambda x, g: x * g, st["k_raw"], param(0))
        kk_ss = each(lambda x: head_sums(x * x), kk)
        st["kk"] = each(lambda x, ss: x * lax.rsqrt(jnp.maximum(ss, KK_EPS)), kk, kk_ss)
        st["k"] = each(lambda x, ai, g: x * (1.0 + (ai - 1.0) * g), st["k_raw"], st["a"], param(1))
        st["bonus"] = each(lambda ri_, ki, g, vi: head_sums(ri_ * ki * g) * vi, st["r"], st["k"], param(2), st["v"])
        return st

    def pro_cum(st):
        st["cum"] = each(lambda x: _dot_exact_lhs(lincl, x, WKV_CUM_PIECES), st["lw"])
        return st

    def pro_scale(st):
        g_incl = each(jnp.exp, st["cum"])
        g_excl = each(lambda cs, x: jnp.exp(cs - x), st["cum"], st["lw"])
        g_inv = each(lambda cs: jnp.exp(-cs), st["cum"])
        st["g_end"] = each(lambda g: g[c - 1:c, :], g_incl)
        st["ar"] = each(lambda kki, ri_, ge, gi: jnp.concatenate([-kki * ge, ri_ * gi], axis=0),
                        st["kk"], st["r"], g_excl, g_incl)
        st["bt"] = each(lambda kki, ai, gv: kki * ai * gv, st["kk"], st["a"], g_inv)
        st["kt"] = each(lambda ki, gv: ki * gv, st["k"], g_inv)
        return st

    def pro_stack(st):
        st["bks"] = each(lambda b_, k_: jnp.concatenate([stack(b_), stack(k_)], axis=0), st["bt"], st["kt"])
        st["vs"] = each(stack, st["v"])
        return st

    def pro_products(st):
        st["s0"] = [s_ref[bi, t] for bi, _, t in st["units"]]
        st["prod"] = each(lambda x, y: jnp.where(tri, mm_nt(x, y), 0.0), st["ar"], st["bks"])
        st["from_s0"] = each(mm_nt, st["ar"], st["s0"])
        return st

    pro_pieces = [pro_keys, pro_cum, pro_scale, pro_stack, pro_products]
    assert len(pro_pieces) <= steps

    def solve_start(st):
        x = each(lambda fs, pr, vs: fs[:c, :] + mm(pr[:c, n:], vs), st["from_s0"], st["prod"], st["vs"])
        return x, [pr[:c, :n] for pr in st["prod"]]

    def solve_stage(x, p, last):
        if last:
            return each(lambda xi, pi: xi + mm(pi, stack(xi)), x, p), None
        px = each(lambda xi, pi: mm(pi, jnp.concatenate([stack(xi), stack(pi)], axis=1)), x, p)
        return each(lambda xi, pxi: xi + pxi[:, :LANES], x, px), each(lambda pxi: pxi[:, LANES:], px)

    def finish(st, x):
        st["y"] = each(lambda fs, pr, vs, u: fs[c:, :] + mm(pr[c:, :], jnp.concatenate([stack(u), vs], axis=0)),
                       st["from_s0"], st["prod"], st["vs"], x)
        outer = each(lambda u, v, bt, kt, ge: mm_tn(jnp.concatenate([u, v], axis=0),
                                                    jnp.concatenate([bt * ge, kt * ge], axis=0)),
                     x, st["v"], st["bt"], st["kt"], st["g_end"])
        for (bi, _, t), s0, ge, o in zip(st["units"], st["s0"], st["g_end"], outer):
            s_ref[bi, t] = s0 * ge + jnp.where(own, o, 0.0)

    def epilogue(st, lo, hi):
        y = st["y"][lo:hi]
        mean = each(lambda yi: head_sums(yi) * inv_n, y)
        yc = each(lambda yi, m: yi - m, y, mean)
        var = each(lambda yi: head_sums(yi * yi) * inv_n, yc)
        for (bi, sl, _), bonus, yi, vr in zip(st["units"][lo:hi], st["bonus"][lo:hi], yc, var):
            out = yi * lax.rsqrt(vr + GN_EPS) * pv_ref[3:4, sl] + pv_ref[4:5, sl] + bonus
            y_ref[bi, :, sl] = (out * g_ref[bi, :, sl]).astype(BF16)

    cur = groups[0]
    for piece in pro_pieces:
        cur = piece(cur)
    prev = None
    for gi in range(len(groups)):
        x, p = solve_start(cur)
        nxt = groups[gi + 1] if gi + 1 < len(groups) else None
        n_units = len(groups[gi])
        for i in range(steps):
            x, p = solve_stage(x, p, i == steps - 1)
            if nxt is not None and i < len(pro_pieces):
                nxt = pro_pieces[i](nxt)
            if prev is not None:
                epilogue(prev, i * n_units // steps, (i + 1) * n_units // steps)
        finish(cur, x)
        prev, cur = cur, nxt
    epilogue(prev, 0, len(prev["units"]))


def _wkv(r, lw, k, v, a, g, pvec, passes):
    b, s, d = r.shape
    c = WKV_CHUNK
    rows = math.gcd(b, WKV_ROWS)
    tok = pl.BlockSpec((rows, c, d), lambda i, j: (i, j, 0))
    par = pl.BlockSpec((SUBLANES, d), lambda i, j: (0, 0))
    tok_bytes = _nbytes((rows, c, d), F32)
    sq_bytes = _nbytes((2 * c, 2 * c), F32)
    n_tiles = d // LANES
    return pl.pallas_call(
        functools.partial(_wkv_kernel, passes=passes),
        grid=(b // rows, s // c),
        in_specs=[tok] * 6 + [par],
        out_specs=tok,
        out_shape=jax.ShapeDtypeStruct((b, s, d), BF16),
        scratch_shapes=[pltpu.VMEM((rows, n_tiles, LANES, LANES), F32)],
        compiler_params=pltpu.CompilerParams(
            dimension_semantics=("parallel", "arbitrary"),
            vmem_limit_bytes=_vmem_limit(8 * tok_bytes, rows * n_tiles * sq_bytes, rows * n_tiles * 48 * sq_bytes)),
        name="wkv",
    )(r, lw, k, v, a, g, pvec)


def _conv_in_kernel(h_ref, g_ref, win_ref, cw_ref, z_out, carry_ref):
    @pl.when(pl.program_id(1) == 0)
    def _():
        carry_ref[...] = jnp.zeros_like(carry_ref)

    ts, d = h_ref.shape
    u = _rms(h_ref[...], g_ref[...]).astype(BF16)
    z = _dot(u, win_ref[:, d:2 * d]) * _dot(u, win_ref[:, 2 * d:])
    b_gate = _dot(u, win_ref[:, :d])
    carry = carry_ref[...]
    z1 = _shift_rows(z, carry, 1)
    z2 = _shift_rows(z, carry, 2)
    carry_ref[...] = z[ts - SUBLANES:, :]
    zc = z2 * cw_ref[0:1, :] + z1 * cw_ref[1:2, :] + z * cw_ref[2:3, :]
    z_out[...] = (b_gate * zc).astype(BF16)


def _conv_in(h, g, w_in, cw8):
    b, s, d = h.shape
    ts = min(CONV_TS, s)
    tok = pl.BlockSpec((None, ts, d), lambda i, j: (i, j, 0))
    tok_bytes = _nbytes((ts, d), F32)
    block_bytes = 2 * tok_bytes + _nbytes(w_in.shape, BF16) + _nbytes(cw8.shape, F32)
    return pl.pallas_call(
        _conv_in_kernel,
        grid=(b, s // ts),
        in_specs=[tok,
                  pl.BlockSpec(g.shape, lambda i, j: (0, 0)),
                  pl.BlockSpec(w_in.shape, lambda i, j: (0, 0)),
                  pl.BlockSpec(cw8.shape, lambda i, j: (0, 0))],
        out_specs=tok,
        out_shape=jax.ShapeDtypeStruct((b, s, d), BF16),
        scratch_shapes=[pltpu.VMEM((SUBLANES, d), F32)],
        compiler_params=pltpu.CompilerParams(
            dimension_semantics=("parallel", "arbitrary"),
            vmem_limit_bytes=_vmem_limit(block_bytes, _nbytes((SUBLANES, d), F32), 10 * tok_bytes)),
        name="conv_in",
    )(h, g, w_in, cw8)


def _sb_qkv_kernel(h_ref, g_ref, w_ref, qg_ref, kg_ref, q_out, k_out, v_out):
    d = h_ref.shape[1]
    u = _rms(h_ref[...], g_ref[...]).astype(BF16)
    hsum = _head_sum_matrix(2 * LANES)
    inv_n = 1.0 / HEAD_DIM

    def head_norm(x, gain):
        cols = []
        for j in range(0, d, 2 * LANES):
            xs = x[:, j:j + 2 * LANES]
            ms = _dot_exact_rhs(xs * xs, hsum, QKV_SUM_PIECES) * inv_n
            cols.append(xs * lax.rsqrt(ms + RMS_EPS))
        return jnp.concatenate(cols, axis=1) * gain

    q = _dot(u, w_ref[:, :d])
    k = _dot(u, w_ref[:, d:2 * d])
    q_out[...] = head_norm(q, qg_ref[...]).astype(BF16)
    v = _dot(u, w_ref[:, 2 * d:])
    k_out[...] = head_norm(k, kg_ref[...]).astype(BF16)
    v_out[...] = v.astype(BF16)


def _sb_qkv(h2, g, w_qkv, qg, kg):
    t, d = h2.shape
    tm = min(QKV_TM, t)
    tok = pl.BlockSpec((tm, d), lambda i: (i, 0))
    vec = pl.BlockSpec((1, d), lambda i: (0, 0))
    tok_bytes = _nbytes((tm, d), F32)
    block_bytes = 4 * tok_bytes + _nbytes(w_qkv.shape, BF16) + 3 * _nbytes((1, d), F32)
    out = jax.ShapeDtypeStruct((t, d), BF16)
    return pl.pallas_call(
        _sb_qkv_kernel,
        grid=(t // tm,),
        in_specs=[tok, vec, pl.BlockSpec(w_qkv.shape, lambda i: (0, 0)), vec, vec],
        out_specs=[tok] * 3,
        out_shape=[out] * 3,
        compiler_params=pltpu.CompilerParams(
            dimension_semantics=("parallel",),
            vmem_limit_bytes=_vmem_limit(block_bytes, 0, 10 * tok_bytes)),
        name="sb_qkv",
    )(h2, g, w_qkv, qg, kg)


def _sb_attn_kernel(zmax_ref, q_ref, k_ref, v_ref, suffix_ref, o_ref,
                    qs_ref, acc_ref, run_ref, za_ref, zb_ref, att_ref):
    tq = q_ref.shape[0]
    n_lt = q_ref.shape[1] // LANES
    qb = pl.program_id(2)
    rows = lax.broadcasted_iota(jnp.int32, (2 * tq, LANES), 0)
    lanes = lax.broadcasted_iota(jnp.int32, (2 * tq, LANES), 1)
    own = (rows >> (tq.bit_length() - 1)) == (lanes >> HEAD_SHIFT)
    lts = [slice(p * LANES, (p + 1) * LANES) for p in range(n_lt)]
    for p, sl in enumerate(lts):
        q = q_ref[:, sl].astype(F32)
        qs_ref[p] = (jnp.where(own, jnp.concatenate([q, q], axis=0), 0.0) * SB_SCALE).astype(BF16)
    suffix = suffix_ref[...]

    groups = [slice(g * SB_ROWS, (g + 1) * SB_ROWS) for g in range(2 * tq // SB_ROWS)]
    units = [(p, g) for p in range(n_lt) for g in groups]

    def each(f, *cols):
        return [f(*args) for args in zip(*cols)]

    def key_tile(ref, j, p):
        return ref[pl.ds(pl.multiple_of(j * tq, tq), tq), lts[p]]

    def scores(j, z_out):
        for p in range(n_lt):
            ks = key_tile(k_ref, j, p)
            for g in groups:
                z_out[p, g, :] = _dot_nt(qs_ref[p, g, :], ks)

    def add_values(j):
        pv = []
        for p in range(n_lt):
            vs = key_tile(v_ref, j, p)
            pv += [_dot(att_ref[p, g, :], vs) for g in groups]
        return pv

    def step(j, z_in, z_out, diagonal):
        if not diagonal:
            pv = add_values(j + 1)
        sp = [_softplus(z_in[p, g, :]) for p, g in units]
        if diagonal:
            t_loc = lax.broadcasted_iota(jnp.int32, (SB_ROWS, tq), 0)
            s_loc = lax.broadcasted_iota(jnp.int32, (SB_ROWS, tq), 1)
            causal = [s_loc < t_loc + (g.start & (tq - 1)) for _, g in units]
            sp = each(lambda m, x: jnp.where(m, x, 0.0), causal, sp)
        rev_cum = each(lambda x: _dot_exact_rhs(x, suffix, SB_CUM_PIECES), sp)
        scores(jnp.maximum(j - 1, 0), z_out)
        for i, (p, g) in enumerate(units):
            rc = rev_cum[i]
            total = jnp.broadcast_to(rc[:, 0:1], (SB_ROWS, LANES))
            if diagonal:
                att = jnp.where(causal[i], jnp.exp(z_in[p, g, :] - rc), 0.0)
                run_ref[p, g, :] = total
                acc_ref[p, g, :] = jnp.zeros((SB_ROWS, LANES), F32)
            else:
                run = run_ref[p, g, :]
                att = jnp.exp(z_in[p, g, :] - rc - jnp.concatenate([run] * (tq // LANES), axis=1))
                run_ref[p, g, :] = run + total
                acc_ref[p, g, :] += pv[i]
            att_ref[p, g, :] = att.astype(BF16)

    stop_at = zmax_ref[0] + SB_EXP_ZERO

    def run_min():
        return jnp.min(run_ref[...])

    scores(qb, za_ref)
    step(qb, za_ref, zb_ref, True)

    @pl.when(qb >= 1)
    def _():
        step(qb - 1, zb_ref, za_ref, False)

    def more(state):
        j, low = state
        return jnp.logical_and(j >= 1, low < stop_at)

    def pair(state):
        j, _ = state
        step(j, za_ref, zb_ref, False)
        step(j - 1, zb_ref, za_ref, False)
        return j - 2, run_min()

    j, low = lax.while_loop(more, pair, (jnp.maximum(qb - 2, -1), run_min()))
    last_tile = jnp.logical_and(j == 0, low < stop_at)

    @pl.when(last_tile)
    def _():
        step(0, za_ref, zb_ref, False)

    pv = add_values(jnp.where(last_tile, 0, j + 1))
    for (p, g), pvi in zip(units, pv):
        acc_ref[p, g, :] += pvi
    lane = lax.broadcasted_iota(jnp.int32, (tq, LANES), 1)
    for p, sl in enumerate(lts):
        o_ref[:, sl] = jnp.where(lane >= HEAD_DIM, acc_ref[p, tq:, :], acc_ref[p, :tq, :]).astype(BF16)


def _sb_attn(q, k, v, z_max):
    b, s, d = q.shape
    tq = SB_TQ
    n_lt = min(SB_LANE_TILES, d // LANES)
    width = n_lt * LANES
    ri = lax.broadcasted_iota(jnp.int32, (tq, tq), 0)
    ci = lax.broadcasted_iota(jnp.int32, (tq, tq), 1)
    suffix = jnp.where(ri >= ci, 1.0, 0.0).astype(BF16)
    qspec = pl.BlockSpec((None, tq, width), lambda i, p, j: (i, j, p))
    kspec = pl.BlockSpec((None, s, width), lambda i, p, j: (i, 0, p))
    block_bytes = 2 * _nbytes((tq, width), BF16) + 2 * _nbytes((s, width), BF16) + _nbytes((tq, tq), BF16)
    scratch = [pltpu.VMEM((n_lt, 2 * tq, LANES), BF16),
               pltpu.VMEM((n_lt, 2 * tq, LANES), F32),
               pltpu.VMEM((n_lt, 2 * tq, LANES), F32),
               pltpu.VMEM((n_lt, 2 * tq, tq), F32),
               pltpu.VMEM((n_lt, 2 * tq, tq), F32),
               pltpu.VMEM((n_lt, 2 * tq, tq), BF16)]
    scratch_bytes = n_lt * (_nbytes((2 * tq, LANES), BF16) + 2 * _nbytes((2 * tq, LANES), F32)
                            + 2 * _nbytes((2 * tq, tq), F32) + _nbytes((2 * tq, tq), BF16))
    return pl.pallas_call(
        _sb_attn_kernel,
        grid=(b, d // width, s // tq),
        in_specs=[pl.BlockSpec(memory_space=pltpu.SMEM), qspec, kspec, kspec,
                  pl.BlockSpec((tq, tq), lambda i, p, j: (0, 0))],
        out_specs=qspec,
        out_shape=jax.ShapeDtypeStruct((b, s, d), BF16),
        scratch_shapes=scratch,
        compiler_params=pltpu.CompilerParams(
            dimension_semantics=("parallel", "parallel", "arbitrary"),
            vmem_limit_bytes=_vmem_limit(block_bytes, scratch_bytes, 12 * _nbytes((2 * tq, tq), F32))),
        name="sb_attn",
    )(z_max.reshape(1).astype(F32), q, k, v, suffix)


def _outproj_mlp_kernel(h_ref, z_ref, wo_ref, g_ref, wup_ref, wdn_ref, o_ref, xn_ref):
    @pl.when(pl.program_id(1) == 0)
    def _():
        h1 = h_ref[...] + _dot(z_ref[...], wo_ref[...])
        o_ref[...] = h1
        xn_ref[...] = _rms(h1, g_ref[...]).astype(BF16)

    act = jnp.square(jnp.maximum(_dot(xn_ref[...], wup_ref[...]), 0.0)).astype(BF16)
    o_ref[...] += _dot(act, wdn_ref[...])


def _outproj_mlp(h2, z, w_o, g, w_up_all, w_dn_all, layer):
    t, d = h2.shape
    f = w_up_all.shape[2]
    tm = min(MLP_TM, t)
    tf = min(MLP_TF, f)
    tok = pl.BlockSpec((tm, d), lambda i, j: (i, 0))
    in_specs = [
        tok, tok,
        pl.BlockSpec((d, d), lambda i, j: (0, 0)),
        pl.BlockSpec((1, d), lambda i, j: (0, 0)),
        pl.BlockSpec((None, d, tf), lambda i, j: (layer, 0, j)),
        pl.BlockSpec((None, tf, d), lambda i, j: (layer, j, 0)),
    ]
    tok_bytes = _nbytes((tm, d), F32)
    block_bytes = 2 * tok_bytes + _nbytes((tm, d), BF16) + _nbytes((d, d), BF16) + 2 * _nbytes((d, tf), BF16)
    return pl.pallas_call(
        _outproj_mlp_kernel,
        grid=(t // tm, f // tf),
        in_specs=in_specs,
        out_specs=tok,
        out_shape=jax.ShapeDtypeStruct((t, d), F32),
        scratch_shapes=[pltpu.VMEM((tm, d), BF16)],
        compiler_params=pltpu.CompilerParams(
            dimension_semantics=("parallel", "arbitrary"),
            vmem_limit_bytes=_vmem_limit(block_bytes, _nbytes((tm, d), BF16),
                                         tok_bytes + 2 * _nbytes((tm, tf), F32))),
        name="outproj_mlp",
    )(h2, z, w_o, g, w_up_all, w_dn_all)


def _pad_cols(w, mult):
    pad = (-w.shape[1]) % mult
    return jnp.pad(w, ((0, 0), (0, pad)))


def _pad_rows(w, mult):
    pad = (-w.shape[0]) % mult
    return jnp.pad(w, ((0, pad), (0, 0)))


def _rows8(*rows):
    d = rows[0].shape[-1]
    out = jnp.zeros((SUBLANES, d), F32)
    return out.at[:len(rows)].set(jnp.stack([r.reshape(d) for r in rows]))


def _lora(w_in, w_out):
    return _pad_cols(w_in, LANES).astype(BF16), _pad_rows(w_out, LANES).astype(BF16)


def kernel(x, mix_norm, mlp_norm, mlp_up, mlp_down, rwkv_mu, rwkv_w_r, rwkv_w_k, rwkv_w_v, rwkv_w_o, rwkv_decay_w0, rwkv_decay_w1, rwkv_decay_w2, rwkv_iclr_a0, rwkv_iclr_a1, rwkv_iclr_a2, rwkv_gate_g1, rwkv_gate_g2, rwkv_k_k, rwkv_k_a, rwkv_r_k, rwkv_lnx_w, rwkv_lnx_b, rwkv_vres_v0, rwkv_vres_v1, rwkv_vres_v2, conv_w_in, conv_w, conv_w_out, sb_w_qkv, sb_q_norm, sb_k_norm, sb_w_o):
    b, s, d = x.shape
    depth = mix_norm.shape[0]
    n_heads = d // HEAD_DIM
    mlp_up_bf = mlp_up.astype(BF16)
    mlp_down_bf = mlp_down.astype(BF16)
    h = x
    v_first = None
    for i in range(depth):
        kind = i % 3
        j = i // 3
        g_mix = mix_norm[i].reshape(1, d)
        if kind == 0:
            vres = None
            v0 = jnp.zeros((d,), F32)
            if j > 0:
                v1, v2 = _lora(rwkv_vres_v1[j - 1], rwkv_vres_v2[j - 1])
                vres = (v1, v2, v_first)
                v0 = rwkv_vres_v0[j - 1]
            vecs = _rows8(mix_norm[i], rwkv_decay_w0[j], rwkv_iclr_a0[j], v0)
            mu8 = _rows8(*[rwkv_mu[j, m] for m in range(rwkv_mu.shape[1])])
            w1, w2 = _lora(rwkv_decay_w1[j], rwkv_decay_w2[j])
            a1, a2 = _lora(rwkv_iclr_a1[j], rwkv_iclr_a2[j])
            g1, g2 = _lora(rwkv_gate_g1[j], rwkv_gate_g2[j])
            r, k, v, lw, a, gate = _rwkv_in(
                h, vecs, mu8, rwkv_w_r[j].astype(BF16), rwkv_w_k[j].astype(BF16),
                rwkv_w_v[j].astype(BF16), w1, w2, a1, a2, g1, g2, vres)
            if j == 0:
                v_first = v
            pvec = _rows8(rwkv_k_k[j], rwkv_k_a[j], rwkv_r_k[j].reshape(d), rwkv_lnx_w[j], rwkv_lnx_b[j])
            z = _wkv(r, lw, k, v, a, gate, pvec, WKV_PASSES)
            w_o = rwkv_w_o[j]
        elif kind == 1:
            z = _conv_in(h, g_mix, conv_w_in[j].astype(BF16), _rows8(*[conv_w[j, m] for m in range(3)]))
            w_o = conv_w_out[j]
        else:
            qg = jnp.tile(sb_q_norm[j], n_heads).reshape(1, d)
            kg = jnp.tile(sb_k_norm[j], n_heads).reshape(1, d)
            q, k, v = _sb_qkv(h.reshape(b * s, d), g_mix, sb_w_qkv[j].astype(BF16), qg, kg)
            z_max = (SB_ZMAX_SLACK * HEAD_DIM * SB_SCALE
                     * jnp.max(jnp.abs(sb_q_norm[j])) * jnp.max(jnp.abs(sb_k_norm[j])))
            z = _sb_attn(q.reshape(b, s, d), k.reshape(b, s, d), v.reshape(b, s, d), z_max)
            w_o = sb_w_o[j]
        h = _outproj_mlp(
            h.reshape(b * s, d), z.reshape(b * s, d), w_o.astype(BF16), mlp_norm[i].reshape(1, d), mlp_up_bf, mlp_down_bf, i,
        ).reshape(b, s, d)
    return h
```

```python
import functools
import math

import jax
import jax.numpy as jnp
from jax import lax
from jax.experimental import pallas as pl
from jax.experimental.pallas import tpu as pltpu

F32 = jnp.float32
BF16 = jnp.bfloat16

HEAD_DIM = 64
HEAD_SHIFT = 6
LANES = 128
SUBLANES = 8
RMS_EPS = 1e-6
GN_EPS = 64e-5
KK_EPS = 1e-24
SB_SCALE = HEAD_DIM ** -0.5
SB_CUM_PIECES = 1
SB_EXP_ZERO = 105.0
SB_ZMAX_SLACK = 1.02

V7X_VMEM_BYTES = 64 * 1024 * 1024
VMEM_LIMIT_CAP = V7X_VMEM_BYTES - 8 * 1024 * 1024

MLP_TM = 1024
MLP_TF = 1024
RWKV_TS = 256
CONV_TS = 512
QKV_TM = 512
QKV_SUM_PIECES = 1
WKV_CHUNK = 64
WKV_ROWS = 8
WKV_GROUP_ROWS = 2
WKV_PASSES = 1
WKV_CUM_PIECES = 2
WKV_SUM_PIECES = 1
SB_TQ = 256
SB_ROWS = 256
SB_LANE_TILES = 8


def _vmem_limit(block_bytes, scratch_bytes, temp_bytes):
    return int(min(2 * block_bytes + scratch_bytes + temp_bytes, VMEM_LIMIT_CAP))


def _nbytes(shape, dtype):
    n = 1
    for s in shape:
        n *= s
    return n * jnp.dtype(dtype).itemsize


def _dot(a, b):
    return jnp.dot(a, b, preferred_element_type=F32)


def _dot_nt(a, b):
    return lax.dot_general(a, b, (((1,), (1,)), ((), ())), preferred_element_type=F32)


def _dot_tn(a, b):
    return lax.dot_general(a, b, (((0,), (0,)), ((), ())), preferred_element_type=F32)


def _split(x, pieces):
    out = []
    for _ in range(pieces - 1):
        hi = x.astype(BF16)
        out.append(hi)
        x = x - hi.astype(F32)
    out.append(x.astype(BF16))
    return out


def _dot_exact_rhs(x, m_bf16, pieces, dot=_dot):
    acc = None
    for p in _split(x, pieces):
        t = dot(p, m_bf16)
        acc = t if acc is None else acc + t
    return acc


def _dot_exact_lhs(m_bf16, x, pieces):
    acc = None
    for p in _split(x, pieces):
        t = _dot(m_bf16, p)
        acc = t if acc is None else acc + t
    return acc


def _mm(a, b, passes, dot=_dot):
    if passes == 1:
        return dot(a.astype(BF16), b.astype(BF16))
    ah, al = _split(a, 2)
    bh, bl = _split(b, 2)
    return dot(ah, bh) + (dot(al, bh) + dot(ah, bl))


def _rms(x, g):
    ms = jnp.mean(x * x, axis=-1, keepdims=True)
    return x * lax.rsqrt(ms + RMS_EPS) * g


def _sigmoid(x):
    return 1.0 / (1.0 + jnp.exp(-x))


def _softplus(x):
    return jnp.maximum(x, 0.0) + jnp.log(1.0 + jnp.exp(-jnp.abs(x)))


def _head_sum_matrix(n):
    r = lax.broadcasted_iota(jnp.int32, (n, n), 0) >> HEAD_SHIFT
    c = lax.broadcasted_iota(jnp.int32, (n, n), 1) >> HEAD_SHIFT
    return jnp.where(r == c, 1.0, 0.0).astype(BF16)


def _shift_rows(x, carry8, n):
    rows = lax.broadcasted_iota(jnp.int32, (x.shape[0], 1), 0)
    out = pltpu.roll(x, n, axis=0)
    for i in range(n):
        out = jnp.where(rows == i, carry8[SUBLANES - n + i:SUBLANES - n + i + 1, :], out)
    return out


def _rwkv_in_kernel(*refs, has_vres):
    if has_vres:
        (h_ref, vec_ref, mu_ref, wr_ref, wk_ref, wv_ref, w1_ref, w2_ref, a1_ref, a2_ref,
         g1_ref, g2_ref, v1_ref, v2_ref, vf_ref,
         r_out, k_out, v_out, lw_out, a_out, g_out, carry_ref) = refs
    else:
        (h_ref, vec_ref, mu_ref, wr_ref, wk_ref, wv_ref, w1_ref, w2_ref, a1_ref, a2_ref,
         g1_ref, g2_ref,
         r_out, k_out, v_out, lw_out, a_out, g_out, carry_ref) = refs

    @pl.when(pl.program_id(1) == 0)
    def _():
        carry_ref[...] = jnp.zeros_like(carry_ref)

    ts = h_ref.shape[0]
    u = _rms(h_ref[...], vec_ref[0:1, :])
    prev = _shift_rows(u, carry_ref[...], 1)
    carry_ref[...] = u[ts - SUBLANES:, :]
    xx = prev - u

    def mix(i):
        return (u + xx * mu_ref[i:i + 1, :]).astype(BF16)

    hw = jnp.tanh(_dot(mix(1), w1_ref[...])).astype(BF16)
    ha = _dot(mix(4), a1_ref[...]).astype(BF16)
    hg = _sigmoid(_dot(mix(5), g1_ref[...])).astype(BF16)
    xv = mix(3)
    if has_vres:
        hv = _dot(xv, v1_ref[...]).astype(BF16)
    dw = vec_ref[1:2, :] + _dot(hw, w2_ref[...])
    da = vec_ref[2:3, :] + _dot(ha, a2_ref[...])
    g_out[...] = _dot(hg, g2_ref[...])
    if has_vres:
        dv = vec_ref[3:4, :] + _dot(hv, v2_ref[...])

    r_out[...] = _dot(mix(0), wr_ref[...])
    lw_out[...] = -jnp.exp(-_softplus(-dw) - 0.5)
    k_out[...] = _dot(mix(2), wk_ref[...])
    a_out[...] = _sigmoid(da)
    v = _dot(xv, wv_ref[...])
    if has_vres:
        v = v + (vf_ref[...] - v) * _sigmoid(dv)
    v_out[...] = v


def _rwkv_in(h, vecs, mu8, wr, wk, wv, w1, w2, a1, a2, g1, g2, vres):
    b, s, d = h.shape
    ts = min(RWKV_TS, s)
    tok = pl.BlockSpec((None, ts, d), lambda i, j: (i, j, 0))

    def full(x):
        return pl.BlockSpec(x.shape, lambda i, j: (0,) * x.ndim)

    weights = [vecs, mu8, wr, wk, wv, w1, w2, a1, a2, g1, g2]
    args = [h] + weights
    in_specs = [tok] + [full(w) for w in weights]
    if vres is not None:
        v1, v2, v_first = vres
        args += [v1, v2, v_first]
        in_specs += [full(v1), full(v2), tok]
    tok_bytes = _nbytes((ts, d), F32)
    block_bytes = sum(_nbytes(w.shape, w.dtype) for w in args[1:1 + len(weights) + (2 if vres else 0)])
    block_bytes += tok_bytes * (7 + (1 if vres else 0))
    out = jax.ShapeDtypeStruct((b, s, d), F32)
    return pl.pallas_call(
        functools.partial(_rwkv_in_kernel, has_vres=vres is not None),
        grid=(b, s // ts),
        in_specs=in_specs,
        out_specs=[tok] * 6,
        out_shape=[out] * 6,
        scratch_shapes=[pltpu.VMEM((SUBLANES, d), F32)],
        compiler_params=pltpu.CompilerParams(
            dimension_semantics=("parallel", "arbitrary"),
            vmem_limit_bytes=_vmem_limit(block_bytes, _nbytes((SUBLANES, d), F32), 12 * tok_bytes)),
        name="rwkv_in",
    )(*args)


def _wkv_kernel(r_ref, lw_ref, k_ref, v_ref, a_ref, g_ref, pv_ref, y_ref, s_ref, *, passes):
    n_rows, c, d_model = r_ref.shape
    n = 2 * c
    n_tiles = d_model // LANES

    @pl.when(pl.program_id(1) == 0)
    def _():
        s_ref[...] = jnp.zeros_like(s_ref)

    first_head = lax.broadcasted_iota(jnp.int32, (c, LANES), 1) < HEAD_DIM

    def head_sums(x):
        lo = jnp.sum(jnp.where(first_head, x, 0.0), axis=-1, keepdims=True)
        hi = jnp.sum(jnp.where(first_head, 0.0, x), axis=-1, keepdims=True)
        return jnp.where(first_head, lo, hi)

    ti = lax.broadcasted_iota(jnp.int32, (c, c), 0)
    si = lax.broadcasted_iota(jnp.int32, (c, c), 1)
    lincl = jnp.where(si <= ti, 1.0, 0.0).astype(BF16)
    log2c = c.bit_length() - 1
    row = lax.broadcasted_iota(jnp.int32, (n, LANES), 0)
    lane = lax.broadcasted_iota(jnp.int32, (n, LANES), 1)
    own = (row >> log2c) == (lane >> HEAD_SHIFT)
    ri = lax.broadcasted_iota(jnp.int32, (n, 2 * n), 0)
    ci = lax.broadcasted_iota(jnp.int32, (n, 2 * n), 1) & (c - 1)
    tri = ci < (ri & (c - 1)) + (ri >> log2c)
    inv_n = 1.0 / HEAD_DIM
    mm = functools.partial(_mm, passes=passes)
    mm_nt = functools.partial(_mm, passes=passes, dot=_dot_nt)
    mm_tn = functools.partial(_mm, passes=passes, dot=_dot_tn)

    def stack(x):
        return jnp.where(own, jnp.concatenate([x, x], axis=0), 0.0)

    steps = log2c
    rows_per_group = WKV_GROUP_ROWS if n_rows % WKV_GROUP_ROWS == 0 else 1
    groups = [[(bi, slice(t * LANES, (t + 1) * LANES), t)
               for bi in range(g0, g0 + rows_per_group) for t in range(n_tiles)]
              for g0 in range(0, n_rows, rows_per_group)]

    def each(f, *cols):
        return [f(*args) for args in zip(*cols)]

    def pro_keys(units):
        st = {"units": units}
        for name, ref in (("r", r_ref), ("lw", lw_ref), ("k_raw", k_ref), ("v", v_ref), ("a", a_ref)):
            st[name] = [ref[bi, :, sl] for bi, sl, _ in units]

        def param(i):
            return [pv_ref[i:i + 1, sl] for _, sl, _ in units]

        kk = each(lambda x, g: x * g, st["k_raw"], param(0))
        kk_ss = each(lambda x: head_sums(x * x), kk)
        st["kk"] = each(lambda x, ss: x * lax.rsqrt(jnp.maximum(ss, KK_EPS)), kk, kk_ss)
        st["k"] = each(lambda x, ai, g: x * (1.0 + (ai - 1.0) * g), st["k_raw"], st["a"], param(1))
        st["bonus"] = each(lambda ri_, ki, g, vi: head_sums(ri_ * ki * g) * vi, st["r"], st["k"], param(2), st["v"])
        return st

    def pro_cum(st):
        st["cum"] = each(lambda x: _dot_exact_lhs(lincl, x, WKV_CUM_PIECES), st["lw"])
        return st

    def pro_scale(st):
        g_incl = each(jnp.exp, st["cum"])
        g_excl = each(lambda cs, x: jnp.exp(cs - x), st["cum"], st["lw"])
        g_inv = each(lambda cs: jnp.exp(-cs), st["cum"])
        st["g_end"] = each(lambda g: g[c - 1:c, :], g_incl)
        st["ar"] = each(lambda kki, ri_, ge, gi: jnp.concatenate([-kki * ge, ri_ * gi], axis=0),
                        st["kk"], st["r"], g_excl, g_incl)
        st["bt"] = each(lambda kki, ai, gv: kki * ai * gv, st["kk"], st["a"], g_inv)
        st["kt"] = each(lambda ki, gv: ki * gv, st["k"], g_inv)
        return st

    def pro_stack(st):
        st["bks"] = each(lambda b_, k_: jnp.concatenate([stack(b_), stack(k_)], axis=0), st["bt"], st["kt"])
        st["vs"] = each(stack, st["v"])
        return st

    def pro_products(st):
        st["s0"] = [s_ref[bi, t] for bi, _, t in st["units"]]
        st["prod"] = each(lambda x, y: jnp.where(tri, mm_nt(x, y), 0.0), st["ar"], st["bks"])
        st["from_s0"] = each(mm_nt, st["ar"], st["s0"])
        return st

    pro_pieces = [pro_keys, pro_cum, pro_scale, pro_stack, pro_products]
    assert len(pro_pieces) <= steps

    def solve_start(st):
        x = each(lambda fs, pr, vs: fs[:c, :] + mm(pr[:c, n:], vs), st["from_s0"], st["prod"], st["vs"])
        return x, [pr[:c, :n] for pr in st["prod"]]

    def solve_stage(x, p, last):
        if last:
            return each(lambda xi, pi: xi + mm(pi, stack(xi)), x, p), None
        px = each(lambda xi, pi: mm(pi, jnp.concatenate([stack(xi), stack(pi)], axis=1)), x, p)
        return each(lambda xi, pxi: xi + pxi[:, :LANES], x, px), each(lambda pxi: pxi[:, LANES:], px)

    def finish(st, x):
        st["y"] = each(lambda fs, pr, vs, u: fs[c:, :] + mm(pr[c:, :], jnp.concatenate([stack(u), vs], axis=0)),
                       st["from_s0"], st["prod"], st["vs"], x)
        outer = each(lambda u, v, bt, kt, ge: mm_tn(jnp.concatenate([u, v], axis=0),
                                                    jnp.concatenate([bt * ge, kt * ge], axis=0)),
                     x, st["v"], st["bt"], st["kt"], st["g_end"])
        for (bi, _, t), s0, ge, o in zip(st["units"], st["s0"], st["g_end"], outer):
            s_ref[bi, t] = s0 * ge + jnp.where(own, o, 0.0)

    def epilogue(st, lo, hi):
        y = st["y"][lo:hi]
        mean = each(lambda yi: head_sums(yi) * inv_n, y)
        yc = each(lambda yi, m: yi - m, y, mean)
        var = each(lambda yi: head_sums(yi * yi) * inv_n, yc)
        for (bi, sl, _), bonus, yi, vr in zip(st["units"][lo:hi], st["bonus"][lo:hi], yc, var):
            out = yi * lax.rsqrt(vr + GN_EPS) * pv_ref[3:4, sl] + pv_ref[4:5, sl] + bonus
            y_ref[bi, :, sl] = (out * g_ref[bi, :, sl]).astype(BF16)

    cur = groups[0]
    for piece in pro_pieces:
        cur = piece(cur)
    prev = None
    for gi in range(len(groups)):
        x, p = solve_start(cur)
        nxt = groups[gi + 1] if gi + 1 < len(groups) else None
        n_units = len(groups[gi])
        for i in range(steps):
            x, p = solve_stage(x, p, i == steps - 1)
            if nxt is not None and i < len(pro_pieces):
                nxt = pro_pieces[i](nxt)
            if prev is not None:
                epilogue(prev, i * n_units // steps, (i + 1) * n_units // steps)
        finish(cur, x)
        prev, cur = cur, nxt
    epilogue(prev, 0, len(prev["units"]))


def _wkv(r, lw, k, v, a, g, pvec, passes):
    b, s, d = r.shape
    c = WKV_CHUNK
    rows = math.gcd(b, WKV_ROWS)
    tok = pl.BlockSpec((rows, c, d), lambda i, j: (i, j, 0))
    par = pl.BlockSpec((SUBLANES, d), lambda i, j: (0, 0))
    tok_bytes = _nbytes((rows, c, d), F32)
    sq_bytes = _nbytes((2 * c, 2 * c), F32)
    n_tiles = d // LANES
    return pl.pallas_call(
        functools.partial(_wkv_kernel, passes=passes),
        grid=(b // rows, s // c),
        in_specs=[tok] * 6 + [par],
        out_specs=tok,
        out_shape=jax.ShapeDtypeStruct((b, s, d), BF16),
        scratch_shapes=[pltpu.VMEM((rows, n_tiles, LANES, LANES), F32)],
        compiler_params=pltpu.CompilerParams(
            dimension_semantics=("parallel", "arbitrary"),
            vmem_limit_bytes=_vmem_limit(8 * tok_bytes, rows * n_tiles * sq_bytes, rows * n_tiles * 48 * sq_bytes)),
        name="wkv",
    )(r, lw, k, v, a, g, pvec)


def _conv_in_kernel(h_ref, g_ref, win_ref, cw_ref, z_out, carry_ref):
    @pl.when(pl.program_id(1) == 0)
    def _():
        carry_ref[...] = jnp.zeros_like(carry_ref)

    ts, d = h_ref.shape
    u = _rms(h_ref[...], g_ref[...]).astype(BF16)
    z = _dot(u, win_ref[:, d:2 * d]) * _dot(u, win_ref[:, 2 * d:])
    b_gate = _dot(u, win_ref[:, :d])
    carry = carry_ref[...]
    z1 = _shift_rows(z, carry, 1)
    z2 = _shift_rows(z, carry, 2)
    carry_ref[...] = z[ts - SUBLANES:, :]
    zc = z2 * cw_ref[0:1, :] + z1 * cw_ref[1:2, :] + z * cw_ref[2:3, :]
    z_out[...] = (b_gate * zc).astype(BF16)


def _conv_in(h, g, w_in, cw8):
    b, s, d = h.shape
    ts = min(CONV_TS, s)
    tok = pl.BlockSpec((None, ts, d), lambda i, j: (i, j, 0))
    tok_bytes = _nbytes((ts, d), F32)
    block_bytes = 2 * tok_bytes + _nbytes(w_in.shape, BF16) + _nbytes(cw8.shape, F32)
    return pl.pallas_call(
        _conv_in_kernel,
        grid=(b, s // ts),
        in_specs=[tok,
                  pl.BlockSpec(g.shape, lambda i, j: (0, 0)),
                  pl.BlockSpec(w_in.shape, lambda i, j: (0, 0)),
                  pl.BlockSpec(cw8.shape, lambda i, j: (0, 0))],
        out_specs=tok,
        out_shape=jax.ShapeDtypeStruct((b, s, d), BF16),
        scratch_shapes=[pltpu.VMEM((SUBLANES, d), F32)],
        compiler_params=pltpu.CompilerParams(
            dimension_semantics=("parallel", "arbitrary"),
            vmem_limit_bytes=_vmem_limit(block_bytes, _nbytes((SUBLANES, d), F32), 10 * tok_bytes)),
        name="conv_in",
    )(h, g, w_in, cw8)


def _sb_qkv_kernel(h_ref, g_ref, w_ref, qg_ref, kg_ref, q_out, k_out, v_out):
    d = h_ref.shape[1]
    u = _rms(h_ref[...], g_ref[...]).astype(BF16)
    hsum = _head_sum_matrix(2 * LANES)
    inv_n = 1.0 / HEAD_DIM

    def head_norm(x, gain):
        cols = []
        for j in range(0, d, 2 * LANES):
            xs = x[:, j:j + 2 * LANES]
            ms = _dot_exact_rhs(xs * xs, hsum, QKV_SUM_PIECES) * inv_n
            cols.append(xs * lax.rsqrt(ms + RMS_EPS))
        return jnp.concatenate(cols, axis=1) * gain

    q = _dot(u, w_ref[:, :d])
    k = _dot(u, w_ref[:, d:2 * d])
    q_out[...] = head_norm(q, qg_ref[...]).astype(BF16)
    v = _dot(u, w_ref[:, 2 * d:])
    k_out[...] = head_norm(k, kg_ref[...]).astype(BF16)
    v_out[...] = v.astype(BF16)


def _sb_qkv(h2, g, w_qkv, qg, kg):
    t, d = h2.shape
    tm = min(QKV_TM, t)
    tok = pl.BlockSpec((tm, d), lambda i: (i, 0))
    vec = pl.BlockSpec((1, d), lambda i: (0, 0))
    tok_bytes = _nbytes((tm, d), F32)
    block_bytes = 4 * tok_bytes + _nbytes(w_qkv.shape, BF16) + 3 * _nbytes((1, d), F32)
    out = jax.ShapeDtypeStruct((t, d), BF16)
    return pl.pallas_call(
        _sb_qkv_kernel,
        grid=(t // tm,),
        in_specs=[tok, vec, pl.BlockSpec(w_qkv.shape, lambda i: (0, 0)), vec, vec],
        out_specs=[tok] * 3,
        out_shape=[out] * 3,
        compiler_params=pltpu.CompilerParams(
            dimension_semantics=("parallel",),
            vmem_limit_bytes=_vmem_limit(block_bytes, 0, 10 * tok_bytes)),
        name="sb_qkv",
    )(h2, g, w_qkv, qg, kg)


def _sb_attn_kernel(zmax_ref, q_ref, k_ref, v_ref, suffix_ref, o_ref,
                    qs_ref, acc_ref, run_ref, za_ref, zb_ref, att_ref):
    tq = q_ref.shape[0]
    n_lt = q_ref.shape[1] // LANES
    qb = pl.program_id(2)
    rows = lax.broadcasted_iota(jnp.int32, (2 * tq, LANES), 0)
    lanes = lax.broadcasted_iota(jnp.int32, (2 * tq, LANES), 1)
    own = (rows >> (tq.bit_length() - 1)) == (lanes >> HEAD_SHIFT)
    lts = [slice(p * LANES, (p + 1) * LANES) for p in range(n_lt)]
    for p, sl in enumerate(lts):
        q = q_ref[:, sl].astype(F32)
        qs_ref[p] = (jnp.where(own, jnp.concatenate([q, q], axis=0), 0.0) * SB_SCALE).astype(BF16)
    suffix = suffix_ref[...]

    groups = [slice(g * SB_ROWS, (g + 1) * SB_ROWS) for g in range(2 * tq // SB_ROWS)]
    units = [(p, g) for p in range(n_lt) for g in groups]

    def each(f, *cols):
        return [f(*args) for args in zip(*cols)]

    def key_tile(ref, j, p):
        return ref[pl.ds(pl.multiple_of(j * tq, tq), tq), lts[p]]

    def scores(j, z_out):
        for p in range(n_lt):
            ks = key_tile(k_ref, j, p)
            for g in groups:
                z_out[p, g, :] = _dot_nt(qs_ref[p, g, :], ks)

    def add_values(j):
        pv = []
        for p in range(n_lt):
            vs = key_tile(v_ref, j, p)
            pv += [_dot(att_ref[p, g, :], vs) for g in groups]
        return pv

    def step(j, z_in, z_out, diagonal):
        if not diagonal:
            pv = add_values(j + 1)
        sp = [_softplus(z_in[p, g, :]) for p, g in units]
        if diagonal:
            t_loc = lax.broadcasted_iota(jnp.int32, (SB_ROWS, tq), 0)
            s_loc = lax.broadcasted_iota(jnp.int32, (SB_ROWS, tq), 1)
            causal = [s_loc < t_loc + (g.start & (tq - 1)) for _, g in units]
            sp = each(lambda m, x: jnp.where(m, x, 0.0), causal, sp)
        rev_cum = each(lambda x: _dot_exact_rhs(x, suffix, SB_CUM_PIECES), sp)
        if z_out is not None:
            scores(jnp.maximum(j - 1, 0), z_out)
        for i, (p, g) in enumerate(units):
            rc = rev_cum[i]
            total = jnp.broadcast_to(rc[:, 0:1], (SB_ROWS, LANES))
            if diagonal:
                att = jnp.where(causal[i], jnp.exp(z_in[p, g, :] - rc), 0.0)
                run_ref[p, g, :] = total
                acc_ref[p, g, :] = jnp.zeros((SB_ROWS, LANES), F32)
            else:
                run = run_ref[p, g, :]
                att = jnp.exp(z_in[p, g, :] - rc - jnp.concatenate([run] * (tq // LANES), axis=1))
                run_ref[p, g, :] = run + total
                acc_ref[p, g, :] += pv[i]
            att_ref[p, g, :] = att.astype(BF16)

    stop_at = zmax_ref[0] + SB_EXP_ZERO

    def run_min():
        return jnp.min(run_ref[...])

    scores(qb, za_ref)
    step(qb, za_ref, zb_ref, True)

    @pl.when(qb >= 1)
    def _():
        step(qb - 1, zb_ref, None, False)

    def more(state):
        j, low = state
        return jnp.logical_and(j >= 1, low < stop_at)

    def pair(state):
        j, _ = state
        scores(j, za_ref)
        step(j, za_ref, zb_ref, False)
        step(j - 1, zb_ref, None, False)
        return j - 2, run_min()

    j, low = lax.while_loop(more, pair, (jnp.maximum(qb - 2, -1), run_min()))
    last_tile = jnp.logical_and(j == 0, low < stop_at)

    @pl.when(last_tile)
    def _():
        scores(0, za_ref)
        step(0, za_ref, None, False)

    pv = add_values(jnp.where(last_tile, 0, j + 1))
    for (p, g), pvi in zip(units, pv):
        acc_ref[p, g, :] += pvi
    lane = lax.broadcasted_iota(jnp.int32, (tq, LANES), 1)
    for p, sl in enumerate(lts):
        o_ref[:, sl] = jnp.where(lane >= HEAD_DIM, acc_ref[p, tq:, :], acc_ref[p, :tq, :]).astype(BF16)


def _sb_attn(q, k, v, z_max):
    b, s, d = q.shape
    tq = SB_TQ
    n_lt = min(SB_LANE_TILES, d // LANES)
    width = n_lt * LANES
    ri = lax.broadcasted_iota(jnp.int32, (tq, tq), 0)
    ci = lax.broadcasted_iota(jnp.int32, (tq, tq), 1)
    suffix = jnp.where(ri >= ci, 1.0, 0.0).astype(BF16)
    qspec = pl.BlockSpec((None, tq, width), lambda i, p, j: (i, j, p))
    kspec = pl.BlockSpec((None, s, width), lambda i, p, j: (i, 0, p))
    block_bytes = 2 * _nbytes((tq, width), BF16) + 2 * _nbytes((s, width), BF16) + _nbytes((tq, tq), BF16)
    scratch = [pltpu.VMEM((n_lt, 2 * tq, LANES), BF16),
               pltpu.VMEM((n_lt, 2 * tq, LANES), F32),
               pltpu.VMEM((n_lt, 2 * tq, LANES), F32),
               pltpu.VMEM((n_lt, 2 * tq, tq), F32),
               pltpu.VMEM((n_lt, 2 * tq, tq), F32),
               pltpu.VMEM((n_lt, 2 * tq, tq), BF16)]
    scratch_bytes = n_lt * (_nbytes((2 * tq, LANES), BF16) + 2 * _nbytes((2 * tq, LANES), F32)
                            + 2 * _nbytes((2 * tq, tq), F32) + _nbytes((2 * tq, tq), BF16))
    return pl.pallas_call(
        _sb_attn_kernel,
        grid=(b, d // width, s // tq),
        in_specs=[pl.BlockSpec(memory_space=pltpu.SMEM), qspec, kspec, kspec,
                  pl.BlockSpec((tq, tq), lambda i, p, j: (0, 0))],
        out_specs=qspec,
        out_shape=jax.ShapeDtypeStruct((b, s, d), BF16),
        scratch_shapes=scratch,
        compiler_params=pltpu.CompilerParams(
            dimension_semantics=("parallel", "parallel", "arbitrary"),
            vmem_limit_bytes=_vmem_limit(block_bytes, scratch_bytes, 12 * _nbytes((2 * tq, tq), F32))),
        name="sb_attn",
    )(z_max.reshape(1).astype(F32), q, k, v, suffix)


def _outproj_mlp_kernel(h_ref, z_ref, wo_ref, g_ref, wup_ref, wdn_ref, o_ref, xn_ref):
    @pl.when(pl.program_id(1) == 0)
    def _():
        h1 = h_ref[...] + _dot(z_ref[...], wo_ref[...])
        o_ref[...] = h1
        xn_ref[...] = _rms(h1, g_ref[...]).astype(BF16)

    act = jnp.square(jnp.maximum(_dot(xn_ref[...], wup_ref[...]), 0.0)).astype(BF16)
    o_ref[...] += _dot(act, wdn_ref[...])


def _outproj_mlp(h2, z, w_o, g, w_up_all, w_dn_all, layer):
    t, d = h2.shape
    f = w_up_all.shape[2]
    tm = min(MLP_TM, t)
    tf = min(MLP_TF, f)
    tok = pl.BlockSpec((tm, d), lambda i, j: (i, 0))
    in_specs = [
        tok, tok,
        pl.BlockSpec((d, d), lambda i, j: (0, 0)),
        pl.BlockSpec((1, d), lambda i, j: (0, 0)),
        pl.BlockSpec((None, d, tf), lambda i, j: (layer, 0, j)),
        pl.BlockSpec((None, tf, d), lambda i, j: (layer, j, 0)),
    ]
    tok_bytes = _nbytes((tm, d), F32)
    block_bytes = 2 * tok_bytes + _nbytes((tm, d), BF16) + _nbytes((d, d), BF16) + 2 * _nbytes((d, tf), BF16)
    return pl.pallas_call(
        _outproj_mlp_kernel,
        grid=(t // tm, f // tf),
        in_specs=in_specs,
        out_specs=tok,
        out_shape=jax.ShapeDtypeStruct((t, d), F32),
        scratch_shapes=[pltpu.VMEM((tm, d), BF16)],
        compiler_params=pltpu.CompilerParams(
            dimension_semantics=("parallel", "arbitrary"),
            vmem_limit_bytes=_vmem_limit(block_bytes, _nbytes((tm, d), BF16),
                                         tok_bytes + 2 * _nbytes((tm, tf), F32))),
        name="outproj_mlp",
    )(h2, z, w_o, g, w_up_all, w_dn_all)


def _pad_cols(w, mult):
    pad = (-w.shape[1]) % mult
    return jnp.pad(w, ((0, 0), (0, pad)))


def _pad_rows(w, mult):
    pad = (-w.shape[0]) % mult
    return jnp.pad(w, ((0, pad), (0, 0)))


def _rows8(*rows):
    d = rows[0].shape[-1]
    out = jnp.zeros((SUBLANES, d), F32)
    return out.at[:len(rows)].set(jnp.stack([r.reshape(d) for r in rows]))


def _lora(w_in, w_out):
    return _pad_cols(w_in, LANES).astype(BF16), _pad_rows(w_out, LANES).astype(BF16)


def kernel(x, mix_norm, mlp_norm, mlp_up, mlp_down, rwkv_mu, rwkv_w_r, rwkv_w_k, rwkv_w_v, rwkv_w_o, rwkv_decay_w0, rwkv_decay_w1, rwkv_decay_w2, rwkv_iclr_a0, rwkv_iclr_a1, rwkv_iclr_a2, rwkv_gate_g1, rwkv_gate_g2, rwkv_k_k, rwkv_k_a, rwkv_r_k, rwkv_lnx_w, rwkv_lnx_b, rwkv_vres_v0, rwkv_vres_v1, rwkv_vres_v2, conv_w_in, conv_w, conv_w_out, sb_w_qkv, sb_q_norm, sb_k_norm, sb_w_o):
    b, s, d = x.shape
    depth = mix_norm.shape[0]
    n_heads = d // HEAD_DIM
    mlp_up_bf = mlp_up.astype(BF16)
    mlp_down_bf = mlp_down.astype(BF16)
    h = x
    v_first = None
    for i in range(depth):
        kind = i % 3
        j = i // 3
        g_mix = mix_norm[i].reshape(1, d)
        if kind == 0:
            vres = None
            v0 = jnp.zeros((d,), F32)
            if j > 0:
                v1, v2 = _lora(rwkv_vres_v1[j - 1], rwkv_vres_v2[j - 1])
                vres = (v1, v2, v_first)
                v0 = rwkv_vres_v0[j - 1]
            vecs = _rows8(mix_norm[i], rwkv_decay_w0[j], rwkv_iclr_a0[j], v0)
            mu8 = _rows8(*[rwkv_mu[j, m] for m in range(rwkv_mu.shape[1])])
            w1, w2 = _lora(rwkv_decay_w1[j], rwkv_decay_w2[j])
            a1, a2 = _lora(rwkv_iclr_a1[j], rwkv_iclr_a2[j])
            g1, g2 = _lora(rwkv_gate_g1[j], rwkv_gate_g2[j])
            r, k, v, lw, a, gate = _rwkv_in(
                h, vecs, mu8, rwkv_w_r[j].astype(BF16), rwkv_w_k[j].astype(BF16),
                rwkv_w_v[j].astype(BF16), w1, w2, a1, a2, g1, g2, vres)
            if j == 0:
                v_first = v
            pvec = _rows8(rwkv_k_k[j], rwkv_k_a[j], rwkv_r_k[j].reshape(d), rwkv_lnx_w[j], rwkv_lnx_b[j])
            z = _wkv(r, lw, k, v, a, gate, pvec, WKV_PASSES)
            w_o = rwkv_w_o[j]
        elif kind == 1:
            z = _conv_in(h, g_mix, conv_w_in[j].astype(BF16), _rows8(*[conv_w[j, m] for m in range(3)]))
            w_o = conv_w_out[j]
        else:
            qg = jnp.tile(sb_q_norm[j], n_heads).reshape(1, d)
            kg = jnp.tile(sb_k_norm[j], n_heads).reshape(1, d)
            q, k, v = _sb_qkv(h.reshape(b * s, d), g_mix, sb_w_qkv[j].astype(BF16), qg, kg)
            z_max = (SB_ZMAX_SLACK * HEAD_DIM * SB_SCALE
                     * jnp.max(jnp.abs(sb_q_norm[j])) * jnp.max(jnp.abs(sb_k_norm[j])))
            z = _sb_attn(q.reshape(b, s, d), k.reshape(b, s, d), v.reshape(b, s, d), z_max)
            w_o = sb_w_o[j]
        h = _outproj_mlp(
            h.reshape(b * s, d), z.reshape(b * s, d), w_o.astype(BF16), mlp_norm[i].reshape(1, d), mlp_up_bf, mlp_down_bf, i,
        ).reshape(b, s, d)
    return h
```

```python
import functools
import math

import jax
import jax.numpy as jnp
from jax import lax
from jax.experimental import pallas as pl
from jax.experimental.pallas import tpu as pltpu

F32 = jnp.float32
BF16 = jnp.bfloat16

HEAD_DIM = 64
HEAD_SHIFT = 6
LANES = 128
SUBLANES = 8
RMS_EPS = 1e-6
GN_EPS = 64e-5
KK_EPS = 1e-24
SB_SCALE = HEAD_DIM ** -0.5
SB_EXP_ZERO = 105.0

V7X_VMEM_BYTES = 64 * 1024 * 1024
VMEM_LIMIT_CAP = V7X_VMEM_BYTES - 8 * 1024 * 1024

MLP_TM = 1024
MLP_TF = 1024
RWKV_TS = 256
CONV_TS = 512
QKV_TM = 512
QKV_SUM_PIECES = 1
WKV_CHUNK = 64
WKV_ROWS = 8
WKV_GROUP_ROWS = 2
WKV_PASSES = 1
WKV_CUM_PIECES = 2
SB_TQ = 256
SB_ROWS = 256
SB_LANE_TILES = 8


def _vmem_limit(block_bytes, scratch_bytes, temp_bytes):
    return int(min(2 * block_bytes + scratch_bytes + temp_bytes, VMEM_LIMIT_CAP))


def _nbytes(shape, dtype):
    n = 1
    for s in shape:
        n *= s
    return n * jnp.dtype(dtype).itemsize


def _dot(a, b):
    return jnp.dot(a, b, preferred_element_type=F32)


def _dot_nt(a, b):
    return lax.dot_general(a, b, (((1,), (1,)), ((), ())), preferred_element_type=F32)


def _dot_tn(a, b):
    return lax.dot_general(a, b, (((0,), (0,)), ((), ())), preferred_element_type=F32)


def _split(x, pieces):
    out = []
    for _ in range(pieces - 1):
        hi = x.astype(BF16)
        out.append(hi)
        x = x - hi.astype(F32)
    out.append(x.astype(BF16))
    return out


def _dot_exact_rhs(x, m_bf16, pieces, dot=_dot):
    acc = None
    for p in _split(x, pieces):
        t = dot(p, m_bf16)
        acc = t if acc is None else acc + t
    return acc


def _dot_exact_lhs(m_bf16, x, pieces):
    acc = None
    for p in _split(x, pieces):
        t = _dot(m_bf16, p)
        acc = t if acc is None else acc + t
    return acc


def _mm(a, b, passes, dot=_dot):
    if passes == 1:
        return dot(a.astype(BF16), b.astype(BF16))
    ah, al = _split(a, 2)
    bh, bl = _split(b, 2)
    return dot(ah, bh) + (dot(al, bh) + dot(ah, bl))


def _rms(x, g):
    ms = jnp.mean(x * x, axis=-1, keepdims=True)
    return x * lax.rsqrt(ms + RMS_EPS) * g


def _sigmoid(x):
    return 1.0 / (1.0 + jnp.exp(-x))


def _softplus(x):
    return jnp.maximum(x, 0.0) + jnp.log(1.0 + jnp.exp(-jnp.abs(x)))


def _head_sum_matrix(n):
    r = lax.broadcasted_iota(jnp.int32, (n, n), 0) >> HEAD_SHIFT
    c = lax.broadcasted_iota(jnp.int32, (n, n), 1) >> HEAD_SHIFT
    return jnp.where(r == c, 1.0, 0.0).astype(BF16)


def _shift_rows(x, carry8, n):
    rows = lax.broadcasted_iota(jnp.int32, (x.shape[0], 1), 0)
    out = pltpu.roll(x, n, axis=0)
    for i in range(n):
        out = jnp.where(rows == i, carry8[SUBLANES - n + i:SUBLANES - n + i + 1, :], out)
    return out


def _rwkv_in_kernel(*refs, has_vres):
    if has_vres:
        (h_ref, vec_ref, mu_ref, wr_ref, wk_ref, wv_ref, w1_ref, w2_ref, a1_ref, a2_ref,
         g1_ref, g2_ref, v1_ref, v2_ref, vf_ref,
         r_out, k_out, v_out, lw_out, a_out, g_out, carry_ref) = refs
    else:
        (h_ref, vec_ref, mu_ref, wr_ref, wk_ref, wv_ref, w1_ref, w2_ref, a1_ref, a2_ref,
         g1_ref, g2_ref,
         r_out, k_out, v_out, lw_out, a_out, g_out, carry_ref) = refs

    @pl.when(pl.program_id(1) == 0)
    def _():
        carry_ref[...] = jnp.zeros_like(carry_ref)

    ts = h_ref.shape[0]
    u = _rms(h_ref[...], vec_ref[0:1, :])
    prev = _shift_rows(u, carry_ref[...], 1)
    carry_ref[...] = u[ts - SUBLANES:, :]
    xx = prev - u

    def mix(i):
        return (u + xx * mu_ref[i:i + 1, :]).astype(BF16)

    hw = jnp.tanh(_dot(mix(1), w1_ref[...])).astype(BF16)
    ha = _dot(mix(4), a1_ref[...]).astype(BF16)
    hg = _sigmoid(_dot(mix(5), g1_ref[...])).astype(BF16)
    xv = mix(3)
    if has_vres:
        hv = _dot(xv, v1_ref[...]).astype(BF16)
    dw = vec_ref[1:2, :] + _dot(hw, w2_ref[...])
    da = vec_ref[2:3, :] + _dot(ha, a2_ref[...])
    g_out[...] = _dot(hg, g2_ref[...])
    if has_vres:
        dv = vec_ref[3:4, :] + _dot(hv, v2_ref[...])

    r_out[...] = _dot(mix(0), wr_ref[...])
    lw_out[...] = -jnp.exp(-_softplus(-dw) - 0.5)
    k_out[...] = _dot(mix(2), wk_ref[...])
    a_out[...] = _sigmoid(da)
    v = _dot(xv, wv_ref[...])
    if has_vres:
        v = v + (vf_ref[...] - v) * _sigmoid(dv)
    v_out[...] = v


def _rwkv_in(h, vecs, mu8, wr, wk, wv, w1, w2, a1, a2, g1, g2, vres):
    b, s, d = h.shape
    ts = min(RWKV_TS, s)
    tok = pl.BlockSpec((None, ts, d), lambda i, j: (i, j, 0))

    def full(x):
        return pl.BlockSpec(x.shape, lambda i, j: (0,) * x.ndim)

    weights = [vecs, mu8, wr, wk, wv, w1, w2, a1, a2, g1, g2]
    args = [h] + weights
    in_specs = [tok] + [full(w) for w in weights]
    if vres is not None:
        v1, v2, v_first = vres
        args += [v1, v2, v_first]
        in_specs += [full(v1), full(v2), tok]
    tok_bytes = _nbytes((ts, d), F32)
    block_bytes = sum(_nbytes(w.shape, w.dtype) for w in args[1:1 + len(weights) + (2 if vres else 0)])
    block_bytes += tok_bytes * (7 + (1 if vres else 0))
    out = jax.ShapeDtypeStruct((b, s, d), F32)
    return pl.pallas_call(
        functools.partial(_rwkv_in_kernel, has_vres=vres is not None),
        grid=(b, s // ts),
        in_specs=in_specs,
        out_specs=[tok] * 6,
        out_shape=[out] * 6,
        scratch_shapes=[pltpu.VMEM((SUBLANES, d), F32)],
        compiler_params=pltpu.CompilerParams(
            dimension_semantics=("parallel", "arbitrary"),
            vmem_limit_bytes=_vmem_limit(block_bytes, _nbytes((SUBLANES, d), F32), 12 * tok_bytes)),
        name="rwkv_in",
    )(*args)


def _wkv_kernel(r_ref, lw_ref, k_ref, v_ref, a_ref, g_ref, pv_ref, y_ref, s_ref, *, passes):
    n_rows, c, d_model = r_ref.shape
    n = 2 * c
    n_tiles = d_model // LANES

    @pl.when(pl.program_id(1) == 0)
    def _():
        s_ref[...] = jnp.zeros_like(s_ref)

    first_head = lax.broadcasted_iota(jnp.int32, (c, LANES), 1) < HEAD_DIM

    def head_sums(x):
        lo = jnp.sum(jnp.where(first_head, x, 0.0), axis=-1, keepdims=True)
        hi = jnp.sum(jnp.where(first_head, 0.0, x), axis=-1, keepdims=True)
        return jnp.where(first_head, lo, hi)

    ti = lax.broadcasted_iota(jnp.int32, (c, c), 0)
    si = lax.broadcasted_iota(jnp.int32, (c, c), 1)
    lincl = jnp.where(si <= ti, 1.0, 0.0).astype(BF16)
    log2c = c.bit_length() - 1
    row = lax.broadcasted_iota(jnp.int32, (n, LANES), 0)
    lane = lax.broadcasted_iota(jnp.int32, (n, LANES), 1)
    own = (row >> log2c) == (lane >> HEAD_SHIFT)
    ri = lax.broadcasted_iota(jnp.int32, (n, 2 * n), 0)
    ci = lax.broadcasted_iota(jnp.int32, (n, 2 * n), 1) & (c - 1)
    tri = ci < (ri & (c - 1)) + (ri >> log2c)
    inv_n = 1.0 / HEAD_DIM
    mm = functools.partial(_mm, passes=passes)
    mm_nt = functools.partial(_mm, passes=passes, dot=_dot_nt)
    mm_tn = functools.partial(_mm, passes=passes, dot=_dot_tn)

    def stack(x):
        return jnp.where(own, jnp.concatenate([x, x], axis=0), 0.0)

    steps = log2c
    rows_per_group = WKV_GROUP_ROWS if n_rows % WKV_GROUP_ROWS == 0 else 1
    groups = [[(bi, slice(t * LANES, (t + 1) * LANES), t)
               for bi in range(g0, g0 + rows_per_group) for t in range(n_tiles)]
              for g0 in range(0, n_rows, rows_per_group)]

    def each(f, *cols):
        return [f(*args) for args in zip(*cols)]

    def pro_keys(units):
        st = {"units": units}
        for name, ref in (("r", r_ref), ("lw", lw_ref), ("k_raw", k_ref), ("v", v_ref), ("a", a_ref)):
            st[name] = [ref[bi, :, sl] for bi, sl, _ in units]

        def param(i):
            return [pv_ref[i:i + 1, sl] for _, sl, _ in units]

        kk = each(lambda x, g: x * g, st["k_raw"], param(0))
        kk_ss = each(lambda x: head_sums(x * x), kk)
        st["kk"] = each(lambda x, ss: x * lax.rsqrt(jnp.maximum(ss, KK_EPS)), kk, kk_ss)
        st["k"] = each(lambda x, ai, g: x * (1.0 + (ai - 1.0) * g), st["k_raw"], st["a"], param(1))
        st["bonus"] = each(lambda ri_, ki, g, vi: head_sums(ri_ * ki * g) * vi, st["r"], st["k"], param(2), st["v"])
        return st

    def pro_cum(st):
        st["cum"] = each(lambda x: _dot_exact_lhs(lincl, x, WKV_CUM_PIECES), st["lw"])
        return st

    def pro_scale(st):
        g_incl = each(jnp.exp, st["cum"])
        g_excl = each(lambda cs, x: jnp.exp(cs - x), st["cum"], st["lw"])
        g_inv = each(lambda cs: jnp.exp(-cs), st["cum"])
        st["g_end"] = each(lambda g: g[c - 1:c, :], g_incl)
        st["ar"] = each(lambda kki, ri_, ge, gi: jnp.concatenate([-kki * ge, ri_ * gi], axis=0),
                        st["kk"], st["r"], g_excl, g_incl)
        st["bt"] = each(lambda kki, ai, gv: kki * ai * gv, st["kk"], st["a"], g_inv)
        st["kt"] = each(lambda ki, gv: ki * gv, st["k"], g_inv)
        return st

    def pro_stack(st):
        st["bks"] = each(lambda b_, k_: jnp.concatenate([stack(b_), stack(k_)], axis=0), st["bt"], st["kt"])
        st["vs"] = each(stack, st["v"])
        return st

    def pro_products(st):
        st["s0"] = [s_ref[bi, t] for bi, _, t in st["units"]]
        st["prod"] = each(lambda x, y: jnp.where(tri, mm_nt(x, y), 0.0), st["ar"], st["bks"])
        st["from_s0"] = each(mm_nt, st["ar"], st["s0"])
        return st

    pro_pieces = [pro_keys, pro_cum, pro_scale, pro_stack, pro_products]
    assert len(pro_pieces) <= steps

    def solve_start(st):
        x = each(lambda fs, pr, vs: fs[:c, :] + mm(pr[:c, n:], vs), st["from_s0"], st["prod"], st["vs"])
        return x, [pr[:c, :n] for pr in st["prod"]]

    def solve_stage(x, p, last):
        if last:
            return each(lambda xi, pi: xi + mm(pi, stack(xi)), x, p), None
        px = each(lambda xi, pi: mm(pi, jnp.concatenate([stack(xi), stack(pi)], axis=1)), x, p)
        return each(lambda xi, pxi: xi + pxi[:, :LANES], x, px), each(lambda pxi: pxi[:, LANES:], px)

    def finish(st, x):
        st["y"] = each(lambda fs, pr, vs, u: fs[c:, :] + mm(pr[c:, :], jnp.concatenate([stack(u), vs], axis=0)),
                       st["from_s0"], st["prod"], st["vs"], x)
        outer = each(lambda u, v, bt, kt, ge: mm_tn(jnp.concatenate([u, v], axis=0),
                                                    jnp.concatenate([bt * ge, kt * ge], axis=0)),
                     x, st["v"], st["bt"], st["kt"], st["g_end"])
        for (bi, _, t), s0, ge, o in zip(st["units"], st["s0"], st["g_end"], outer):
            s_ref[bi, t] = s0 * ge + jnp.where(own, o, 0.0)

    def epilogue(st, lo, hi):
        y = st["y"][lo:hi]
        mean = each(lambda yi: head_sums(yi) * inv_n, y)
        yc = each(lambda yi, m: yi - m, y, mean)
        var = each(lambda yi: head_sums(yi * yi) * inv_n, yc)
        for (bi, sl, _), bonus, yi, vr in zip(st["units"][lo:hi], st["bonus"][lo:hi], yc, var):
            out = yi * lax.rsqrt(vr + GN_EPS) * pv_ref[3:4, sl] + pv_ref[4:5, sl] + bonus
            y_ref[bi, :, sl] = (out * g_ref[bi, :, sl]).astype(BF16)

    cur = groups[0]
    for piece in pro_pieces:
        cur = piece(cur)
    prev = None
    for gi in range(len(groups)):
        x, p = solve_start(cur)
        nxt = groups[gi + 1] if gi + 1 < len(groups) else None
        n_units = len(groups[gi])
        for i in range(steps):
            x, p = solve_stage(x, p, i == steps - 1)
            if nxt is not None and i < len(pro_pieces):
                nxt = pro_pieces[i](nxt)
            if prev is not None:
                epilogue(prev, i * n_units // steps, (i + 1) * n_units // steps)
        finish(cur, x)
        prev, cur = cur, nxt
    epilogue(prev, 0, len(prev["units"]))


def _wkv(r, lw, k, v, a, g, pvec, passes):
    b, s, d = r.shape
    c = WKV_CHUNK
    rows = math.gcd(b, WKV_ROWS)
    tok = pl.BlockSpec((rows, c, d), lambda i, j: (i, j, 0))
    par = pl.BlockSpec((SUBLANES, d), lambda i, j: (0, 0))
    tok_bytes = _nbytes((rows, c, d), F32)
    sq_bytes = _nbytes((2 * c, 2 * c), F32)
    n_tiles = d // LANES
    return pl.pallas_call(
        functools.partial(_wkv_kernel, passes=passes),
        grid=(b // rows, s // c),
        in_specs=[tok] * 6 + [par],
        out_specs=tok,
        out_shape=jax.ShapeDtypeStruct((b, s, d), BF16),
        scratch_shapes=[pltpu.VMEM((rows, n_tiles, LANES, LANES), F32)],
        compiler_params=pltpu.CompilerParams(
            dimension_semantics=("parallel", "arbitrary"),
            vmem_limit_bytes=_vmem_limit(8 * tok_bytes, rows * n_tiles * sq_bytes, rows * n_tiles * 48 * sq_bytes)),
        name="wkv",
    )(r, lw, k, v, a, g, pvec)


def _conv_in_kernel(h_ref, g_ref, win_ref, cw_ref, z_out, carry_ref):
    @pl.when(pl.program_id(1) == 0)
    def _():
        carry_ref[...] = jnp.zeros_like(carry_ref)

    ts, d = h_ref.shape
    u = _rms(h_ref[...], g_ref[...]).astype(BF16)
    z = _dot(u, win_ref[:, d:2 * d]) * _dot(u, win_ref[:, 2 * d:])
    b_gate = _dot(u, win_ref[:, :d])
    carry = carry_ref[...]
    z1 = _shift_rows(z, carry, 1)
    z2 = _shift_rows(z, carry, 2)
    carry_ref[...] = z[ts - SUBLANES:, :]
    zc = z2 * cw_ref[0:1, :] + z1 * cw_ref[1:2, :] + z * cw_ref[2:3, :]
    z_out[...] = (b_gate * zc).astype(BF16)


def _conv_in(h, g, w_in, cw8):
    b, s, d = h.shape
    ts = min(CONV_TS, s)
    tok = pl.BlockSpec((None, ts, d), lambda i, j: (i, j, 0))
    tok_bytes = _nbytes((ts, d), F32)
    block_bytes = 2 * tok_bytes + _nbytes(w_in.shape, BF16) + _nbytes(cw8.shape, F32)
    return pl.pallas_call(
        _conv_in_kernel,
        grid=(b, s // ts),
        in_specs=[tok,
                  pl.BlockSpec(g.shape, lambda i, j: (0, 0)),
                  pl.BlockSpec(w_in.shape, lambda i, j: (0, 0)),
                  pl.BlockSpec(cw8.shape, lambda i, j: (0, 0))],
        out_specs=tok,
        out_shape=jax.ShapeDtypeStruct((b, s, d), BF16),
        scratch_shapes=[pltpu.VMEM((SUBLANES, d), F32)],
        compiler_params=pltpu.CompilerParams(
            dimension_semantics=("parallel", "arbitrary"),
            vmem_limit_bytes=_vmem_limit(block_bytes, _nbytes((SUBLANES, d), F32), 10 * tok_bytes)),
        name="conv_in",
    )(h, g, w_in, cw8)


def _sb_qkv_kernel(h_ref, g_ref, w_ref, qg_ref, kg_ref, q_out, k_out, v_out):
    d = h_ref.shape[1]
    u = _rms(h_ref[...], g_ref[...]).astype(BF16)
    hsum = _head_sum_matrix(2 * LANES)
    inv_n = 1.0 / HEAD_DIM

    def head_norm(x, gain):
        cols = []
        for j in range(0, d, 2 * LANES):
            xs = x[:, j:j + 2 * LANES]
            ms = _dot_exact_rhs(xs * xs, hsum, QKV_SUM_PIECES) * inv_n
            cols.append(xs * lax.rsqrt(ms + RMS_EPS))
        return jnp.concatenate(cols, axis=1) * gain

    q = _dot(u, w_ref[:, :d])
    k = _dot(u, w_ref[:, d:2 * d])
    q_out[...] = head_norm(q, qg_ref[...]).astype(BF16)
    v = _dot(u, w_ref[:, 2 * d:])
    k_out[...] = head_norm(k, kg_ref[...]).astype(BF16)
    v_out[...] = v.astype(BF16)


def _sb_qkv(h2, g, w_qkv, qg, kg):
    t, d = h2.shape
    tm = min(QKV_TM, t)
    tok = pl.BlockSpec((tm, d), lambda i: (i, 0))
    vec = pl.BlockSpec((1, d), lambda i: (0, 0))
    tok_bytes = _nbytes((tm, d), F32)
    block_bytes = 4 * tok_bytes + _nbytes(w_qkv.shape, BF16) + 3 * _nbytes((1, d), F32)
    out = jax.ShapeDtypeStruct((t, d), BF16)
    return pl.pallas_call(
        _sb_qkv_kernel,
        grid=(t // tm,),
        in_specs=[tok, vec, pl.BlockSpec(w_qkv.shape, lambda i: (0, 0)), vec, vec],
        out_specs=[tok] * 3,
        out_shape=[out] * 3,
        compiler_params=pltpu.CompilerParams(
            dimension_semantics=("parallel",),
            vmem_limit_bytes=_vmem_limit(block_bytes, 0, 10 * tok_bytes)),
        name="sb_qkv",
    )(h2, g, w_qkv, qg, kg)


def _sb_attn_kernel(q_ref, k_ref, v_ref, suffix_ref, o_ref,
                    qs_ref, acc_ref, run_ref, za_ref, zb_ref, att_ref):
    tq = q_ref.shape[0]
    n_lt = q_ref.shape[1] // LANES
    qb = pl.program_id(2)
    rows = lax.broadcasted_iota(jnp.int32, (2 * tq, LANES), 0)
    lanes = lax.broadcasted_iota(jnp.int32, (2 * tq, LANES), 1)
    own = (rows >> (tq.bit_length() - 1)) == (lanes >> HEAD_SHIFT)
    lts = [slice(p * LANES, (p + 1) * LANES) for p in range(n_lt)]
    for p, sl in enumerate(lts):
        q = q_ref[:, sl].astype(F32)
        qs_ref[p] = (jnp.where(own, jnp.concatenate([q, q], axis=0), 0.0) * SB_SCALE).astype(BF16)
    suffix = suffix_ref[...]

    groups = [slice(g * SB_ROWS, (g + 1) * SB_ROWS) for g in range(2 * tq // SB_ROWS)]
    units = [(p, g) for p in range(n_lt) for g in groups]

    def each(f, *cols):
        return [f(*args) for args in zip(*cols)]

    def key_tile(ref, j, p):
        return ref[pl.ds(pl.multiple_of(j * tq, tq), tq), lts[p]]

    def scores(j, z_out):
        for p in range(n_lt):
            ks = key_tile(k_ref, j, p)
            for g in groups:
                z_out[p, g, :] = _dot_nt(qs_ref[p, g, :], ks)

    def add_values(j):
        pv = []
        for p in range(n_lt):
            vs = key_tile(v_ref, j, p)
            pv += [_dot(att_ref[p, g, :], vs) for g in groups]
        return pv

    def step(j, z_in, z_out, diagonal):
        if not diagonal:
            pv = add_values(j + 1)
        if diagonal:
            t_loc = lax.broadcasted_iota(jnp.int32, (SB_ROWS, tq), 0)
            s_loc = lax.broadcasted_iota(jnp.int32, (SB_ROWS, tq), 1)
            causal = [s_loc < t_loc + (g.start & (tq - 1)) for _, g in units]
        spb = []
        for i, (p, g) in enumerate(units):
            z = z_in[p, g, :]
            sp = _softplus(z)
            z_in[p, g, :] = z - sp
            if diagonal:
                sp = jnp.where(causal[i], sp, 0.0)
            spb.append(sp.astype(BF16))
        tail = each(lambda x: _dot(x, suffix), spb)
        if z_out is not None:
            scores(jnp.maximum(j - 1, 0), z_out)
        for i, (p, g) in enumerate(units):
            total = jnp.broadcast_to(tail[i][:, 0:1] + spb[i][:, 0:1].astype(F32), (SB_ROWS, LANES))
            if diagonal:
                att = jnp.where(causal[i], jnp.exp(z_in[p, g, :] - tail[i]), 0.0)
                run_ref[p, g, :] = total
                acc_ref[p, g, :] = jnp.zeros((SB_ROWS, LANES), F32)
            else:
                run = run_ref[p, g, :]
                att = jnp.exp(z_in[p, g, :] - tail[i] - jnp.concatenate([run] * (tq // LANES), axis=1))
                run_ref[p, g, :] = run + total
                acc_ref[p, g, :] += pv[i]
            att_ref[p, g, :] = att.astype(BF16)

    stop_at = SB_EXP_ZERO

    def run_min():
        return jnp.min(run_ref[...])

    scores(qb, za_ref)
    step(qb, za_ref, zb_ref, True)

    @pl.when(qb >= 1)
    def _():
        step(qb - 1, zb_ref, None, False)

    def more(state):
        j, low = state
        return jnp.logical_and(j >= 1, low < stop_at)

    def pair(state):
        j, _ = state
        scores(j, za_ref)
        step(j, za_ref, zb_ref, False)
        step(j - 1, zb_ref, None, False)
        return j - 2, run_min()

    j, low = lax.while_loop(more, pair, (jnp.maximum(qb - 2, -1), run_min()))
    last_tile = jnp.logical_and(j == 0, low < stop_at)

    @pl.when(last_tile)
    def _():
        scores(0, za_ref)
        step(0, za_ref, None, False)

    pv = add_values(jnp.where(last_tile, 0, j + 1))
    for (p, g), pvi in zip(units, pv):
        acc_ref[p, g, :] += pvi
    lane = lax.broadcasted_iota(jnp.int32, (tq, LANES), 1)
    for p, sl in enumerate(lts):
        o_ref[:, sl] = jnp.where(lane >= HEAD_DIM, acc_ref[p, tq:, :], acc_ref[p, :tq, :]).astype(BF16)


def _sb_attn(q, k, v):
    b, s, d = q.shape
    tq = SB_TQ
    n_lt = min(SB_LANE_TILES, d // LANES)
    width = n_lt * LANES
    ri = lax.broadcasted_iota(jnp.int32, (tq, tq), 0)
    ci = lax.broadcasted_iota(jnp.int32, (tq, tq), 1)
    suffix = jnp.where(ri > ci, 1.0, 0.0).astype(BF16)
    qspec = pl.BlockSpec((None, tq, width), lambda i, p, j: (i, j, p))
    kspec = pl.BlockSpec((None, s, width), lambda i, p, j: (i, 0, p))
    block_bytes = 2 * _nbytes((tq, width), BF16) + 2 * _nbytes((s, width), BF16) + _nbytes((tq, tq), BF16)
    scratch = [pltpu.VMEM((n_lt, 2 * tq, LANES), BF16),
               pltpu.VMEM((n_lt, 2 * tq, LANES), F32),
               pltpu.VMEM((n_lt, 2 * tq, LANES), F32),
               pltpu.VMEM((n_lt, 2 * tq, tq), F32),
               pltpu.VMEM((n_lt, 2 * tq, tq), F32),
               pltpu.VMEM((n_lt, 2 * tq, tq), BF16)]
    scratch_bytes = n_lt * (_nbytes((2 * tq, LANES), BF16) + 2 * _nbytes((2 * tq, LANES), F32)
                            + 2 * _nbytes((2 * tq, tq), F32) + _nbytes((2 * tq, tq), BF16))
    return pl.pallas_call(
        _sb_attn_kernel,
        grid=(b, d // width, s // tq),
        in_specs=[qspec, kspec, kspec, pl.BlockSpec((tq, tq), lambda i, p, j: (0, 0))],
        out_specs=qspec,
        out_shape=jax.ShapeDtypeStruct((b, s, d), BF16),
        scratch_shapes=scratch,
        compiler_params=pltpu.CompilerParams(
            dimension_semantics=("parallel", "parallel", "arbitrary"),
            vmem_limit_bytes=_vmem_limit(block_bytes, scratch_bytes, 12 * _nbytes((2 * tq, tq), F32))),
        name="sb_attn",
    )(q, k, v, suffix)


def _outproj_mlp_kernel(h_ref, z_ref, wo_ref, g_ref, wup_ref, wdn_ref, o_ref, xn_ref):
    @pl.when(pl.program_id(1) == 0)
    def _():
        h1 = h_ref[...] + _dot(z_ref[...], wo_ref[...])
        o_ref[...] = h1
        xn_ref[...] = _rms(h1, g_ref[...]).astype(BF16)

    act = jnp.square(jnp.maximum(_dot(xn_ref[...], wup_ref[...]), 0.0)).astype(BF16)
    o_ref[...] += _dot(act, wdn_ref[...])


def _outproj_mlp(h2, z, w_o, g, w_up_all, w_dn_all, layer):
    t, d = h2.shape
    f = w_up_all.shape[2]
    tm = min(MLP_TM, t)
    tf = min(MLP_TF, f)
    tok = pl.BlockSpec((tm, d), lambda i, j: (i, 0))
    in_specs = [
        tok, tok,
        pl.BlockSpec((d, d), lambda i, j: (0, 0)),
        pl.BlockSpec((1, d), lambda i, j: (0, 0)),
        pl.BlockSpec((None, d, tf), lambda i, j: (layer, 0, j)),
        pl.BlockSpec((None, tf, d), lambda i, j: (layer, j, 0)),
    ]
    tok_bytes = _nbytes((tm, d), F32)
    block_bytes = 2 * tok_bytes + _nbytes((tm, d), BF16) + _nbytes((d, d), BF16) + 2 * _nbytes((d, tf), BF16)
    return pl.pallas_call(
        _outproj_mlp_kernel,
        grid=(t // tm, f // tf),
        in_specs=in_specs,
        out_specs=tok,
        out_shape=jax.ShapeDtypeStruct((t, d), F32),
        scratch_shapes=[pltpu.VMEM((tm, d), BF16)],
        compiler_params=pltpu.CompilerParams(
            dimension_semantics=("parallel", "arbitrary"),
            vmem_limit_bytes=_vmem_limit(block_bytes, _nbytes((tm, d), BF16),
                                         tok_bytes + 2 * _nbytes((tm, tf), F32))),
        name="outproj_mlp",
    )(h2, z, w_o, g, w_up_all, w_dn_all)


def _pad_cols(w, mult):
    pad = (-w.shape[1]) % mult
    return jnp.pad(w, ((0, 0), (0, pad)))


def _pad_rows(w, mult):
    pad = (-w.shape[0]) % mult
    return jnp.pad(w, ((0, pad), (0, 0)))


def _rows8(*rows):
    d = rows[0].shape[-1]
    pad = jnp.zeros((SUBLANES - len(rows), d), F32)
    return jnp.concatenate([r.reshape(1, d).astype(F32) for r in rows] + [pad], axis=0)


def _lora(w_in, w_out):
    return _pad_cols(w_in, LANES).astype(BF16), _pad_rows(w_out, LANES).astype(BF16)


def kernel(x, mix_norm, mlp_norm, mlp_up, mlp_down, rwkv_mu, rwkv_w_r, rwkv_w_k, rwkv_w_v, rwkv_w_o, rwkv_decay_w0, rwkv_decay_w1, rwkv_decay_w2, rwkv_iclr_a0, rwkv_iclr_a1, rwkv_iclr_a2, rwkv_gate_g1, rwkv_gate_g2, rwkv_k_k, rwkv_k_a, rwkv_r_k, rwkv_lnx_w, rwkv_lnx_b, rwkv_vres_v0, rwkv_vres_v1, rwkv_vres_v2, conv_w_in, conv_w, conv_w_out, sb_w_qkv, sb_q_norm, sb_k_norm, sb_w_o):
    b, s, d = x.shape
    depth = mix_norm.shape[0]
    n_heads = d // HEAD_DIM
    mlp_up_bf = mlp_up.astype(BF16)
    mlp_down_bf = mlp_down.astype(BF16)
    h = x
    v_first = None
    for i in range(depth):
        kind = i % 3
        j = i // 3
        g_mix = mix_norm[i].reshape(1, d)
        if kind == 0:
            vres = None
            v0 = jnp.zeros((d,), F32)
            if j > 0:
                v1, v2 = _lora(rwkv_vres_v1[j - 1], rwkv_vres_v2[j - 1])
                vres = (v1, v2, v_first)
                v0 = rwkv_vres_v0[j - 1]
            vecs = _rows8(mix_norm[i], rwkv_decay_w0[j], rwkv_iclr_a0[j], v0)
            mu8 = _rows8(*[rwkv_mu[j, m] for m in range(rwkv_mu.shape[1])])
            w1, w2 = _lora(rwkv_decay_w1[j], rwkv_decay_w2[j])
            a1, a2 = _lora(rwkv_iclr_a1[j], rwkv_iclr_a2[j])
            g1, g2 = _lora(rwkv_gate_g1[j], rwkv_gate_g2[j])
            r, k, v, lw, a, gate = _rwkv_in(
                h, vecs, mu8, rwkv_w_r[j].astype(BF16), rwkv_w_k[j].astype(BF16),
                rwkv_w_v[j].astype(BF16), w1, w2, a1, a2, g1, g2, vres)
            if j == 0:
                v_first = v
            pvec = _rows8(rwkv_k_k[j], rwkv_k_a[j], rwkv_r_k[j].reshape(d), rwkv_lnx_w[j], rwkv_lnx_b[j])
            z = _wkv(r, lw, k, v, a, gate, pvec, WKV_PASSES)
            w_o = rwkv_w_o[j]
        elif kind == 1:
            z = _conv_in(h, g_mix, conv_w_in[j].astype(BF16), _rows8(*[conv_w[j, m] for m in range(3)]))
            w_o = conv_w_out[j]
        else:
            qg = jnp.tile(sb_q_norm[j], n_heads).reshape(1, d)
            kg = jnp.tile(sb_k_norm[j], n_heads).reshape(1, d)
            q, k, v = _sb_qkv(h.reshape(b * s, d), g_mix, sb_w_qkv[j].astype(BF16), qg, kg)
            z = _sb_attn(q.reshape(b, s, d), k.reshape(b, s, d), v.reshape(b, s, d))
            w_o = sb_w_o[j]
        h = _outproj_mlp(
            h.reshape(b * s, d), z.reshape(b * s, d), w_o.astype(BF16), mlp_norm[i].reshape(1, d), mlp_up_bf, mlp_down_bf, i,
        ).reshape(b, s, d)
    return h
```

```python
import functools
import math

import jax
import jax.numpy as jnp
from jax import lax
from jax.experimental import pallas as pl
from jax.experimental.pallas import tpu as pltpu

F32 = jnp.float32
BF16 = jnp.bfloat16

HEAD_DIM = 64
HEAD_SHIFT = 6
LANES = 128
SUBLANES = 8
RMS_EPS = 1e-6
GN_EPS = 64e-5
KK_EPS = 1e-24
SB_SCALE = HEAD_DIM ** -0.5
SB_EXP_ZERO = 105.0

V7X_VMEM_BYTES = 64 * 1024 * 1024
VMEM_LIMIT_CAP = V7X_VMEM_BYTES - 8 * 1024 * 1024

MLP_TM = 1024
MLP_TF = 2048
RWKV_TS = 256
CONV_TS = 1024
QKV_TM = 1024
QKV_SUM_PIECES = 1
WKV_CHUNK = 64
WKV_ROWS = 8
WKV_GROUP_ROWS = 2
WKV_PASSES = 1
WKV_CUM_PIECES = 2
SB_TQ = 256
SB_ROWS = 256
SB_LANE_TILES = 8


def _vmem_limit(block_bytes, scratch_bytes, temp_bytes):
    return int(min(2 * block_bytes + scratch_bytes + temp_bytes, VMEM_LIMIT_CAP))


def _nbytes(shape, dtype):
    n = 1
    for s in shape:
        n *= s
    return n * jnp.dtype(dtype).itemsize


def _dot(a, b):
    return jnp.dot(a, b, preferred_element_type=F32)


def _dot_nt(a, b):
    return lax.dot_general(a, b, (((1,), (1,)), ((), ())), preferred_element_type=F32)


def _dot_tn(a, b):
    return lax.dot_general(a, b, (((0,), (0,)), ((), ())), preferred_element_type=F32)


def _split(x, pieces):
    out = []
    for _ in range(pieces - 1):
        hi = x.astype(BF16)
        out.append(hi)
        x = x - hi.astype(F32)
    out.append(x.astype(BF16))
    return out


def _dot_exact_rhs(x, m_bf16, pieces, dot=_dot):
    acc = None
    for p in _split(x, pieces):
        t = dot(p, m_bf16)
        acc = t if acc is None else acc + t
    return acc


def _dot_exact_lhs(m_bf16, x, pieces):
    acc = None
    for p in _split(x, pieces):
        t = _dot(m_bf16, p)
        acc = t if acc is None else acc + t
    return acc


def _mm(a, b, passes, dot=_dot):
    if passes == 1:
        return dot(a.astype(BF16), b.astype(BF16))
    ah, al = _split(a, 2)
    bh, bl = _split(b, 2)
    return dot(ah, bh) + (dot(al, bh) + dot(ah, bl))


def _rms(x, g):
    ms = jnp.mean(x * x, axis=-1, keepdims=True)
    return x * lax.rsqrt(ms + RMS_EPS) * g


def _sigmoid(x):
    return 1.0 / (1.0 + jnp.exp(-x))


def _softplus(x):
    return jnp.maximum(x, 0.0) + jnp.log(1.0 + jnp.exp(-jnp.abs(x)))


def _head_sum_matrix(n):
    r = lax.broadcasted_iota(jnp.int32, (n, n), 0) >> HEAD_SHIFT
    c = lax.broadcasted_iota(jnp.int32, (n, n), 1) >> HEAD_SHIFT
    return jnp.where(r == c, 1.0, 0.0).astype(BF16)


def _shift_rows(x, carry8, n):
    rows = lax.broadcasted_iota(jnp.int32, (x.shape[0], 1), 0)
    out = pltpu.roll(x, n, axis=0)
    for i in range(n):
        out = jnp.where(rows == i, carry8[SUBLANES - n + i:SUBLANES - n + i + 1, :], out)
    return out


def _rwkv_in_kernel(*refs, has_vres):
    if has_vres:
        (h_ref, vec_ref, mu_ref, wr_ref, wk_ref, wv_ref, w1_ref, w2_ref, a1_ref, a2_ref,
         g1_ref, g2_ref, v1_ref, v2_ref, vf_ref,
         r_out, k_out, v_out, lw_out, a_out, g_out, carry_ref) = refs
    else:
        (h_ref, vec_ref, mu_ref, wr_ref, wk_ref, wv_ref, w1_ref, w2_ref, a1_ref, a2_ref,
         g1_ref, g2_ref,
         r_out, k_out, v_out, lw_out, a_out, g_out, carry_ref) = refs

    @pl.when(pl.program_id(1) == 0)
    def _():
        carry_ref[...] = jnp.zeros_like(carry_ref)

    ts = h_ref.shape[0]
    u = _rms(h_ref[...], vec_ref[0:1, :])
    prev = _shift_rows(u, carry_ref[...], 1)
    carry_ref[...] = u[ts - SUBLANES:, :]
    xx = prev - u

    def mix(i):
        return (u + xx * mu_ref[i:i + 1, :]).astype(BF16)

    hw = jnp.tanh(_dot(mix(1), w1_ref[...])).astype(BF16)
    ha = _dot(mix(4), a1_ref[...]).astype(BF16)
    hg = _sigmoid(_dot(mix(5), g1_ref[...])).astype(BF16)
    xv = mix(3)
    if has_vres:
        hv = _dot(xv, v1_ref[...]).astype(BF16)
    dw = vec_ref[1:2, :] + _dot(hw, w2_ref[...])
    da = vec_ref[2:3, :] + _dot(ha, a2_ref[...])
    g_out[...] = _dot(hg, g2_ref[...])
    if has_vres:
        dv = vec_ref[3:4, :] + _dot(hv, v2_ref[...])

    r_out[...] = _dot(mix(0), wr_ref[...])
    lw_out[...] = -jnp.exp(-_softplus(-dw) - 0.5)
    k_out[...] = _dot(mix(2), wk_ref[...])
    a_out[...] = _sigmoid(da)
    v = _dot(xv, wv_ref[...])
    if has_vres:
        v = v + (vf_ref[...] - v) * _sigmoid(dv)
    v_out[...] = v


def _rwkv_in(h, vecs, mu8, wr, wk, wv, w1, w2, a1, a2, g1, g2, vres):
    b, s, d = h.shape
    ts = min(RWKV_TS, s)
    tok = pl.BlockSpec((None, ts, d), lambda i, j: (i, j, 0))

    def full(x):
        return pl.BlockSpec(x.shape, lambda i, j: (0,) * x.ndim)

    weights = [vecs, mu8, wr, wk, wv, w1, w2, a1, a2, g1, g2]
    args = [h] + weights
    in_specs = [tok] + [full(w) for w in weights]
    if vres is not None:
        v1, v2, v_first = vres
        args += [v1, v2, v_first]
        in_specs += [full(v1), full(v2), tok]
    tok_bytes = _nbytes((ts, d), F32)
    block_bytes = sum(_nbytes(w.shape, w.dtype) for w in args[1:1 + len(weights) + (2 if vres else 0)])
    block_bytes += tok_bytes * (7 + (1 if vres else 0))
    out = jax.ShapeDtypeStruct((b, s, d), F32)
    return pl.pallas_call(
        functools.partial(_rwkv_in_kernel, has_vres=vres is not None),
        grid=(b, s // ts),
        in_specs=in_specs,
        out_specs=[tok] * 6,
        out_shape=[out] * 6,
        scratch_shapes=[pltpu.VMEM((SUBLANES, d), F32)],
        compiler_params=pltpu.CompilerParams(
            dimension_semantics=("parallel", "arbitrary"),
            vmem_limit_bytes=_vmem_limit(block_bytes, _nbytes((SUBLANES, d), F32), 12 * tok_bytes)),
        name="rwkv_in",
    )(*args)


def _wkv_kernel(r_ref, lw_ref, k_ref, v_ref, a_ref, g_ref, pv_ref, y_ref, s_ref, *, passes):
    n_rows, c, d_model = r_ref.shape
    n = 2 * c
    n_tiles = d_model // LANES

    @pl.when(pl.program_id(1) == 0)
    def _():
        s_ref[...] = jnp.zeros_like(s_ref)

    first_head = lax.broadcasted_iota(jnp.int32, (c, LANES), 1) < HEAD_DIM

    def head_sums(x):
        lo = jnp.sum(jnp.where(first_head, x, 0.0), axis=-1, keepdims=True)
        hi = jnp.sum(jnp.where(first_head, 0.0, x), axis=-1, keepdims=True)
        return jnp.where(first_head, lo, hi)

    ti = lax.broadcasted_iota(jnp.int32, (c, c), 0)
    si = lax.broadcasted_iota(jnp.int32, (c, c), 1)
    lincl = jnp.where(si <= ti, 1.0, 0.0).astype(BF16)
    log2c = c.bit_length() - 1
    row = lax.broadcasted_iota(jnp.int32, (n, LANES), 0)
    lane = lax.broadcasted_iota(jnp.int32, (n, LANES), 1)
    own = (row >> log2c) == (lane >> HEAD_SHIFT)
    ri = lax.broadcasted_iota(jnp.int32, (n, 2 * n), 0)
    ci = lax.broadcasted_iota(jnp.int32, (n, 2 * n), 1) & (c - 1)
    tri = ci < (ri & (c - 1)) + (ri >> log2c)
    inv_n = 1.0 / HEAD_DIM
    mm = functools.partial(_mm, passes=passes)
    mm_nt = functools.partial(_mm, passes=passes, dot=_dot_nt)
    mm_tn = functools.partial(_mm, passes=passes, dot=_dot_tn)

    def stack(x):
        return jnp.where(own, jnp.concatenate([x, x], axis=0), 0.0)

    steps = log2c
    rows_per_group = WKV_GROUP_ROWS if n_rows % WKV_GROUP_ROWS == 0 else 1
    groups = [[(bi, slice(t * LANES, (t + 1) * LANES), t)
               for bi in range(g0, g0 + rows_per_group) for t in range(n_tiles)]
              for g0 in range(0, n_rows, rows_per_group)]

    def each(f, *cols):
        return [f(*args) for args in zip(*cols)]

    def pro_keys(units):
        st = {"units": units}
        for name, ref in (("r", r_ref), ("lw", lw_ref), ("k_raw", k_ref), ("v", v_ref), ("a", a_ref)):
            st[name] = [ref[bi, :, sl] for bi, sl, _ in units]

        def param(i):
            return [pv_ref[i:i + 1, sl] for _, sl, _ in units]

        kk = each(lambda x, g: x * g, st["k_raw"], param(0))
        kk_ss = each(lambda x: head_sums(x * x), kk)
        st["kk"] = each(lambda x, ss: x * lax.rsqrt(jnp.maximum(ss, KK_EPS)), kk, kk_ss)
        st["k"] = each(lambda x, ai, g: x * (1.0 + (ai - 1.0) * g), st["k_raw"], st["a"], param(1))
        st["bonus"] = each(lambda ri_, ki, g, vi: head_sums(ri_ * ki * g) * vi, st["r"], st["k"], param(2), st["v"])
        return st

    def pro_cum(st):
        st["cum"] = each(lambda x: _dot_exact_lhs(lincl, x, WKV_CUM_PIECES), st["lw"])
        return st

    def pro_scale(st):
        g_incl = each(jnp.exp, st["cum"])
        g_excl = each(lambda cs, x: jnp.exp(cs - x), st["cum"], st["lw"])
        g_inv = each(lambda cs: jnp.exp(-cs), st["cum"])
        st["g_end"] = each(lambda g: g[c - 1:c, :], g_incl)
        st["ar"] = each(lambda kki, ri_, ge, gi: jnp.concatenate([-kki * ge, ri_ * gi], axis=0),
                        st["kk"], st["r"], g_excl, g_incl)
        st["bt"] = each(lambda kki, ai, gv: kki * ai * gv, st["kk"], st["a"], g_inv)
        st["kt"] = each(lambda ki, gv: ki * gv, st["k"], g_inv)
        return st

    def pro_stack(st):
        st["bks"] = each(lambda b_, k_: jnp.concatenate([stack(b_), stack(k_)], axis=0), st["bt"], st["kt"])
        st["vs"] = each(stack, st["v"])
        return st

    def pro_products(st):
        st["s0"] = [s_ref[bi, t] for bi, _, t in st["units"]]
        st["prod"] = each(lambda x, y: jnp.where(tri, mm_nt(x, y), 0.0), st["ar"], st["bks"])
        st["from_s0"] = each(mm_nt, st["ar"], st["s0"])
        return st

    pro_pieces = [pro_keys, pro_cum, pro_scale, pro_stack, pro_products]
    assert len(pro_pieces) <= steps

    def solve_start(st):
        x = each(lambda fs, pr, vs: fs[:c, :] + mm(pr[:c, n:], vs), st["from_s0"], st["prod"], st["vs"])
        return x, [pr[:c, :n] for pr in st["prod"]]

    def solve_stage(x, p, last):
        if last:
            return each(lambda xi, pi: xi + mm(pi, stack(xi)), x, p), None
        px = each(lambda xi, pi: mm(pi, jnp.concatenate([stack(xi), stack(pi)], axis=1)), x, p)
        return each(lambda xi, pxi: xi + pxi[:, :LANES], x, px), each(lambda pxi: pxi[:, LANES:], px)

    def finish(st, x):
        st["y"] = each(lambda fs, pr, vs, u: fs[c:, :] + mm(pr[c:, :], jnp.concatenate([stack(u), vs], axis=0)),
                       st["from_s0"], st["prod"], st["vs"], x)
        outer = each(lambda u, v, bt, kt, ge: mm_tn(jnp.concatenate([u, v], axis=0),
                                                    jnp.concatenate([bt * ge, kt * ge], axis=0)),
                     x, st["v"], st["bt"], st["kt"], st["g_end"])
        for (bi, _, t), s0, ge, o in zip(st["units"], st["s0"], st["g_end"], outer):
            s_ref[bi, t] = s0 * ge + jnp.where(own, o, 0.0)

    def epilogue(st, lo, hi):
        y = st["y"][lo:hi]
        mean = each(lambda yi: head_sums(yi) * inv_n, y)
        yc = each(lambda yi, m: yi - m, y, mean)
        var = each(lambda yi: head_sums(yi * yi) * inv_n, yc)
        for (bi, sl, _), bonus, yi, vr in zip(st["units"][lo:hi], st["bonus"][lo:hi], yc, var):
            out = yi * lax.rsqrt(vr + GN_EPS) * pv_ref[3:4, sl] + pv_ref[4:5, sl] + bonus
            y_ref[bi, :, sl] = (out * g_ref[bi, :, sl]).astype(BF16)

    cur = groups[0]
    for piece in pro_pieces:
        cur = piece(cur)
    prev = None
    for gi in range(len(groups)):
        x, p = solve_start(cur)
        nxt = groups[gi + 1] if gi + 1 < len(groups) else None
        n_units = len(groups[gi])
        for i in range(steps):
            x, p = solve_stage(x, p, i == steps - 1)
            if nxt is not None and i < len(pro_pieces):
                nxt = pro_pieces[i](nxt)
            if prev is not None:
                epilogue(prev, i * n_units // steps, (i + 1) * n_units // steps)
        finish(cur, x)
        prev, cur = cur, nxt
    epilogue(prev, 0, len(prev["units"]))


def _wkv(r, lw, k, v, a, g, pvec, passes):
    b, s, d = r.shape
    c = WKV_CHUNK
    rows = math.gcd(b, WKV_ROWS)
    tok = pl.BlockSpec((rows, c, d), lambda i, j: (i, j, 0))
    par = pl.BlockSpec((SUBLANES, d), lambda i, j: (0, 0))
    tok_bytes = _nbytes((rows, c, d), F32)
    sq_bytes = _nbytes((2 * c, 2 * c), F32)
    n_tiles = d // LANES
    return pl.pallas_call(
        functools.partial(_wkv_kernel, passes=passes),
        grid=(b // rows, s // c),
        in_specs=[tok] * 6 + [par],
        out_specs=tok,
        out_shape=jax.ShapeDtypeStruct((b, s, d), BF16),
        scratch_shapes=[pltpu.VMEM((rows, n_tiles, LANES, LANES), F32)],
        compiler_params=pltpu.CompilerParams(
            dimension_semantics=("parallel", "arbitrary"),
            vmem_limit_bytes=_vmem_limit(8 * tok_bytes, rows * n_tiles * sq_bytes, rows * n_tiles * 48 * sq_bytes)),
        name="wkv",
    )(r, lw, k, v, a, g, pvec)


def _conv_in_kernel(h_ref, g_ref, win_ref, cw_ref, z_out, carry_ref):
    @pl.when(pl.program_id(1) == 0)
    def _():
        carry_ref[...] = jnp.zeros_like(carry_ref)

    ts, d = h_ref.shape
    u = _rms(h_ref[...], g_ref[...]).astype(BF16)
    z = _dot(u, win_ref[:, d:2 * d]) * _dot(u, win_ref[:, 2 * d:])
    b_gate = _dot(u, win_ref[:, :d])
    carry = carry_ref[...]
    z1 = _shift_rows(z, carry, 1)
    z2 = _shift_rows(z, carry, 2)
    carry_ref[...] = z[ts - SUBLANES:, :]
    zc = z2 * cw_ref[0:1, :] + z1 * cw_ref[1:2, :] + z * cw_ref[2:3, :]
    z_out[...] = (b_gate * zc).astype(BF16)


def _conv_in(h, g, w_in, cw8):
    b, s, d = h.shape
    ts = min(CONV_TS, s)
    tok = pl.BlockSpec((None, ts, d), lambda i, j: (i, j, 0))
    tok_bytes = _nbytes((ts, d), F32)
    block_bytes = 2 * tok_bytes + _nbytes(w_in.shape, BF16) + _nbytes(cw8.shape, F32)
    return pl.pallas_call(
        _conv_in_kernel,
        grid=(b, s // ts),
        in_specs=[tok,
                  pl.BlockSpec(g.shape, lambda i, j: (0, 0)),
                  pl.BlockSpec(w_in.shape, lambda i, j: (0, 0)),
                  pl.BlockSpec(cw8.shape, lambda i, j: (0, 0))],
        out_specs=tok,
        out_shape=jax.ShapeDtypeStruct((b, s, d), BF16),
        scratch_shapes=[pltpu.VMEM((SUBLANES, d), F32)],
        compiler_params=pltpu.CompilerParams(
            dimension_semantics=("parallel", "arbitrary"),
            vmem_limit_bytes=_vmem_limit(block_bytes, _nbytes((SUBLANES, d), F32), 10 * tok_bytes)),
        name="conv_in",
    )(h, g, w_in, cw8)


def _sb_qkv_kernel(h_ref, g_ref, w_ref, qg_ref, kg_ref, q_out, k_out, v_out):
    d = h_ref.shape[1]
    u = _rms(h_ref[...], g_ref[...]).astype(BF16)
    hsum = _head_sum_matrix(2 * LANES)
    inv_n = 1.0 / HEAD_DIM

    def head_norm(x, gain):
        cols = []
        for j in range(0, d, 2 * LANES):
            xs = x[:, j:j + 2 * LANES]
            ms = _dot_exact_rhs(xs * xs, hsum, QKV_SUM_PIECES) * inv_n
            cols.append(xs * lax.rsqrt(ms + RMS_EPS))
        return jnp.concatenate(cols, axis=1) * gain

    q = _dot(u, w_ref[:, :d])
    k = _dot(u, w_ref[:, d:2 * d])
    q_out[...] = head_norm(q, qg_ref[...]).astype(BF16)
    v = _dot(u, w_ref[:, 2 * d:])
    k_out[...] = head_norm(k, kg_ref[...]).astype(BF16)
    v_out[...] = v.astype(BF16)


def _sb_qkv(h2, g, w_qkv, qg, kg):
    t, d = h2.shape
    tm = min(QKV_TM, t)
    tok = pl.BlockSpec((tm, d), lambda i: (i, 0))
    vec = pl.BlockSpec((1, d), lambda i: (0, 0))
    tok_bytes = _nbytes((tm, d), F32)
    block_bytes = 4 * tok_bytes + _nbytes(w_qkv.shape, BF16) + 3 * _nbytes((1, d), F32)
    out = jax.ShapeDtypeStruct((t, d), BF16)
    return pl.pallas_call(
        _sb_qkv_kernel,
        grid=(t // tm,),
        in_specs=[tok, vec, pl.BlockSpec(w_qkv.shape, lambda i: (0, 0)), vec, vec],
        out_specs=[tok] * 3,
        out_shape=[out] * 3,
        compiler_params=pltpu.CompilerParams(
            dimension_semantics=("parallel",),
            vmem_limit_bytes=_vmem_limit(block_bytes, 0, 10 * tok_bytes)),
        name="sb_qkv",
    )(h2, g, w_qkv, qg, kg)


def _sb_attn_kernel(q_ref, k_ref, v_ref, suffix_ref, o_ref,
                    qs_ref, acc_ref, run_ref, za_ref, zb_ref, att_ref):
    tq = q_ref.shape[0]
    n_lt = q_ref.shape[1] // LANES
    qb = pl.program_id(2)
    rows = lax.broadcasted_iota(jnp.int32, (2 * tq, LANES), 0)
    lanes = lax.broadcasted_iota(jnp.int32, (2 * tq, LANES), 1)
    own = (rows >> (tq.bit_length() - 1)) == (lanes >> HEAD_SHIFT)
    lts = [slice(p * LANES, (p + 1) * LANES) for p in range(n_lt)]
    for p, sl in enumerate(lts):
        q = q_ref[:, sl].astype(F32)
        qs_ref[p] = (jnp.where(own, jnp.concatenate([q, q], axis=0), 0.0) * SB_SCALE).astype(BF16)
    suffix = suffix_ref[...]

    groups = [slice(g * SB_ROWS, (g + 1) * SB_ROWS) for g in range(2 * tq // SB_ROWS)]
    units = [(p, g) for p in range(n_lt) for g in groups]

    def each(f, *cols):
        return [f(*args) for args in zip(*cols)]

    def key_tile(ref, j, p):
        return ref[pl.ds(pl.multiple_of(j * tq, tq), tq), lts[p]]

    def scores(j, z_out):
        for p in range(n_lt):
            ks = key_tile(k_ref, j, p)
            for g in groups:
                z_out[p, g, :] = _dot_nt(qs_ref[p, g, :], ks)

    def add_values(j):
        pv = []
        for p in range(n_lt):
            vs = key_tile(v_ref, j, p)
            pv += [_dot(att_ref[p, g, :], vs) for g in groups]
        return pv

    def step(j, z_in, z_out, diagonal):
        if not diagonal:
            pv = add_values(j + 1)
        if diagonal:
            t_loc = lax.broadcasted_iota(jnp.int32, (SB_ROWS, tq), 0)
            s_loc = lax.broadcasted_iota(jnp.int32, (SB_ROWS, tq), 1)
            causal = [s_loc < t_loc + (g.start & (tq - 1)) for _, g in units]
        spb = []
        for i, (p, g) in enumerate(units):
            z = z_in[p, g, :]
            sp = _softplus(z)
            z_in[p, g, :] = z - sp
            if diagonal:
                sp = jnp.where(causal[i], sp, 0.0)
            spb.append(sp.astype(BF16))
        tail = each(lambda x: _dot(x, suffix), spb)
        if z_out is not None:
            scores(jnp.maximum(j - 1, 0), z_out)
        for i, (p, g) in enumerate(units):
            total = jnp.broadcast_to(tail[i][:, 0:1] + spb[i][:, 0:1].astype(F32), (SB_ROWS, LANES))
            if diagonal:
                att = jnp.where(causal[i], jnp.exp(z_in[p, g, :] - tail[i]), 0.0)
                run_ref[p, g, :] = total
                acc_ref[p, g, :] = jnp.zeros((SB_ROWS, LANES), F32)
            else:
                run = run_ref[p, g, :]
                att = jnp.exp(z_in[p, g, :] - tail[i] - jnp.concatenate([run] * (tq // LANES), axis=1))
                run_ref[p, g, :] = run + total
                acc_ref[p, g, :] += pv[i]
            att_ref[p, g, :] = att.astype(BF16)

    stop_at = SB_EXP_ZERO

    def run_min():
        return jnp.min(run_ref[...])

    scores(qb, za_ref)
    step(qb, za_ref, zb_ref, True)

    @pl.when(qb >= 1)
    def _():
        step(qb - 1, zb_ref, None, False)

    def more(state):
        j, low = state
        return jnp.logical_and(j >= 1, low < stop_at)

    def pair(state):
        j, _ = state
        scores(j, za_ref)
        step(j, za_ref, zb_ref, False)
        step(j - 1, zb_ref, None, False)
        return j - 2, run_min()

    j, low = lax.while_loop(more, pair, (jnp.maximum(qb - 2, -1), run_min()))
    last_tile = jnp.logical_and(j == 0, low < stop_at)

    @pl.when(last_tile)
    def _():
        scores(0, za_ref)
        step(0, za_ref, None, False)

    pv = add_values(jnp.where(last_tile, 0, j + 1))
    for (p, g), pvi in zip(units, pv):
        acc_ref[p, g, :] += pvi
    lane = lax.broadcasted_iota(jnp.int32, (tq, LANES), 1)
    for p, sl in enumerate(lts):
        o_ref[:, sl] = jnp.where(lane >= HEAD_DIM, acc_ref[p, tq:, :], acc_ref[p, :tq, :]).astype(BF16)


def _sb_attn(q, k, v):
    b, s, d = q.shape
    tq = SB_TQ
    n_lt = min(SB_LANE_TILES, d // LANES)
    width = n_lt * LANES
    ri = lax.broadcasted_iota(jnp.int32, (tq, tq), 0)
    ci = lax.broadcasted_iota(jnp.int32, (tq, tq), 1)
    suffix = jnp.where(ri > ci, 1.0, 0.0).astype(BF16)
    qspec = pl.BlockSpec((None, tq, width), lambda i, p, j: (i, j, p))
    kspec = pl.BlockSpec((None, s, width), lambda i, p, j: (i, 0, p))
    block_bytes = 2 * _nbytes((tq, width), BF16) + 2 * _nbytes((s, width), BF16) + _nbytes((tq, tq), BF16)
    scratch = [pltpu.VMEM((n_lt, 2 * tq, LANES), BF16),
               pltpu.VMEM((n_lt, 2 * tq, LANES), F32),
               pltpu.VMEM((n_lt, 2 * tq, LANES), F32),
               pltpu.VMEM((n_lt, 2 * tq, tq), F32),
               pltpu.VMEM((n_lt, 2 * tq, tq), F32),
               pltpu.VMEM((n_lt, 2 * tq, tq), BF16)]
    scratch_bytes = n_lt * (_nbytes((2 * tq, LANES), BF16) + 2 * _nbytes((2 * tq, LANES), F32)
                            + 2 * _nbytes((2 * tq, tq), F32) + _nbytes((2 * tq, tq), BF16))
    return pl.pallas_call(
        _sb_attn_kernel,
        grid=(b, d // width, s // tq),
        in_specs=[qspec, kspec, kspec, pl.BlockSpec((tq, tq), lambda i, p, j: (0, 0))],
        out_specs=qspec,
        out_shape=jax.ShapeDtypeStruct((b, s, d), BF16),
        scratch_shapes=scratch,
        compiler_params=pltpu.CompilerParams(
            dimension_semantics=("parallel", "parallel", "arbitrary"),
            vmem_limit_bytes=_vmem_limit(block_bytes, scratch_bytes, 12 * _nbytes((2 * tq, tq), F32))),
        name="sb_attn",
    )(q, k, v, suffix)


def _outproj_mlp_kernel(h_ref, z_ref, wo_ref, g_ref, wup_ref, wdn_ref, o_ref, xn_ref):
    @pl.when(pl.program_id(1) == 0)
    def _():
        h1 = h_ref[...] + _dot(z_ref[...], wo_ref[...])
        o_ref[...] = h1
        xn_ref[...] = _rms(h1, g_ref[...]).astype(BF16)

    act = jnp.square(jnp.maximum(_dot(xn_ref[...], wup_ref[...]), 0.0)).astype(BF16)
    o_ref[...] += _dot(act, wdn_ref[...])


def _outproj_mlp(h2, z, w_o, g, w_up_all, w_dn_all, layer):
    t, d = h2.shape
    f = w_up_all.shape[2]
    tm = min(MLP_TM, t)
    tf = min(MLP_TF, f)
    tok = pl.BlockSpec((tm, d), lambda i, j: (i, 0))
    in_specs = [
        tok, tok,
        pl.BlockSpec((d, d), lambda i, j: (0, 0)),
        pl.BlockSpec((1, d), lambda i, j: (0, 0)),
        pl.BlockSpec((None, d, tf), lambda i, j: (layer, 0, j)),
        pl.BlockSpec((None, tf, d), lambda i, j: (layer, j, 0)),
    ]
    tok_bytes = _nbytes((tm, d), F32)
    block_bytes = 2 * tok_bytes + _nbytes((tm, d), BF16) + _nbytes((d, d), BF16) + 2 * _nbytes((d, tf), BF16)
    return pl.pallas_call(
        _outproj_mlp_kernel,
        grid=(t // tm, f // tf),
        in_specs=in_specs,
        out_specs=tok,
        out_shape=jax.ShapeDtypeStruct((t, d), F32),
        scratch_shapes=[pltpu.VMEM((tm, d), BF16)],
        compiler_params=pltpu.CompilerParams(
            dimension_semantics=("parallel", "arbitrary"),
            vmem_limit_bytes=_vmem_limit(block_bytes, _nbytes((tm, d), BF16),
                                         tok_bytes + 2 * _nbytes((tm, tf), F32))),
        name="outproj_mlp",
    )(h2, z, w_o, g, w_up_all, w_dn_all)


def _pad_cols(w, mult):
    pad = (-w.shape[1]) % mult
    return jnp.pad(w, ((0, 0), (0, pad)))


def _pad_rows(w, mult):
    pad = (-w.shape[0]) % mult
    return jnp.pad(w, ((0, pad), (0, 0)))


def _rows8(*rows):
    d = rows[0].shape[-1]
    pad = jnp.zeros((SUBLANES - len(rows), d), F32)
    return jnp.concatenate([r.reshape(1, d).astype(F32) for r in rows] + [pad], axis=0)


def _lora(w_in, w_out):
    return _pad_cols(w_in, LANES).astype(BF16), _pad_rows(w_out, LANES).astype(BF16)


def kernel(x, mix_norm, mlp_norm, mlp_up, mlp_down, rwkv_mu, rwkv_w_r, rwkv_w_k, rwkv_w_v, rwkv_w_o, rwkv_decay_w0, rwkv_decay_w1, rwkv_decay_w2, rwkv_iclr_a0, rwkv_iclr_a1, rwkv_iclr_a2, rwkv_gate_g1, rwkv_gate_g2, rwkv_k_k, rwkv_k_a, rwkv_r_k, rwkv_lnx_w, rwkv_lnx_b, rwkv_vres_v0, rwkv_vres_v1, rwkv_vres_v2, conv_w_in, conv_w, conv_w_out, sb_w_qkv, sb_q_norm, sb_k_norm, sb_w_o):
    b, s, d = x.shape
    depth = mix_norm.shape[0]
    n_heads = d // HEAD_DIM
    mlp_up_bf = mlp_up.astype(BF16)
    mlp_down_bf = mlp_down.astype(BF16)
    h = x
    v_first = None
    for i in range(depth):
        kind = i % 3
        j = i // 3
        g_mix = mix_norm[i].reshape(1, d)
        if kind == 0:
            vres = None
            v0 = jnp.zeros((d,), F32)
            if j > 0:
                v1, v2 = _lora(rwkv_vres_v1[j - 1], rwkv_vres_v2[j - 1])
                vres = (v1, v2, v_first)
                v0 = rwkv_vres_v0[j - 1]
            vecs = _rows8(mix_norm[i], rwkv_decay_w0[j], rwkv_iclr_a0[j], v0)
            mu8 = _rows8(*[rwkv_mu[j, m] for m in range(rwkv_mu.shape[1])])
            w1, w2 = _lora(rwkv_decay_w1[j], rwkv_decay_w2[j])
            a1, a2 = _lora(rwkv_iclr_a1[j], rwkv_iclr_a2[j])
            g1, g2 = _lora(rwkv_gate_g1[j], rwkv_gate_g2[j])
            r, k, v, lw, a, gate = _rwkv_in(
                h, vecs, mu8, rwkv_w_r[j].astype(BF16), rwkv_w_k[j].astype(BF16),
                rwkv_w_v[j].astype(BF16), w1, w2, a1, a2, g1, g2, vres)
            if j == 0:
                v_first = v
            pvec = _rows8(rwkv_k_k[j], rwkv_k_a[j], rwkv_r_k[j].reshape(d), rwkv_lnx_w[j], rwkv_lnx_b[j])
            z = _wkv(r, lw, k, v, a, gate, pvec, WKV_PASSES)
            w_o = rwkv_w_o[j]
        elif kind == 1:
            z = _conv_in(h, g_mix, conv_w_in[j].astype(BF16), _rows8(*[conv_w[j, m] for m in range(3)]))
            w_o = conv_w_out[j]
        else:
            qg = jnp.tile(sb_q_norm[j], n_heads).reshape(1, d)
            kg = jnp.tile(sb_k_norm[j], n_heads).reshape(1, d)
            q, k, v = _sb_qkv(h.reshape(b * s, d), g_mix, sb_w_qkv[j].astype(BF16), qg, kg)
            z = _sb_attn(q.reshape(b, s, d), k.reshape(b, s, d), v.reshape(b, s, d))
            w_o = sb_w_o[j]
        h = _outproj_mlp(
            h.reshape(b * s, d), z.reshape(b * s, d), w_o.astype(BF16), mlp_norm[i].reshape(1, d), mlp_up_bf, mlp_down_bf, i,
        ).reshape(b, s, d)
    return h
```

```python
import functools
import math

import jax
import jax.numpy as jnp
from jax import lax
from jax.experimental import pallas as pl
from jax.experimental.pallas import tpu as pltpu

F32 = jnp.float32
BF16 = jnp.bfloat16

HEAD_DIM = 64
HEAD_SHIFT = 6
LANES = 128
SUBLANES = 8
RMS_EPS = 1e-6
GN_EPS = 64e-5
KK_EPS = 1e-24
SB_SCALE = HEAD_DIM ** -0.5
SB_EXP_ZERO = 105.0

V7X_VMEM_BYTES = 64 * 1024 * 1024
VMEM_LIMIT_CAP = V7X_VMEM_BYTES - 8 * 1024 * 1024

MLP_TM = 1024
MLP_TF = 2048
RWKV_TS = 256
CONV_TS = 1024
QKV_TM = 1024
QKV_SUM_PIECES = 1
WKV_CHUNK = 64
WKV_ROWS = 8
WKV_GROUP_ROWS = 2
WKV_PASSES = 1
WKV_CUM_PIECES = 2
SB_TQ = 256
SB_ROWS = 256
SB_LANE_TILES = 8


def _vmem_limit(block_bytes, scratch_bytes, temp_bytes):
    return int(min(2 * block_bytes + scratch_bytes + temp_bytes, VMEM_LIMIT_CAP))


def _nbytes(shape, dtype):
    n = 1
    for s in shape:
        n *= s
    return n * jnp.dtype(dtype).itemsize


def _dot(a, b):
    return jnp.dot(a, b, preferred_element_type=F32)


def _dot_nt(a, b):
    return lax.dot_general(a, b, (((1,), (1,)), ((), ())), preferred_element_type=F32)


def _dot_tn(a, b):
    return lax.dot_general(a, b, (((0,), (0,)), ((), ())), preferred_element_type=F32)


def _split(x, pieces):
    out = []
    for _ in range(pieces - 1):
        hi = x.astype(BF16)
        out.append(hi)
        x = x - hi.astype(F32)
    out.append(x.astype(BF16))
    return out


def _dot_exact_rhs(x, m_bf16, pieces, dot=_dot):
    acc = None
    for p in _split(x, pieces):
        t = dot(p, m_bf16)
        acc = t if acc is None else acc + t
    return acc


def _dot_exact_lhs(m_bf16, x, pieces):
    acc = None
    for p in _split(x, pieces):
        t = _dot(m_bf16, p)
        acc = t if acc is None else acc + t
    return acc


def _mm(a, b, passes, dot=_dot):
    if passes == 1:
        return dot(a.astype(BF16), b.astype(BF16))
    ah, al = _split(a, 2)
    bh, bl = _split(b, 2)
    return dot(ah, bh) + (dot(al, bh) + dot(ah, bl))


def _rms(x, g):
    ms = jnp.mean(x * x, axis=-1, keepdims=True)
    return x * lax.rsqrt(ms + RMS_EPS) * g


def _sigmoid(x):
    return 1.0 / (1.0 + jnp.exp(-x))


def _softplus(x):
    return jnp.maximum(x, 0.0) + jnp.log(1.0 + jnp.exp(-jnp.abs(x)))


def _head_sum_matrix(n):
    r = lax.broadcasted_iota(jnp.int32, (n, n), 0) >> HEAD_SHIFT
    c = lax.broadcasted_iota(jnp.int32, (n, n), 1) >> HEAD_SHIFT
    return jnp.where(r == c, 1.0, 0.0).astype(BF16)


def _shift_rows(x, carry8, n):
    rows = lax.broadcasted_iota(jnp.int32, (x.shape[0], 1), 0)
    out = pltpu.roll(x, n, axis=0)
    for i in range(n):
        out = jnp.where(rows == i, carry8[SUBLANES - n + i:SUBLANES - n + i + 1, :], out)
    return out


def _rwkv_in_kernel(*refs, has_vres):
    if has_vres:
        (h_ref, vec_ref, mu_ref, wr_ref, wk_ref, wv_ref, w1_ref, w2_ref, a1_ref, a2_ref,
         g1_ref, g2_ref, v1_ref, v2_ref, vf_ref,
         r_out, k_out, v_out, lw_out, a_out, g_out, carry_ref) = refs
    else:
        (h_ref, vec_ref, mu_ref, wr_ref, wk_ref, wv_ref, w1_ref, w2_ref, a1_ref, a2_ref,
         g1_ref, g2_ref,
         r_out, k_out, v_out, lw_out, a_out, g_out, carry_ref) = refs

    @pl.when(pl.program_id(1) == 0)
    def _():
        carry_ref[...] = jnp.zeros_like(carry_ref)

    ts = h_ref.shape[0]
    u = _rms(h_ref[...], vec_ref[0:1, :])
    prev = _shift_rows(u, carry_ref[...], 1)
    carry_ref[...] = u[ts - SUBLANES:, :]
    xx = prev - u

    def mix(i):
        return (u + xx * mu_ref[i:i + 1, :]).astype(BF16)

    hw = jnp.tanh(_dot(mix(1), w1_ref[...])).astype(BF16)
    ha = _dot(mix(4), a1_ref[...]).astype(BF16)
    hg = _sigmoid(_dot(mix(5), g1_ref[...])).astype(BF16)
    xv = mix(3)
    if has_vres:
        hv = _dot(xv, v1_ref[...]).astype(BF16)
    dw = vec_ref[1:2, :] + _dot(hw, w2_ref[...])
    da = vec_ref[2:3, :] + _dot(ha, a2_ref[...])
    g_out[...] = _dot(hg, g2_ref[...])
    if has_vres:
        dv = vec_ref[3:4, :] + _dot(hv, v2_ref[...])

    r_out[...] = _dot(mix(0), wr_ref[...])
    lw_out[...] = -jnp.exp(-_softplus(-dw) - 0.5)
    k_out[...] = _dot(mix(2), wk_ref[...])
    a_out[...] = _sigmoid(da)
    v = _dot(xv, wv_ref[...])
    if has_vres:
        v = v + (vf_ref[...] - v) * _sigmoid(dv)
    v_out[...] = v


def _rwkv_in(h, vecs, mu8, wr, wk, wv, w1, w2, a1, a2, g1, g2, vres):
    b, s, d = h.shape
    ts = min(RWKV_TS, s)
    tok = pl.BlockSpec((None, ts, d), lambda i, j: (i, j, 0))

    def full(x):
        return pl.BlockSpec(x.shape, lambda i, j: (0,) * x.ndim)

    weights = [vecs, mu8, wr, wk, wv, w1, w2, a1, a2, g1, g2]
    args = [h] + weights
    in_specs = [tok] + [full(w) for w in weights]
    if vres is not None:
        v1, v2, v_first = vres
        args += [v1, v2, v_first]
        in_specs += [full(v1), full(v2), tok]
    tok_bytes = _nbytes((ts, d), F32)
    block_bytes = sum(_nbytes(w.shape, w.dtype) for w in args[1:1 + len(weights) + (2 if vres else 0)])
    block_bytes += tok_bytes * (7 + (1 if vres else 0))
    out = jax.ShapeDtypeStruct((b, s, d), F32)
    return pl.pallas_call(
        functools.partial(_rwkv_in_kernel, has_vres=vres is not None),
        grid=(b, s // ts),
        in_specs=in_specs,
        out_specs=[tok] * 6,
        out_shape=[out] * 6,
        scratch_shapes=[pltpu.VMEM((SUBLANES, d), F32)],
        compiler_params=pltpu.CompilerParams(
            dimension_semantics=("parallel", "arbitrary"),
            vmem_limit_bytes=_vmem_limit(block_bytes, _nbytes((SUBLANES, d), F32), 12 * tok_bytes)),
        name="rwkv_in",
    )(*args)


def _wkv_kernel(r_ref, lw_ref, k_ref, v_ref, a_ref, g_ref, pv_ref, y_ref, s_ref, *, passes):
    n_rows, c, d_model = r_ref.shape
    n = 2 * c
    n_tiles = d_model // LANES

    @pl.when(pl.program_id(1) == 0)
    def _():
        s_ref[...] = jnp.zeros_like(s_ref)

    first_head = lax.broadcasted_iota(jnp.int32, (c, LANES), 1) < HEAD_DIM

    def head_sums(x):
        lo = jnp.sum(jnp.where(first_head, x, 0.0), axis=-1, keepdims=True)
        hi = jnp.sum(jnp.where(first_head, 0.0, x), axis=-1, keepdims=True)
        return jnp.where(first_head, lo, hi)

    ti = lax.broadcasted_iota(jnp.int32, (c, c), 0)
    si = lax.broadcasted_iota(jnp.int32, (c, c), 1)
    lincl = jnp.where(si <= ti, 1.0, 0.0).astype(BF16)
    log2c = c.bit_length() - 1
    row = lax.broadcasted_iota(jnp.int32, (n, LANES), 0)
    lane = lax.broadcasted_iota(jnp.int32, (n, LANES), 1)
    own = (row >> log2c) == (lane >> HEAD_SHIFT)
    ri = lax.broadcasted_iota(jnp.int32, (n, 2 * n), 0)
    ci = lax.broadcasted_iota(jnp.int32, (n, 2 * n), 1) & (c - 1)
    tri = ci < (ri & (c - 1)) + (ri >> log2c)
    inv_n = 1.0 / HEAD_DIM
    mm = functools.partial(_mm, passes=passes)
    mm_nt = functools.partial(_mm, passes=passes, dot=_dot_nt)
    mm_tn = functools.partial(_mm, passes=passes, dot=_dot_tn)

    def stack(x):
        return jnp.where(own, jnp.concatenate([x, x], axis=0), 0.0)

    steps = log2c
    rows_per_group = WKV_GROUP_ROWS if n_rows % WKV_GROUP_ROWS == 0 else 1
    groups = [[(bi, slice(t * LANES, (t + 1) * LANES), t)
               for bi in range(g0, g0 + rows_per_group) for t in range(n_tiles)]
              for g0 in range(0, n_rows, rows_per_group)]

    def each(f, *cols):
        return [f(*args) for args in zip(*cols)]

    def pro_keys(units):
        st = {"units": units}
        for name, ref in (("r", r_ref), ("lw", lw_ref), ("k_raw", k_ref), ("v", v_ref), ("a", a_ref)):
            st[name] = [ref[bi, :, sl] for bi, sl, _ in units]

        def param(i):
            return [pv_ref[i:i + 1, sl] for _, sl, _ in units]

        kk = each(lambda x, g: x * g, st["k_raw"], param(0))
        kk_ss = each(lambda x: head_sums(x * x), kk)
        st["kk"] = each(lambda x, ss: x * lax.rsqrt(jnp.maximum(ss, KK_EPS)), kk, kk_ss)
        st["k"] = each(lambda x, ai, g: x * (1.0 + (ai - 1.0) * g), st["k_raw"], st["a"], param(1))
        st["bonus"] = each(lambda ri_, ki, g, vi: head_sums(ri_ * ki * g) * vi, st["r"], st["k"], param(2), st["v"])
        return st

    def pro_cum(st):
        st["cum"] = each(lambda x: _dot_exact_lhs(lincl, x, WKV_CUM_PIECES), st["lw"])
        return st

    def pro_scale(st):
        g_incl = each(jnp.exp, st["cum"])
        g_excl = each(lambda cs, x: jnp.exp(cs - x), st["cum"], st["lw"])
        g_inv = each(lambda cs: jnp.exp(-cs), st["cum"])
        st["g_end"] = each(lambda g: g[c - 1:c, :], g_incl)
        st["ar"] = each(lambda kki, ri_, ge, gi: jnp.concatenate([-kki * ge, ri_ * gi], axis=0),
                        st["kk"], st["r"], g_excl, g_incl)
        st["bt"] = each(lambda kki, ai, gv: kki * ai * gv, st["kk"], st["a"], g_inv)
        st["kt"] = each(lambda ki, gv: ki * gv, st["k"], g_inv)
        return st

    def pro_stack(st):
        st["bks"] = each(lambda b_, k_: jnp.concatenate([stack(b_), stack(k_)], axis=0), st["bt"], st["kt"])
        st["vs"] = each(stack, st["v"])
        return st

    def pro_products(st):
        st["s0"] = [s_ref[bi, t] for bi, _, t in st["units"]]
        st["prod"] = each(lambda x, y: jnp.where(tri, mm_nt(x, y), 0.0), st["ar"], st["bks"])
        st["from_s0"] = each(mm_nt, st["ar"], st["s0"])
        return st

    pro_pieces = [pro_keys, pro_cum, pro_scale, pro_stack, pro_products]
    assert len(pro_pieces) <= steps

    def solve_start(st):
        x = each(lambda fs, pr, vs: fs[:c, :] + mm(pr[:c, n:], vs), st["from_s0"], st["prod"], st["vs"])
        return x, [pr[:c, :n] for pr in st["prod"]]

    def solve_stage(x, p, last):
        if last:
            return each(lambda xi, pi: xi + mm(pi, stack(xi)), x, p), None
        px = each(lambda xi, pi: mm(pi, jnp.concatenate([stack(xi), stack(pi)], axis=1)), x, p)
        return each(lambda xi, pxi: xi + pxi[:, :LANES], x, px), each(lambda pxi: pxi[:, LANES:], px)

    def finish(st, x):
        st["y"] = each(lambda fs, pr, vs, u: fs[c:, :] + mm(pr[c:, :], jnp.concatenate([stack(u), vs], axis=0)),
                       st["from_s0"], st["prod"], st["vs"], x)
        outer = each(lambda u, v, bt, kt, ge: mm_tn(jnp.concatenate([u, v], axis=0),
                                                    jnp.concatenate([bt * ge, kt * ge], axis=0)),
                     x, st["v"], st["bt"], st["kt"], st["g_end"])
        for (bi, _, t), s0, ge, o in zip(st["units"], st["s0"], st["g_end"], outer):
            s_ref[bi, t] = s0 * ge + jnp.where(own, o, 0.0)

    def epilogue(st, lo, hi):
        y = st["y"][lo:hi]
        mean = each(lambda yi: head_sums(yi) * inv_n, y)
        yc = each(lambda yi, m: yi - m, y, mean)
        var = each(lambda yi: head_sums(yi * yi) * inv_n, yc)
        for (bi, sl, _), bonus, yi, vr in zip(st["units"][lo:hi], st["bonus"][lo:hi], yc, var):
            out = yi * lax.rsqrt(vr + GN_EPS) * pv_ref[3:4, sl] + pv_ref[4:5, sl] + bonus
            y_ref[bi, :, sl] = (out * g_ref[bi, :, sl]).astype(BF16)

    cur = groups[0]
    for piece in pro_pieces:
        cur = piece(cur)
    prev = None
    for gi in range(len(groups)):
        x, p = solve_start(cur)
        nxt = groups[gi + 1] if gi + 1 < len(groups) else None
        n_units = len(groups[gi])
        for i in range(steps):
            x, p = solve_stage(x, p, i == steps - 1)
            if nxt is not None and i < len(pro_pieces):
                nxt = pro_pieces[i](nxt)
            if prev is not None:
                epilogue(prev, i * n_units // steps, (i + 1) * n_units // steps)
        finish(cur, x)
        prev, cur = cur, nxt
    epilogue(prev, 0, len(prev["units"]))


def _wkv(r, lw, k, v, a, g, pvec, passes):
    b, s, d = r.shape
    c = WKV_CHUNK
    rows = math.gcd(b, WKV_ROWS)
    tok = pl.BlockSpec((rows, c, d), lambda i, j: (i, j, 0))
    par = pl.BlockSpec((SUBLANES, d), lambda i, j: (0, 0))
    tok_bytes = _nbytes((rows, c, d), F32)
    sq_bytes = _nbytes((2 * c, 2 * c), F32)
    n_tiles = d // LANES
    return pl.pallas_call(
        functools.partial(_wkv_kernel, passes=passes),
        grid=(b // rows, s // c),
        in_specs=[tok] * 6 + [par],
        out_specs=tok,
        out_shape=jax.ShapeDtypeStruct((b, s, d), BF16),
        scratch_shapes=[pltpu.VMEM((rows, n_tiles, LANES, LANES), F32)],
        compiler_params=pltpu.CompilerParams(
            dimension_semantics=("parallel", "arbitrary"),
            vmem_limit_bytes=_vmem_limit(8 * tok_bytes, rows * n_tiles * sq_bytes, rows * n_tiles * 48 * sq_bytes)),
        name="wkv",
    )(r, lw, k, v, a, g, pvec)


def _conv_in_kernel(h_ref, g_ref, win_ref, cw_ref, z_out, carry_ref):
    @pl.when(pl.program_id(1) == 0)
    def _():
        carry_ref[...] = jnp.zeros_like(carry_ref)

    ts, d = h_ref.shape
    u = _rms(h_ref[...], g_ref[...]).astype(BF16)
    z = _dot(u, win_ref[:, d:2 * d]) * _dot(u, win_ref[:, 2 * d:])
    b_gate = _dot(u, win_ref[:, :d])
    carry = carry_ref[...]
    z1 = _shift_rows(z, carry, 1)
    z2 = _shift_rows(z, carry, 2)
    carry_ref[...] = z[ts - SUBLANES:, :]
    zc = z2 * cw_ref[0:1, :] + z1 * cw_ref[1:2, :] + z * cw_ref[2:3, :]
    z_out[...] = (b_gate * zc).astype(BF16)


def _conv_in(h, g, w_in, cw8):
    b, s, d = h.shape
    ts = min(CONV_TS, s)
    tok = pl.BlockSpec((None, ts, d), lambda i, j: (i, j, 0))
    tok_bytes = _nbytes((ts, d), F32)
    block_bytes = 2 * tok_bytes + _nbytes(w_in.shape, BF16) + _nbytes(cw8.shape, F32)
    return pl.pallas_call(
        _conv_in_kernel,
        grid=(b, s // ts),
        in_specs=[tok,
                  pl.BlockSpec(g.shape, lambda i, j: (0, 0)),
                  pl.BlockSpec(w_in.shape, lambda i, j: (0, 0)),
                  pl.BlockSpec(cw8.shape, lambda i, j: (0, 0))],
        out_specs=tok,
        out_shape=jax.ShapeDtypeStruct((b, s, d), BF16),
        scratch_shapes=[pltpu.VMEM((SUBLANES, d), F32)],
        compiler_params=pltpu.CompilerParams(
            dimension_semantics=("parallel", "arbitrary"),
            vmem_limit_bytes=_vmem_limit(block_bytes, _nbytes((SUBLANES, d), F32), 10 * tok_bytes)),
        name="conv_in",
    )(h, g, w_in, cw8)


def _sb_qkv_kernel(h_ref, g_ref, w_ref, qg_ref, kg_ref, q_out, k_out, v_out):
    d = h_ref.shape[1]
    u = _rms(h_ref[...], g_ref[...]).astype(BF16)
    hsum = _head_sum_matrix(2 * LANES)
    inv_n = 1.0 / HEAD_DIM

    def head_norm(x, gain):
        cols = []
        for j in range(0, d, 2 * LANES):
            xs = x[:, j:j + 2 * LANES]
            ms = _dot_exact_rhs(xs * xs, hsum, QKV_SUM_PIECES) * inv_n
            cols.append(xs * lax.rsqrt(ms + RMS_EPS))
        return jnp.concatenate(cols, axis=1) * gain

    q = _dot(u, w_ref[:, :d])
    k = _dot(u, w_ref[:, d:2 * d])
    q_out[...] = head_norm(q, qg_ref[...]).astype(BF16)
    v = _dot(u, w_ref[:, 2 * d:])
    k_out[...] = head_norm(k, kg_ref[...]).astype(BF16)
    v_out[...] = v.astype(BF16)


def _sb_qkv(h2, g, w_qkv, qg, kg):
    t, d = h2.shape
    tm = min(QKV_TM, t)
    tok = pl.BlockSpec((tm, d), lambda i: (i, 0))
    vec = pl.BlockSpec((1, d), lambda i: (0, 0))
    tok_bytes = _nbytes((tm, d), F32)
    block_bytes = 4 * tok_bytes + _nbytes(w_qkv.shape, BF16) + 3 * _nbytes((1, d), F32)
    out = jax.ShapeDtypeStruct((t, d), BF16)
    return pl.pallas_call(
        _sb_qkv_kernel,
        grid=(t // tm,),
        in_specs=[tok, vec, pl.BlockSpec(w_qkv.shape, lambda i: (0, 0)), vec, vec],
        out_specs=[tok] * 3,
        out_shape=[out] * 3,
        compiler_params=pltpu.CompilerParams(
            dimension_semantics=("parallel",),
            vmem_limit_bytes=_vmem_limit(block_bytes, 0, 10 * tok_bytes)),
        name="sb_qkv",
    )(h2, g, w_qkv, qg, kg)


def _sb_attn_kernel(q_ref, k_ref, v_ref, suffix_ref, o_ref,
                    qs_ref, acc_ref, run_ref, za_ref, zb_ref, att_ref, spb_ref, tail_ref):
    tq = q_ref.shape[0]
    n_lt = q_ref.shape[1] // LANES
    qb = pl.program_id(2)
    rows = lax.broadcasted_iota(jnp.int32, (2 * tq, LANES), 0)
    lanes = lax.broadcasted_iota(jnp.int32, (2 * tq, LANES), 1)
    own = (rows >> (tq.bit_length() - 1)) == (lanes >> HEAD_SHIFT)
    lts = [slice(p * LANES, (p + 1) * LANES) for p in range(n_lt)]
    for p, sl in enumerate(lts):
        q = q_ref[:, sl].astype(F32)
        qs_ref[p] = (jnp.where(own, jnp.concatenate([q, q], axis=0), 0.0) * SB_SCALE).astype(BF16)
    suffix = suffix_ref[...]

    groups = [slice(g * SB_ROWS, (g + 1) * SB_ROWS) for g in range(2 * tq // SB_ROWS)]
    units = [(p, g) for p in range(n_lt) for g in groups]

    def each(f, *cols):
        return [f(*args) for args in zip(*cols)]

    def key_tile(ref, j, p):
        return ref[pl.ds(pl.multiple_of(j * tq, tq), tq), lts[p]]

    def scores(j, z_out):
        for p in range(n_lt):
            ks = key_tile(k_ref, j, p)
            for g in groups:
                z_out[p, g, :] = _dot_nt(qs_ref[p, g, :], ks)

    def add_values(j):
        pv = []
        for p in range(n_lt):
            vs = key_tile(v_ref, j, p)
            pv += [_dot(att_ref[p, g, :], vs) for g in groups]
        return pv

    def step(j, z_in, z_out, diagonal):
        def causal(g):
            t_loc = lax.broadcasted_iota(jnp.int32, (SB_ROWS, tq), 0)
            s_loc = lax.broadcasted_iota(jnp.int32, (SB_ROWS, tq), 1)
            return s_loc < t_loc + (g.start & (tq - 1))

        def softplus_part(us):
            for p, g in us:
                z = z_in[p, g, :]
                sp = _softplus(z)
                z_in[p, g, :] = z - sp
                if diagonal:
                    sp = jnp.where(causal(g), sp, 0.0)
                spb_ref[p, g, :] = sp.astype(BF16)

        def tail_part(us):
            for p, g in us:
                tail_ref[p, g, :] = _dot(spb_ref[p, g, :], suffix)

        def weights_part(us):
            for p, g in us:
                tail = tail_ref[p, g, :]
                total = jnp.broadcast_to(tail[:, 0:1] + spb_ref[p, g, 0:1].astype(F32), (SB_ROWS, LANES))
                if diagonal:
                    att = jnp.where(causal(g), jnp.exp(z_in[p, g, :] - tail), 0.0)
                    run_ref[p, g, :] = total
                    acc_ref[p, g, :] = jnp.zeros((SB_ROWS, LANES), F32)
                else:
                    run = run_ref[p, g, :]
                    att = jnp.exp(z_in[p, g, :] - tail - jnp.concatenate([run] * (tq // LANES), axis=1))
                    run_ref[p, g, :] = run + total
                att_ref[p, g, :] = att.astype(BF16)

        first, second = units[:len(units) // 2], units[len(units) // 2:]
        region = pl.when(qb >= 0)

        @region
        def _():
            if not diagonal:
                for (p, g), pvi in zip(units, add_values(j + 1)):
                    acc_ref[p, g, :] += pvi
            softplus_part(first)

        @region
        def _():
            tail_part(first)
            softplus_part(second)

        @region
        def _():
            tail_part(second)
            if z_out is not None:
                scores(jnp.maximum(j - 1, 0), z_out)
            weights_part(first)

        @region
        def _():
            weights_part(second)

    stop_at = SB_EXP_ZERO

    def run_min():
        return jnp.min(run_ref[...])

    scores(qb, za_ref)
    step(qb, za_ref, zb_ref, True)

    @pl.when(qb >= 1)
    def _():
        step(qb - 1, zb_ref, None, False)

    def more(state):
        j, low = state
        return jnp.logical_and(j >= 1, low < stop_at)

    def pair(state):
        j, _ = state
        scores(j, za_ref)
        step(j, za_ref, zb_ref, False)
        step(j - 1, zb_ref, None, False)
        return j - 2, run_min()

    j, low = lax.while_loop(more, pair, (jnp.maximum(qb - 2, -1), run_min()))
    last_tile = jnp.logical_and(j == 0, low < stop_at)

    @pl.when(last_tile)
    def _():
        scores(0, za_ref)
        step(0, za_ref, None, False)

    pv = add_values(jnp.where(last_tile, 0, j + 1))
    for (p, g), pvi in zip(units, pv):
        acc_ref[p, g, :] += pvi
    lane = lax.broadcasted_iota(jnp.int32, (tq, LANES), 1)
    for p, sl in enumerate(lts):
        o_ref[:, sl] = jnp.where(lane >= HEAD_DIM, acc_ref[p, tq:, :], acc_ref[p, :tq, :]).astype(BF16)


def _sb_attn(q, k, v):
    b, s, d = q.shape
    tq = SB_TQ
    n_lt = min(SB_LANE_TILES, d // LANES)
    width = n_lt * LANES
    ri = lax.broadcasted_iota(jnp.int32, (tq, tq), 0)
    ci = lax.broadcasted_iota(jnp.int32, (tq, tq), 1)
    suffix = jnp.where(ri > ci, 1.0, 0.0).astype(BF16)
    qspec = pl.BlockSpec((None, tq, width), lambda i, p, j: (i, j, p))
    kspec = pl.BlockSpec((None, s, width), lambda i, p, j: (i, 0, p))
    block_bytes = 2 * _nbytes((tq, width), BF16) + 2 * _nbytes((s, width), BF16) + _nbytes((tq, tq), BF16)
    scratch = [pltpu.VMEM((n_lt, 2 * tq, LANES), BF16),
               pltpu.VMEM((n_lt, 2 * tq, LANES), F32),
               pltpu.VMEM((n_lt, 2 * tq, LANES), F32),
               pltpu.VMEM((n_lt, 2 * tq, tq), F32),
               pltpu.VMEM((n_lt, 2 * tq, tq), F32),
               pltpu.VMEM((n_lt, 2 * tq, tq), BF16),
               pltpu.VMEM((n_lt, 2 * tq, tq), BF16),
               pltpu.VMEM((n_lt, 2 * tq, tq), F32)]
    scratch_bytes = n_lt * (_nbytes((2 * tq, LANES), BF16) + 2 * _nbytes((2 * tq, LANES), F32)
                            + 3 * _nbytes((2 * tq, tq), F32) + 2 * _nbytes((2 * tq, tq), BF16))
    return pl.pallas_call(
        _sb_attn_kernel,
        grid=(b, d // width, s // tq),
        in_specs=[qspec, kspec, kspec, pl.BlockSpec((tq, tq), lambda i, p, j: (0, 0))],
        out_specs=qspec,
        out_shape=jax.ShapeDtypeStruct((b, s, d), BF16),
        scratch_shapes=scratch,
        compiler_params=pltpu.CompilerParams(
            dimension_semantics=("parallel", "parallel", "arbitrary"),
            vmem_limit_bytes=_vmem_limit(block_bytes, scratch_bytes, 12 * _nbytes((2 * tq, tq), F32))),
        name="sb_attn",
    )(q, k, v, suffix)


def _outproj_mlp_kernel(h_ref, z_ref, wo_ref, g_ref, wup_ref, wdn_ref, o_ref, xn_ref):
    @pl.when(pl.program_id(1) == 0)
    def _():
        h1 = h_ref[...] + _dot(z_ref[...], wo_ref[...])
        o_ref[...] = h1
        xn_ref[...] = _rms(h1, g_ref[...]).astype(BF16)

    act = jnp.square(jnp.maximum(_dot(xn_ref[...], wup_ref[...]), 0.0)).astype(BF16)
    o_ref[...] += _dot(act, wdn_ref[...])


def _outproj_mlp(h2, z, w_o, g, w_up_all, w_dn_all, layer):
    t, d = h2.shape
    f = w_up_all.shape[2]
    tm = min(MLP_TM, t)
    tf = min(MLP_TF, f)
    tok = pl.BlockSpec((tm, d), lambda i, j: (i, 0))
    in_specs = [
        tok, tok,
        pl.BlockSpec((d, d), lambda i, j: (0, 0)),
        pl.BlockSpec((1, d), lambda i, j: (0, 0)),
        pl.BlockSpec((None, d, tf), lambda i, j: (layer, 0, j)),
        pl.BlockSpec((None, tf, d), lambda i, j: (layer, j, 0)),
    ]
    tok_bytes = _nbytes((tm, d), F32)
    block_bytes = 2 * tok_bytes + _nbytes((tm, d), BF16) + _nbytes((d, d), BF16) + 2 * _nbytes((d, tf), BF16)
    return pl.pallas_call(
        _outproj_mlp_kernel,
        grid=(t // tm, f // tf),
        in_specs=in_specs,
        out_specs=tok,
        out_shape=jax.ShapeDtypeStruct((t, d), F32),
        scratch_shapes=[pltpu.VMEM((tm, d), BF16)],
        compiler_params=pltpu.CompilerParams(
            dimension_semantics=("parallel", "arbitrary"),
            vmem_limit_bytes=_vmem_limit(block_bytes, _nbytes((tm, d), BF16),
                                         tok_bytes + 2 * _nbytes((tm, tf), F32))),
        name="outproj_mlp",
    )(h2, z, w_o, g, w_up_all, w_dn_all)


def _pad_cols(w, mult):
    pad = (-w.shape[1]) % mult
    return jnp.pad(w, ((0, 0), (0, pad)))


def _pad_rows(w, mult):
    pad = (-w.shape[0]) % mult
    return jnp.pad(w, ((0, pad), (0, 0)))


def _rows8(*rows):
    d = rows[0].shape[-1]
    pad = jnp.zeros((SUBLANES - len(rows), d), F32)
    return jnp.concatenate([r.reshape(1, d).astype(F32) for r in rows] + [pad], axis=0)


def _lora(w_in, w_out):
    return _pad_cols(w_in, LANES).astype(BF16), _pad_rows(w_out, LANES).astype(BF16)


def kernel(x, mix_norm, mlp_norm, mlp_up, mlp_down, rwkv_mu, rwkv_w_r, rwkv_w_k, rwkv_w_v, rwkv_w_o, rwkv_decay_w0, rwkv_decay_w1, rwkv_decay_w2, rwkv_iclr_a0, rwkv_iclr_a1, rwkv_iclr_a2, rwkv_gate_g1, rwkv_gate_g2, rwkv_k_k, rwkv_k_a, rwkv_r_k, rwkv_lnx_w, rwkv_lnx_b, rwkv_vres_v0, rwkv_vres_v1, rwkv_vres_v2, conv_w_in, conv_w, conv_w_out, sb_w_qkv, sb_q_norm, sb_k_norm, sb_w_o):
    b, s, d = x.shape
    depth = mix_norm.shape[0]
    n_heads = d // HEAD_DIM
    mlp_up_bf = mlp_up.astype(BF16)
    mlp_down_bf = mlp_down.astype(BF16)
    h = x
    v_first = None
    for i in range(depth):
        kind = i % 3
        j = i // 3
        g_mix = mix_norm[i].reshape(1, d)
        if kind == 0:
            vres = None
            v0 = jnp.zeros((d,), F32)
            if j > 0:
                v1, v2 = _lora(rwkv_vres_v1[j - 1], rwkv_vres_v2[j - 1])
                vres = (v1, v2, v_first)
                v0 = rwkv_vres_v0[j - 1]
            vecs = _rows8(mix_norm[i], rwkv_decay_w0[j], rwkv_iclr_a0[j], v0)
            mu8 = _rows8(*[rwkv_mu[j, m] for m in range(rwkv_mu.shape[1])])
            w1, w2 = _lora(rwkv_decay_w1[j], rwkv_decay_w2[j])
            a1, a2 = _lora(rwkv_iclr_a1[j], rwkv_iclr_a2[j])
            g1, g2 = _lora(rwkv_gate_g1[j], rwkv_gate_g2[j])
            r, k, v, lw, a, gate = _rwkv_in(
                h, vecs, mu8, rwkv_w_r[j].astype(BF16), rwkv_w_k[j].astype(BF16),
                rwkv_w_v[j].astype(BF16), w1, w2, a1, a2, g1, g2, vres)
            if j == 0:
                v_first = v
            pvec = _rows8(rwkv_k_k[j], rwkv_k_a[j], rwkv_r_k[j].reshape(d), rwkv_lnx_w[j], rwkv_lnx_b[j])
            z = _wkv(r, lw, k, v, a, gate, pvec, WKV_PASSES)
            w_o = rwkv_w_o[j]
        elif kind == 1:
            z = _conv_in(h, g_mix, conv_w_in[j].astype(BF16), _rows8(*[conv_w[j, m] for m in range(3)]))
            w_o = conv_w_out[j]
        else:
            qg = jnp.tile(sb_q_norm[j], n_heads).reshape(1, d)
            kg = jnp.tile(sb_k_norm[j], n_heads).reshape(1, d)
            q, k, v = _sb_qkv(h.reshape(b * s, d), g_mix, sb_w_qkv[j].astype(BF16), qg, kg)
            z = _sb_attn(q.reshape(b, s, d), k.reshape(b, s, d), v.reshape(b, s, d))
            w_o = sb_w_o[j]
        h = _outproj_mlp(
            h.reshape(b * s, d), z.reshape(b * s, d), w_o.astype(BF16), mlp_norm[i].reshape(1, d), mlp_up_bf, mlp_down_bf, i,
        ).reshape(b, s, d)
    return h
```

```python
import functools
import math

import jax
import jax.numpy as jnp
from jax import lax
from jax.experimental import pallas as pl
from jax.experimental.pallas import tpu as pltpu

F32 = jnp.float32
BF16 = jnp.bfloat16

HEAD_DIM = 64
HEAD_SHIFT = 6
LANES = 128
SUBLANES = 8
RMS_EPS = 1e-6
GN_EPS = 64e-5
KK_EPS = 1e-24
SB_SCALE = HEAD_DIM ** -0.5
SB_EXP_ZERO = 105.0

V7X_VMEM_BYTES = 64 * 1024 * 1024
VMEM_LIMIT_CAP = V7X_VMEM_BYTES - 8 * 1024 * 1024

MLP_TM = 1024
MLP_TF = 2048
RWKV_TS = 512
CONV_TS = 1024
QKV_TM = 1024
QKV_SUM_PIECES = 1
WKV_CHUNK = 64
WKV_ROWS = 8
WKV_GROUP_ROWS = 2
WKV_PASSES = 1
WKV_CUM_PIECES = 2
SB_TQ = 256
SB_ROWS = 256
SB_LANE_TILES = 8


def _vmem_limit(block_bytes, scratch_bytes, temp_bytes):
    return int(min(2 * block_bytes + scratch_bytes + temp_bytes, VMEM_LIMIT_CAP))


def _nbytes(shape, dtype):
    n = 1
    for s in shape:
        n *= s
    return n * jnp.dtype(dtype).itemsize


def _dot(a, b):
    return jnp.dot(a, b, preferred_element_type=F32)


def _dot_nt(a, b):
    return lax.dot_general(a, b, (((1,), (1,)), ((), ())), preferred_element_type=F32)


def _dot_tn(a, b):
    return lax.dot_general(a, b, (((0,), (0,)), ((), ())), preferred_element_type=F32)


def _split(x, pieces):
    out = []
    for _ in range(pieces - 1):
        hi = x.astype(BF16)
        out.append(hi)
        x = x - hi.astype(F32)
    out.append(x.astype(BF16))
    return out


def _dot_exact_rhs(x, m_bf16, pieces, dot=_dot):
    acc = None
    for p in _split(x, pieces):
        t = dot(p, m_bf16)
        acc = t if acc is None else acc + t
    return acc


def _dot_exact_lhs(m_bf16, x, pieces):
    acc = None
    for p in _split(x, pieces):
        t = _dot(m_bf16, p)
        acc = t if acc is None else acc + t
    return acc


def _mm(a, b, passes, dot=_dot):
    if passes == 1:
        return dot(a.astype(BF16), b.astype(BF16))
    ah, al = _split(a, 2)
    bh, bl = _split(b, 2)
    return dot(ah, bh) + (dot(al, bh) + dot(ah, bl))


def _rms(x, g):
    ms = jnp.mean(x * x, axis=-1, keepdims=True)
    return x * lax.rsqrt(ms + RMS_EPS) * g


def _sigmoid(x):
    return 1.0 / (1.0 + jnp.exp(-x))


def _softplus(x):
    return jnp.maximum(x, 0.0) + jnp.log(1.0 + jnp.exp(-jnp.abs(x)))


def _head_sum_matrix(n):
    r = lax.broadcasted_iota(jnp.int32, (n, n), 0) >> HEAD_SHIFT
    c = lax.broadcasted_iota(jnp.int32, (n, n), 1) >> HEAD_SHIFT
    return jnp.where(r == c, 1.0, 0.0).astype(BF16)


def _shift_rows(x, carry8, n):
    rows = lax.broadcasted_iota(jnp.int32, (x.shape[0], 1), 0)
    out = pltpu.roll(x, n, axis=0)
    for i in range(n):
        out = jnp.where(rows == i, carry8[SUBLANES - n + i:SUBLANES - n + i + 1, :], out)
    return out


def _rwkv_in_kernel(*refs, has_vres):
    if has_vres:
        (h_ref, vec_ref, mu_ref, wr_ref, wk_ref, wv_ref, w1_ref, w2_ref, a1_ref, a2_ref,
         g1_ref, g2_ref, v1_ref, v2_ref, vf_ref,
         r_out, k_out, v_out, lw_out, a_out, g_out, carry_ref) = refs
    else:
        (h_ref, vec_ref, mu_ref, wr_ref, wk_ref, wv_ref, w1_ref, w2_ref, a1_ref, a2_ref,
         g1_ref, g2_ref,
         r_out, k_out, v_out, lw_out, a_out, g_out, carry_ref) = refs

    @pl.when(pl.program_id(1) == 0)
    def _():
        carry_ref[...] = jnp.zeros_like(carry_ref)

    ts = h_ref.shape[0]
    u = _rms(h_ref[...], vec_ref[0:1, :])
    prev = _shift_rows(u, carry_ref[...], 1)
    carry_ref[...] = u[ts - SUBLANES:, :]
    xx = prev - u

    def mix(i):
        return (u + xx * mu_ref[i:i + 1, :]).astype(BF16)

    hw = jnp.tanh(_dot(mix(1), w1_ref[...])).astype(BF16)
    ha = _dot(mix(4), a1_ref[...]).astype(BF16)
    hg = _sigmoid(_dot(mix(5), g1_ref[...])).astype(BF16)
    xv = mix(3)
    if has_vres:
        hv = _dot(xv, v1_ref[...]).astype(BF16)
    dw = vec_ref[1:2, :] + _dot(hw, w2_ref[...])
    da = vec_ref[2:3, :] + _dot(ha, a2_ref[...])
    g_out[...] = _dot(hg, g2_ref[...])
    if has_vres:
        dv = vec_ref[3:4, :] + _dot(hv, v2_ref[...])

    r_out[...] = _dot(mix(0), wr_ref[...])
    lw_out[...] = -jnp.exp(-_softplus(-dw) - 0.5)
    k_out[...] = _dot(mix(2), wk_ref[...])
    a_out[...] = _sigmoid(da)
    v = _dot(xv, wv_ref[...])
    if has_vres:
        v = v + (vf_ref[...] - v) * _sigmoid(dv)
    v_out[...] = v


def _rwkv_in(h, vecs, mu8, wr, wk, wv, w1, w2, a1, a2, g1, g2, vres):
    b, s, d = h.shape
    ts = min(RWKV_TS, s)
    tok = pl.BlockSpec((None, ts, d), lambda i, j: (i, j, 0))

    def full(x):
        return pl.BlockSpec(x.shape, lambda i, j: (0,) * x.ndim)

    weights = [vecs, mu8, wr, wk, wv, w1, w2, a1, a2, g1, g2]
    args = [h] + weights
    in_specs = [tok] + [full(w) for w in weights]
    if vres is not None:
        v1, v2, v_first = vres
        args += [v1, v2, v_first]
        in_specs += [full(v1), full(v2), tok]
    tok_bytes = _nbytes((ts, d), F32)
    block_bytes = sum(_nbytes(w.shape, w.dtype) for w in args[1:1 + len(weights) + (2 if vres else 0)])
    block_bytes += tok_bytes * (7 + (1 if vres else 0))
    out = jax.ShapeDtypeStruct((b, s, d), F32)
    return pl.pallas_call(
        functools.partial(_rwkv_in_kernel, has_vres=vres is not None),
        grid=(b, s // ts),
        in_specs=in_specs,
        out_specs=[tok] * 6,
        out_shape=[out] * 6,
        scratch_shapes=[pltpu.VMEM((SUBLANES, d), F32)],
        compiler_params=pltpu.CompilerParams(
            dimension_semantics=("parallel", "arbitrary"),
            vmem_limit_bytes=_vmem_limit(block_bytes, _nbytes((SUBLANES, d), F32), 12 * tok_bytes)),
        name="rwkv_in",
    )(*args)


def _wkv_kernel(r_ref, lw_ref, k_ref, v_ref, a_ref, g_ref, pv_ref, y_ref, s_ref, *, passes):
    n_rows, c, d_model = r_ref.shape
    n = 2 * c
    n_tiles = d_model // LANES

    @pl.when(pl.program_id(1) == 0)
    def _():
        s_ref[...] = jnp.zeros_like(s_ref)

    first_head = lax.broadcasted_iota(jnp.int32, (c, LANES), 1) < HEAD_DIM

    def head_sums(x):
        lo = jnp.sum(jnp.where(first_head, x, 0.0), axis=-1, keepdims=True)
        hi = jnp.sum(jnp.where(first_head, 0.0, x), axis=-1, keepdims=True)
        return jnp.where(first_head, lo, hi)

    ti = lax.broadcasted_iota(jnp.int32, (c, c), 0)
    si = lax.broadcasted_iota(jnp.int32, (c, c), 1)
    lincl = jnp.where(si <= ti, 1.0, 0.0).astype(BF16)
    log2c = c.bit_length() - 1
    row = lax.broadcasted_iota(jnp.int32, (n, LANES), 0)
    lane = lax.broadcasted_iota(jnp.int32, (n, LANES), 1)
    own = (row >> log2c) == (lane >> HEAD_SHIFT)
    ri = lax.broadcasted_iota(jnp.int32, (n, 2 * n), 0)
    ci = lax.broadcasted_iota(jnp.int32, (n, 2 * n), 1) & (c - 1)
    tri = ci < (ri & (c - 1)) + (ri >> log2c)
    inv_n = 1.0 / HEAD_DIM
    mm = functools.partial(_mm, passes=passes)
    mm_nt = functools.partial(_mm, passes=passes, dot=_dot_nt)
    mm_tn = functools.partial(_mm, passes=passes, dot=_dot_tn)

    def stack(x):
        return jnp.where(own, jnp.concatenate([x, x], axis=0), 0.0)

    steps = log2c
    rows_per_group = WKV_GROUP_ROWS if n_rows % WKV_GROUP_ROWS == 0 else 1
    groups = [[(bi, slice(t * LANES, (t + 1) * LANES), t)
               for bi in range(g0, g0 + rows_per_group) for t in range(n_tiles)]
              for g0 in range(0, n_rows, rows_per_group)]

    def each(f, *cols):
        return [f(*args) for args in zip(*cols)]

    def pro_keys(units):
        st = {"units": units}
        for name, ref in (("r", r_ref), ("lw", lw_ref), ("k_raw", k_ref), ("v", v_ref), ("a", a_ref)):
            st[name] = [ref[bi, :, sl] for bi, sl, _ in units]

        def param(i):
            return [pv_ref[i:i + 1, sl] for _, sl, _ in units]

        kk = each(lambda x, g: x * g, st["k_raw"], param(0))
        kk_ss = each(lambda x: head_sums(x * x), kk)
        st["kk"] = each(lambda x, ss: x * lax.rsqrt(jnp.maximum(ss, KK_EPS)), kk, kk_ss)
        st["k"] = each(lambda x, ai, g: x * (1.0 + (ai - 1.0) * g), st["k_raw"], st["a"], param(1))
        st["bonus"] = each(lambda ri_, ki, g, vi: head_sums(ri_ * ki * g) * vi, st["r"], st["k"], param(2), st["v"])
        return st

    def pro_cum(st):
        st["cum"] = each(lambda x: _dot_exact_lhs(lincl, x, WKV_CUM_PIECES), st["lw"])
        return st

    def pro_scale(st):
        g_incl = each(jnp.exp, st["cum"])
        g_excl = each(lambda cs, x: jnp.exp(cs - x), st["cum"], st["lw"])
        g_inv = each(lambda cs: jnp.exp(-cs), st["cum"])
        st["g_end"] = each(lambda g: g[c - 1:c, :], g_incl)
        st["ar"] = each(lambda kki, ri_, ge, gi: jnp.concatenate([-kki * ge, ri_ * gi], axis=0),
                        st["kk"], st["r"], g_excl, g_incl)
        st["bt"] = each(lambda kki, ai, gv: kki * ai * gv, st["kk"], st["a"], g_inv)
        st["kt"] = each(lambda ki, gv: ki * gv, st["k"], g_inv)
        return st

    def pro_stack(st):
        st["bks"] = each(lambda b_, k_: jnp.concatenate([stack(b_), stack(k_)], axis=0), st["bt"], st["kt"])
        st["vs"] = each(stack, st["v"])
        return st

    def pro_products(st):
        st["s0"] = [s_ref[bi, t] for bi, _, t in st["units"]]
        st["prod"] = each(lambda x, y: jnp.where(tri, mm_nt(x, y), 0.0), st["ar"], st["bks"])
        st["from_s0"] = each(mm_nt, st["ar"], st["s0"])
        return st

    pro_pieces = [pro_keys, pro_cum, pro_scale, pro_stack, pro_products]
    assert len(pro_pieces) <= steps

    def solve_start(st):
        x = each(lambda fs, pr, vs: fs[:c, :] + mm(pr[:c, n:], vs), st["from_s0"], st["prod"], st["vs"])
        return x, [pr[:c, :n] for pr in st["prod"]]

    def solve_stage(x, p, last):
        if last:
            return each(lambda xi, pi: xi + mm(pi, stack(xi)), x, p), None
        px = each(lambda xi, pi: mm(pi, jnp.concatenate([stack(xi), stack(pi)], axis=1)), x, p)
        return each(lambda xi, pxi: xi + pxi[:, :LANES], x, px), each(lambda pxi: pxi[:, LANES:], px)

    def finish(st, x):
        st["y"] = each(lambda fs, pr, vs, u: fs[c:, :] + mm(pr[c:, :], jnp.concatenate([stack(u), vs], axis=0)),
                       st["from_s0"], st["prod"], st["vs"], x)
        outer = each(lambda u, v, bt, kt, ge: mm_tn(jnp.concatenate([u, v], axis=0),
                                                    jnp.concatenate([bt * ge, kt * ge], axis=0)),
                     x, st["v"], st["bt"], st["kt"], st["g_end"])
        for (bi, _, t), s0, ge, o in zip(st["units"], st["s0"], st["g_end"], outer):
            s_ref[bi, t] = s0 * ge + jnp.where(own, o, 0.0)

    def epilogue(st, lo, hi):
        y = st["y"][lo:hi]
        mean = each(lambda yi: head_sums(yi) * inv_n, y)
        yc = each(lambda yi, m: yi - m, y, mean)
        var = each(lambda yi: head_sums(yi * yi) * inv_n, yc)
        for (bi, sl, _), bonus, yi, vr in zip(st["units"][lo:hi], st["bonus"][lo:hi], yc, var):
            out = yi * lax.rsqrt(vr + GN_EPS) * pv_ref[3:4, sl] + pv_ref[4:5, sl] + bonus
            y_ref[bi, :, sl] = (out * g_ref[bi, :, sl]).astype(BF16)

    cur = groups[0]
    for piece in pro_pieces:
        cur = piece(cur)
    prev = None
    for gi in range(len(groups)):
        x, p = solve_start(cur)
        nxt = groups[gi + 1] if gi + 1 < len(groups) else None
        n_units = len(groups[gi])
        for i in range(steps):
            x, p = solve_stage(x, p, i == steps - 1)
            if nxt is not None and i < len(pro_pieces):
                nxt = pro_pieces[i](nxt)
            if prev is not None:
                epilogue(prev, i * n_units // steps, (i + 1) * n_units // steps)
        finish(cur, x)
        prev, cur = cur, nxt
    epilogue(prev, 0, len(prev["units"]))


def _wkv(r, lw, k, v, a, g, pvec, passes):
    b, s, d = r.shape
    c = WKV_CHUNK
    rows = math.gcd(b, WKV_ROWS)
    tok = pl.BlockSpec((rows, c, d), lambda i, j: (i, j, 0))
    par = pl.BlockSpec((SUBLANES, d), lambda i, j: (0, 0))
    tok_bytes = _nbytes((rows, c, d), F32)
    sq_bytes = _nbytes((2 * c, 2 * c), F32)
    n_tiles = d // LANES
    return pl.pallas_call(
        functools.partial(_wkv_kernel, passes=passes),
        grid=(b // rows, s // c),
        in_specs=[tok] * 6 + [par],
        out_specs=tok,
        out_shape=jax.ShapeDtypeStruct((b, s, d), BF16),
        scratch_shapes=[pltpu.VMEM((rows, n_tiles, LANES, LANES), F32)],
        compiler_params=pltpu.CompilerParams(
            dimension_semantics=("parallel", "arbitrary"),
            vmem_limit_bytes=_vmem_limit(8 * tok_bytes, rows * n_tiles * sq_bytes, rows * n_tiles * 48 * sq_bytes)),
        name="wkv",
    )(r, lw, k, v, a, g, pvec)


def _conv_in_kernel(h_ref, g_ref, win_ref, cw_ref, z_out, carry_ref):
    @pl.when(pl.program_id(1) == 0)
    def _():
        carry_ref[...] = jnp.zeros_like(carry_ref)

    ts, d = h_ref.shape
    u = _rms(h_ref[...], g_ref[...]).astype(BF16)
    z = _dot(u, win_ref[:, d:2 * d]) * _dot(u, win_ref[:, 2 * d:])
    b_gate = _dot(u, win_ref[:, :d])
    carry = carry_ref[...]
    z1 = _shift_rows(z, carry, 1)
    z2 = _shift_rows(z, carry, 2)
    carry_ref[...] = z[ts - SUBLANES:, :]
    zc = z2 * cw_ref[0:1, :] + z1 * cw_ref[1:2, :] + z * cw_ref[2:3, :]
    z_out[...] = (b_gate * zc).astype(BF16)


def _conv_in(h, g, w_in, cw8):
    b, s, d = h.shape
    ts = min(CONV_TS, s)
    tok = pl.BlockSpec((None, ts, d), lambda i, j: (i, j, 0))
    tok_bytes = _nbytes((ts, d), F32)
    block_bytes = 2 * tok_bytes + _nbytes(w_in.shape, BF16) + _nbytes(cw8.shape, F32)
    return pl.pallas_call(
        _conv_in_kernel,
        grid=(b, s // ts),
        in_specs=[tok,
                  pl.BlockSpec(g.shape, lambda i, j: (0, 0)),
                  pl.BlockSpec(w_in.shape, lambda i, j: (0, 0)),
                  pl.BlockSpec(cw8.shape, lambda i, j: (0, 0))],
        out_specs=tok,
        out_shape=jax.ShapeDtypeStruct((b, s, d), BF16),
        scratch_shapes=[pltpu.VMEM((SUBLANES, d), F32)],
        compiler_params=pltpu.CompilerParams(
            dimension_semantics=("parallel", "arbitrary"),
            vmem_limit_bytes=_vmem_limit(block_bytes, _nbytes((SUBLANES, d), F32), 10 * tok_bytes)),
        name="conv_in",
    )(h, g, w_in, cw8)


def _sb_qkv_kernel(h_ref, g_ref, w_ref, qg_ref, kg_ref, q_out, k_out, v_out):
    d = h_ref.shape[1]
    u = _rms(h_ref[...], g_ref[...]).astype(BF16)
    hsum = _head_sum_matrix(2 * LANES)
    inv_n = 1.0 / HEAD_DIM

    def head_norm(x, gain):
        cols = []
        for j in range(0, d, 2 * LANES):
            xs = x[:, j:j + 2 * LANES]
            ms = _dot_exact_rhs(xs * xs, hsum, QKV_SUM_PIECES) * inv_n
            cols.append(xs * lax.rsqrt(ms + RMS_EPS))
        return jnp.concatenate(cols, axis=1) * gain

    q = _dot(u, w_ref[:, :d])
    k = _dot(u, w_ref[:, d:2 * d])
    q_out[...] = head_norm(q, qg_ref[...]).astype(BF16)
    v = _dot(u, w_ref[:, 2 * d:])
    k_out[...] = head_norm(k, kg_ref[...]).astype(BF16)
    v_out[...] = v.astype(BF16)


def _sb_qkv(h2, g, w_qkv, qg, kg):
    t, d = h2.shape
    tm = min(QKV_TM, t)
    tok = pl.BlockSpec((tm, d), lambda i: (i, 0))
    vec = pl.BlockSpec((1, d), lambda i: (0, 0))
    tok_bytes = _nbytes((tm, d), F32)
    block_bytes = 4 * tok_bytes + _nbytes(w_qkv.shape, BF16) + 3 * _nbytes((1, d), F32)
    out = jax.ShapeDtypeStruct((t, d), BF16)
    return pl.pallas_call(
        _sb_qkv_kernel,
        grid=(t // tm,),
        in_specs=[tok, vec, pl.BlockSpec(w_qkv.shape, lambda i: (0, 0)), vec, vec],
        out_specs=[tok] * 3,
        out_shape=[out] * 3,
        compiler_params=pltpu.CompilerParams(
            dimension_semantics=("parallel",),
            vmem_limit_bytes=_vmem_limit(block_bytes, 0, 10 * tok_bytes)),
        name="sb_qkv",
    )(h2, g, w_qkv, qg, kg)


def _sb_attn_kernel(q_ref, k_ref, v_ref, suffix_ref, o_ref,
                    qs_ref, acc_ref, run_ref, za_ref, zb_ref, att_ref):
    tq = q_ref.shape[0]
    n_lt = q_ref.shape[1] // LANES
    qb = pl.program_id(2)
    rows = lax.broadcasted_iota(jnp.int32, (2 * tq, LANES), 0)
    lanes = lax.broadcasted_iota(jnp.int32, (2 * tq, LANES), 1)
    own = (rows >> (tq.bit_length() - 1)) == (lanes >> HEAD_SHIFT)
    lts = [slice(p * LANES, (p + 1) * LANES) for p in range(n_lt)]
    for p, sl in enumerate(lts):
        q = q_ref[:, sl].astype(F32)
        qs_ref[p] = (jnp.where(own, jnp.concatenate([q, q], axis=0), 0.0) * SB_SCALE).astype(BF16)
    suffix = suffix_ref[...]

    groups = [slice(g * SB_ROWS, (g + 1) * SB_ROWS) for g in range(2 * tq // SB_ROWS)]
    units = [(p, g) for p in range(n_lt) for g in groups]

    def each(f, *cols):
        return [f(*args) for args in zip(*cols)]

    def key_tile(ref, j, p):
        return ref[pl.ds(pl.multiple_of(j * tq, tq), tq), lts[p]]

    def scores(j, z_out):
        for p in range(n_lt):
            ks = key_tile(k_ref, j, p)
            for g in groups:
                z_out[p, g, :] = _dot_nt(qs_ref[p, g, :], ks)

    def add_values(j):
        pv = []
        for p in range(n_lt):
            vs = key_tile(v_ref, j, p)
            pv += [_dot(att_ref[p, g, :], vs) for g in groups]
        return pv

    def step(j, z_in, z_out, diagonal):
        if not diagonal:
            pv = add_values(j + 1)
        if diagonal:
            t_loc = lax.broadcasted_iota(jnp.int32, (SB_ROWS, tq), 0)
            s_loc = lax.broadcasted_iota(jnp.int32, (SB_ROWS, tq), 1)
            causal = [s_loc < t_loc + (g.start & (tq - 1)) for _, g in units]
        spb = []
        for i, (p, g) in enumerate(units):
            z = z_in[p, g, :]
            sp = _softplus(z)
            z_in[p, g, :] = z - sp
            if diagonal:
                sp = jnp.where(causal[i], sp, 0.0)
            spb.append(sp.astype(BF16))
        tail = each(lambda x: _dot(x, suffix), spb)
        if z_out is not None:
            scores(jnp.maximum(j - 1, 0), z_out)
        for i, (p, g) in enumerate(units):
            total = jnp.broadcast_to(tail[i][:, 0:1] + spb[i][:, 0:1].astype(F32), (SB_ROWS, LANES))
            if diagonal:
                att = jnp.where(causal[i], jnp.exp(z_in[p, g, :] - tail[i]), 0.0)
                run_ref[p, g, :] = total
                acc_ref[p, g, :] = jnp.zeros((SB_ROWS, LANES), F32)
            else:
                run = run_ref[p, g, :]
                att = jnp.exp(z_in[p, g, :] - tail[i] - jnp.concatenate([run] * (tq // LANES), axis=1))
                run_ref[p, g, :] = run + total
                acc_ref[p, g, :] += pv[i]
            att_ref[p, g, :] = att.astype(BF16)

    stop_at = SB_EXP_ZERO

    def run_min():
        return jnp.min(run_ref[...])

    scores(qb, za_ref)
    step(qb, za_ref, zb_ref, True)

    @pl.when(qb >= 1)
    def _():
        step(qb - 1, zb_ref, None, False)

    def more(state):
        j, low = state
        return jnp.logical_and(j >= 1, low < stop_at)

    def pair(state):
        j, _ = state
        scores(j, za_ref)
        step(j, za_ref, zb_ref, False)
        step(j - 1, zb_ref, None, False)
        return j - 2, run_min()

    j, low = lax.while_loop(more, pair, (jnp.maximum(qb - 2, -1), run_min()))
    last_tile = jnp.logical_and(j == 0, low < stop_at)

    @pl.when(last_tile)
    def _():
        scores(0, za_ref)
        step(0, za_ref, None, False)

    pv = add_values(jnp.where(last_tile, 0, j + 1))
    for (p, g), pvi in zip(units, pv):
        acc_ref[p, g, :] += pvi
    lane = lax.broadcasted_iota(jnp.int32, (tq, LANES), 1)
    for p, sl in enumerate(lts):
        o_ref[:, sl] = jnp.where(lane >= HEAD_DIM, acc_ref[p, tq:, :], acc_ref[p, :tq, :]).astype(BF16)


def _sb_attn(q, k, v):
    b, s, d = q.shape
    tq = SB_TQ
    n_lt = min(SB_LANE_TILES, d // LANES)
    width = n_lt * LANES
    ri = lax.broadcasted_iota(jnp.int32, (tq, tq), 0)
    ci = lax.broadcasted_iota(jnp.int32, (tq, tq), 1)
    suffix = jnp.where(ri > ci, 1.0, 0.0).astype(BF16)
    qspec = pl.BlockSpec((None, tq, width), lambda i, p, j: (i, j, p))
    kspec = pl.BlockSpec((None, s, width), lambda i, p, j: (i, 0, p))
    block_bytes = 2 * _nbytes((tq, width), BF16) + 2 * _nbytes((s, width), BF16) + _nbytes((tq, tq), BF16)
    scratch = [pltpu.VMEM((n_lt, 2 * tq, LANES), BF16),
               pltpu.VMEM((n_lt, 2 * tq, LANES), F32),
               pltpu.VMEM((n_lt, 2 * tq, LANES), F32),
               pltpu.VMEM((n_lt, 2 * tq, tq), F32),
               pltpu.VMEM((n_lt, 2 * tq, tq), F32),
               pltpu.VMEM((n_lt, 2 * tq, tq), BF16)]
    scratch_bytes = n_lt * (_nbytes((2 * tq, LANES), BF16) + 2 * _nbytes((2 * tq, LANES), F32)
                            + 2 * _nbytes((2 * tq, tq), F32) + _nbytes((2 * tq, tq), BF16))
    return pl.pallas_call(
        _sb_attn_kernel,
        grid=(b, d // width, s // tq),
        in_specs=[qspec, kspec, kspec, pl.BlockSpec((tq, tq), lambda i, p, j: (0, 0))],
        out_specs=qspec,
        out_shape=jax.ShapeDtypeStruct((b, s, d), BF16),
        scratch_shapes=scratch,
        compiler_params=pltpu.CompilerParams(
            dimension_semantics=("parallel", "parallel", "arbitrary"),
            vmem_limit_bytes=_vmem_limit(block_bytes, scratch_bytes, 12 * _nbytes((2 * tq, tq), F32))),
        name="sb_attn",
    )(q, k, v, suffix)


def _outproj_mlp_kernel(h_ref, z_ref, wo_ref, g_ref, wup_ref, wdn_ref, o_ref, xn_ref):
    @pl.when(pl.program_id(1) == 0)
    def _():
        h1 = h_ref[...] + _dot(z_ref[...], wo_ref[...])
        o_ref[...] = h1
        xn_ref[...] = _rms(h1, g_ref[...]).astype(BF16)

    act = jnp.square(jnp.maximum(_dot(xn_ref[...], wup_ref[...]), 0.0)).astype(BF16)
    o_ref[...] += _dot(act, wdn_ref[...])


def _outproj_mlp(h2, z, w_o, g, w_up_all, w_dn_all, layer):
    t, d = h2.shape
    f = w_up_all.shape[2]
    tm = min(MLP_TM, t)
    tf = min(MLP_TF, f)
    tok = pl.BlockSpec((tm, d), lambda i, j: (i, 0))
    in_specs = [
        tok, tok,
        pl.BlockSpec((d, d), lambda i, j: (0, 0)),
        pl.BlockSpec((1, d), lambda i, j: (0, 0)),
        pl.BlockSpec((None, d, tf), lambda i, j: (layer, 0, j)),
        pl.BlockSpec((None, tf, d), lambda i, j: (layer, j, 0)),
    ]
    tok_bytes = _nbytes((tm, d), F32)
    block_bytes = 2 * tok_bytes + _nbytes((tm, d), BF16) + _nbytes((d, d), BF16) + 2 * _nbytes((d, tf), BF16)
    return pl.pallas_call(
        _outproj_mlp_kernel,
        grid=(t // tm, f // tf),
        in_specs=in_specs,
        out_specs=tok,
        out_shape=jax.ShapeDtypeStruct((t, d), F32),
        scratch_shapes=[pltpu.VMEM((tm, d), BF16)],
        compiler_params=pltpu.CompilerParams(
            dimension_semantics=("parallel", "arbitrary"),
            vmem_limit_bytes=_vmem_limit(block_bytes, _nbytes((tm, d), BF16),
                                         tok_bytes + 2 * _nbytes((tm, tf), F32))),
        name="outproj_mlp",
    )(h2, z, w_o, g, w_up_all, w_dn_all)


def _pad_cols(w, mult):
    pad = (-w.shape[1]) % mult
    return jnp.pad(w, ((0, 0), (0, pad)))


def _pad_rows(w, mult):
    pad = (-w.shape[0]) % mult
    return jnp.pad(w, ((0, pad), (0, 0)))


def _rows8(*rows):
    d = rows[0].shape[-1]
    pad = jnp.zeros((SUBLANES - len(rows), d), F32)
    return jnp.concatenate([r.reshape(1, d).astype(F32) for r in rows] + [pad], axis=0)


def _lora(w_in, w_out):
    return _pad_cols(w_in, LANES).astype(BF16), _pad_rows(w_out, LANES).astype(BF16)


def kernel(x, mix_norm, mlp_norm, mlp_up, mlp_down, rwkv_mu, rwkv_w_r, rwkv_w_k, rwkv_w_v, rwkv_w_o, rwkv_decay_w0, rwkv_decay_w1, rwkv_decay_w2, rwkv_iclr_a0, rwkv_iclr_a1, rwkv_iclr_a2, rwkv_gate_g1, rwkv_gate_g2, rwkv_k_k, rwkv_k_a, rwkv_r_k, rwkv_lnx_w, rwkv_lnx_b, rwkv_vres_v0, rwkv_vres_v1, rwkv_vres_v2, conv_w_in, conv_w, conv_w_out, sb_w_qkv, sb_q_norm, sb_k_norm, sb_w_o):
    b, s, d = x.shape
    depth = mix_norm.shape[0]
    n_heads = d // HEAD_DIM
    mlp_up_bf = mlp_up.astype(BF16)
    mlp_down_bf = mlp_down.astype(BF16)
    h = x
    v_first = None
    for i in range(depth):
        kind = i % 3
        j = i // 3
        g_mix = mix_norm[i].reshape(1, d)
        if kind == 0:
            vres = None
            v0 = jnp.zeros((d,), F32)
            if j > 0:
                v1, v2 = _lora(rwkv_vres_v1[j - 1], rwkv_vres_v2[j - 1])
                vres = (v1, v2, v_first)
                v0 = rwkv_vres_v0[j - 1]
            vecs = _rows8(mix_norm[i], rwkv_decay_w0[j], rwkv_iclr_a0[j], v0)
            mu8 = _rows8(*[rwkv_mu[j, m] for m in range(rwkv_mu.shape[1])])
            w1, w2 = _lora(rwkv_decay_w1[j], rwkv_decay_w2[j])
            a1, a2 = _lora(rwkv_iclr_a1[j], rwkv_iclr_a2[j])
            g1, g2 = _lora(rwkv_gate_g1[j], rwkv_gate_g2[j])
            r, k, v, lw, a, gate = _rwkv_in(
                h, vecs, mu8, rwkv_w_r[j].astype(BF16), rwkv_w_k[j].astype(BF16),
                rwkv_w_v[j].astype(BF16), w1, w2, a1, a2, g1, g2, vres)
            if j == 0:
                v_first = v
            pvec = _rows8(rwkv_k_k[j], rwkv_k_a[j], rwkv_r_k[j].reshape(d), rwkv_lnx_w[j], rwkv_lnx_b[j])
            z = _wkv(r, lw, k, v, a, gate, pvec, WKV_PASSES)
            w_o = rwkv_w_o[j]
        elif kind == 1:
            z = _conv_in(h, g_mix, conv_w_in[j].astype(BF16), _rows8(*[conv_w[j, m] for m in range(3)]))
            w_o = conv_w_out[j]
        else:
            qg = jnp.tile(sb_q_norm[j], n_heads).reshape(1, d)
            kg = jnp.tile(sb_k_norm[j], n_heads).reshape(1, d)
            q, k, v = _sb_qkv(h.reshape(b * s, d), g_mix, sb_w_qkv[j].astype(BF16), qg, kg)
            z = _sb_attn(q.reshape(b, s, d), k.reshape(b, s, d), v.reshape(b, s, d))
            w_o = sb_w_o[j]
        h = _outproj_mlp(
            h.reshape(b * s, d), z.reshape(b * s, d), w_o.astype(BF16), mlp_norm[i].reshape(1, d), mlp_up_bf, mlp_down_bf, i,
        ).reshape(b, s, d)
    return h
```
